```python
import jax, jax.numpy as jnp
from jax import lax
import numpy as np

D_MODEL = 1024
BATCH = 8
SEQ = 2048
DEPTH = 1

CTX_LEN = 256
GRID_W = 64
EPS = 1e-6

SSD_HEAD_DIM = 64
D_SSM = D_MODEL // 2
SSD_HEADS = D_SSM // SSD_HEAD_DIM
SSD_GROUPS = 2
SSD_STATE = 128
SSD_CONV = 5
SSD_CHUNK = 128
D_XB = D_SSM + SSD_GROUPS * SSD_STATE
D_XBC = D_SSM + 2 * SSD_GROUPS * SSD_STATE

POOL_WINDOWS = (2, 4, 8, 16)
POOL_GROUPS = len(POOL_WINDOWS)
D_POOL = D_MODEL // 2
POOL_GROUP_DIM = D_POOL // POOL_GROUPS
D_MIX = D_SSM + D_POOL

OFF_XBC = D_SSM
OFF_DT = OFF_XBC + D_XBC
OFF_POOL = OFF_DT + 2 * SSD_HEADS
D_IN = OFF_POOL + D_POOL

D_FF = ((8 * D_MODEL // 3 + 127) // 128) * 128
FFN_CONV = 3

kernel_name = "hybrid_ssd_pool_prefix_dit_block"


def rmsnorm(x, g):
    xf = x.astype(jnp.float32)
    y = xf * lax.rsqrt(jnp.mean(xf * xf, axis=-1, keepdims=True) + EPS)
    return (y * g.astype(jnp.float32)).astype(x.dtype)


def modulate(x, shift, scale):
    return x * (1 + scale) + shift


def dwconv(x, w, b):
    k = w.shape[0]
    y = lax.conv_general_dilated(
        x, w[:, None, :].astype(x.dtype), window_strides=(1,),
        padding=[(k // 2, k // 2)], dimension_numbers=("NWC", "WIO", "NWC"),
        feature_group_count=x.shape[-1])
    return y + b.astype(x.dtype)


def segsum_exp(cs):
    t = cs.shape[-1]
    diff = cs[..., :, None] - cs[..., None, :]
    mask = jnp.tril(jnp.ones((t, t), dtype=bool))
    return jnp.exp(jnp.where(mask, diff, -jnp.inf))


def ssd_dt(dt_raw, dt_bias, a_log):
    b, n = dt_raw.shape[:2]
    dt = jax.nn.softplus(dt_raw.astype(jnp.float32).reshape(b, n, 2, SSD_HEADS) + dt_bias.astype(jnp.float32))
    a = -jnp.exp(a_log.astype(jnp.float32))
    return dt, a


def ssd_chunked(xs, dt, a, bm, cm, h0):
    b, l, h, p = xs.shape
    g, n = bm.shape[-2:]
    k = h // g
    L = SSD_CHUNK
    c = l // L
    a_cs = jnp.cumsum((dt * a).reshape(b, c, L, h).transpose(0, 3, 1, 2), axis=-1)
    xd = (xs * dt[..., None]).reshape(b, c, L, g, k, p)
    bc = bm.reshape(b, c, L, g, n)
    cc = cm.reshape(b, c, L, g, n)
    decay_in = segsum_exp(a_cs).reshape(b, g, k, c, L, L)
    cb = jnp.einsum("bclgn,bcsgn->bcgls", cc, bc)
    y_diag = jnp.einsum("bcgls,bgkcls,bcsgkp->bclgkp", cb, decay_in, xd)
    decay_st = jnp.exp(a_cs[..., -1:] - a_cs).reshape(b, g, k, c, L)
    st = jnp.einsum("bclgn,bgkcl,bclgkp->bcgkpn", bc, decay_st, xd).reshape(b, c, h, p, n)
    st = jnp.concatenate([h0[:, None].astype(jnp.float32), st], axis=1)
    a_chunk = jnp.pad(a_cs[..., -1], ((0, 0), (0, 0), (1, 0)))
    decay_ch = segsum_exp(jnp.cumsum(a_chunk, axis=-1))
    st = jnp.einsum("bhzc,bchpn->bzhpn", decay_ch, st)
    prev, final = st[:, :-1], st[:, -1]
    decay_out = jnp.exp(a_cs).reshape(b, g, k, c, L)
    y_off = jnp.einsum("bclgn,bcgkpn,bgkcl->bclgkp", cc, prev.reshape(b, c, g, k, p, n), decay_out)
    return (y_diag + y_off).reshape(b, l, h, p), final


def ssd_final_state(xs, dt, a, bm):
    b, l, h, p = xs.shape
    g, n = bm.shape[-2:]
    k = h // g
    a_cs = jnp.cumsum(dt * a, axis=1)
    w = jnp.exp(a_cs[:, -1:] - a_cs) * dt
    st = jnp.einsum("blgn,blgk,blgkp->bgkpn", bm, w.reshape(b, l, g, k), xs.reshape(b, l, g, k, p))
    return st.reshape(b, h, p, n)


def box_mean(x, w, axis):
    n = x.shape[axis]
    cs = jnp.cumsum(x, axis=axis)
    pad = [(0, 0)] * x.ndim
    pad[axis] = (1, 0)
    cs = jnp.pad(cs, pad)
    t = np.arange(n)
    lo = np.clip(t - w // 2, 0, n)
    hi = np.clip(t + w - w // 2, 0, n)
    shape = [1] * x.ndim
    shape[axis] = n
    cnt = jnp.asarray((hi - lo).astype(np.float32).reshape(shape))
    return (jnp.take(cs, jnp.asarray(hi), axis=axis) - jnp.take(cs, jnp.asarray(lo), axis=axis)) / cnt


def pool_mixer(u, w_pool, pool_scale, rows):
    b, n = u.shape[:2]
    ug = u.reshape(b, n, POOL_GROUPS, POOL_GROUP_DIM)
    outs = []
    for gi, w in enumerate(POOL_WINDOWS):
        v = ug[:, :, gi]
        if rows is None:
            m = box_mean(v, w, 1)
        else:
            v2 = v.reshape(b, rows, GRID_W, POOL_GROUP_DIM)
            m = box_mean(box_mean(v2, w, 1), w, 2).reshape(b, n, POOL_GROUP_DIM)
        outs.append(m - v)
    d = jnp.stack(outs, axis=2)
    y = jnp.einsum("bngc,gcd->bngd", d, w_pool.astype(jnp.float32)).reshape(b, n, D_POOL)
    return y * pool_scale.astype(jnp.float32)


def token_mixers(h, w_in, conv_w, conv_b, dt_bias, a_log, d_skip, ssm_g, w_pool, pool_scale, w_out,
                 rows, h0_f, h0_b):
    b, n = h.shape[:2]
    proj = h @ w_in
    z = proj[..., :OFF_XBC].astype(jnp.float32)
    xbc = jax.nn.silu(dwconv(proj[..., OFF_XBC:OFF_DT], conv_w, conv_b)).astype(jnp.float32)
    xs = xbc[..., :D_SSM].reshape(b, n, SSD_HEADS, SSD_HEAD_DIM)
    bm = xbc[..., D_SSM:D_XB].reshape(b, n, SSD_GROUPS, SSD_STATE)
    cm = xbc[..., D_XB:].reshape(b, n, SSD_GROUPS, SSD_STATE)
    dt, a = ssd_dt(proj[..., OFF_DT:OFF_POOL], dt_bias, a_log)
    y_f, s_f = ssd_chunked(xs, dt[:, :, 0], a[0], bm, cm, h0_f)
    y_b, s_b = ssd_chunked(xs[:, ::-1], dt[:, ::-1, 1], a[1], bm[:, ::-1], cm[:, ::-1], h0_b)
    y = y_f + y_b[:, ::-1] + d_skip.astype(jnp.float32)[:, None] * xs
    y = rmsnorm(y.reshape(b, n, D_SSM) * jax.nn.silu(z), ssm_g)
    p = pool_mixer(proj[..., OFF_POOL:].astype(jnp.float32), w_pool, pool_scale, rows)
    mix = jnp.concatenate([y, p], axis=-1).astype(h.dtype)
    return mix @ w_out, s_f, s_b


def context_scan_states(hc, w_in, conv_w, conv_b, dt_bias, a_log):
    b, n = hc.shape[:2]
    xb = jax.nn.silu(dwconv(hc @ w_in[:, OFF_XBC:OFF_XBC + D_XB], conv_w[:, :D_XB], conv_b[:D_XB]))
    xb = xb.astype(jnp.float32)
    xs = xb[..., :D_SSM].reshape(b, n, SSD_HEADS, SSD_HEAD_DIM)
    bm = xb[..., D_SSM:].reshape(b, n, SSD_GROUPS, SSD_STATE)
    dt, a = ssd_dt(hc @ w_in[:, OFF_DT:OFF_POOL], dt_bias, a_log)
    s_f = ssd_final_state(xs, dt[:, :, 0], a[0], bm)
    s_b = ssd_final_state(xs[:, ::-1], dt[:, ::-1, 1], a[1], bm[:, ::-1])
    return s_f, s_b


def conv_ffn(h, w_up, conv_w, conv_b, w_down):
    up = dwconv(h @ w_up, conv_w, conv_b)
    v, g = jnp.split(up, 2, axis=-1)
    return (jax.nn.silu(g) * v) @ w_down


def setup_inputs(seed: int = 0) -> dict:
    key = jax.random.key(seed)
    ks = jax.random.split(key, 26)
    f = jnp.float32
    L = DEPTH

    def nrm(k, shape, scale):
        return jax.random.normal(k, shape, f) * scale

    def gain(k, shape):
        return 1.0 + 0.02 * jax.random.normal(k, shape, f)

    dt0 = jnp.exp(jax.random.uniform(ks[14], (L, 2, SSD_HEADS), f, float(np.log(1e-3)), float(np.log(1e-1))))
    return {
        "x": nrm(ks[0], (BATCH, SEQ, D_MODEL), 1.0),
        "c": nrm(ks[1], (BATCH, D_MODEL), 1.0),
        "ctx": nrm(ks[2], (BATCH, CTX_LEN, D_MODEL), 1.0),
        "c_ctx": nrm(ks[3], (D_MODEL,), 1.0),
        "pre_norm_mix": gain(ks[4], (L, D_MODEL)),
        "post_norm_mix": gain(ks[5], (L, D_MODEL)),
        "pre_norm_ffn": gain(ks[6], (L, D_MODEL)),
        "post_norm_ffn": gain(ks[7], (L, D_MODEL)),
        "w_ada": nrm(ks[8], (L, D_MODEL, 6 * D_MODEL), 0.5 * D_MODEL ** -0.5),
        "b_ada": nrm(ks[9], (L, 6 * D_MODEL), 0.01),
        "w_in": nrm(ks[10], (L, D_MODEL, D_IN), D_MODEL ** -0.5),
        "conv_ssd_w": nrm(ks[11], (L, SSD_CONV, D_XBC), SSD_CONV ** -0.5),
        "conv_ssd_b": nrm(ks[12], (L, D_XBC), 0.01),
        "dt_bias": dt0 + jnp.log(-jnp.expm1(-dt0)),
        "a_log": jnp.log(jax.random.uniform(ks[13], (L, 2, SSD_HEADS), f, 1.0, 16.0)),
        "d_skip": 1.0 + 0.1 * jax.random.normal(ks[15], (L, SSD_HEADS), f),
        "ssm_norm": gain(ks[16], (L, D_SSM)),
        "w_pool": nrm(ks[17], (L, POOL_GROUPS, POOL_GROUP_DIM, POOL_GROUP_DIM), POOL_GROUP_DIM ** -0.5),
        "pool_scale": 1.0 + 0.1 * jax.random.normal(ks[18], (L, D_POOL), f),
        "w_out": nrm(ks[19], (L, D_MIX, D_MODEL), D_MIX ** -0.5),
        "w_up": nrm(ks[20], (L, D_MODEL, 2 * D_FF), D_MODEL ** -0.5),
        "conv_ffn_w": nrm(ks[21], (L, FFN_CONV, 2 * D_FF), FFN_CONV ** -0.5),
        "conv_ffn_b": nrm(ks[22], (L, 2 * D_FF), 0.01),
        "w_down": nrm(ks[23], (L, D_FF, D_MODEL), D_FF ** -0.5),
    }


def reference(x, c, ctx, c_ctx, pre_norm_mix, post_norm_mix, pre_norm_ffn, post_norm_ffn, w_ada, b_ada,
              w_in, conv_ssd_w, conv_ssd_b, dt_bias, a_log, d_skip, ssm_norm, w_pool, pool_scale, w_out,
              w_up, conv_ffn_w, conv_ffn_b, w_down):
    b = x.shape[0]
    rows = x.shape[1] // GRID_W
    for i in range(DEPTH):
        mx = jnp.split((jax.nn.silu(c) @ w_ada[i] + b_ada[i])[:, None, :], 6, axis=-1)
        mc = jnp.split(jax.nn.silu(c_ctx) @ w_ada[i] + b_ada[i], 6, axis=-1)

        hc = modulate(rmsnorm(ctx, pre_norm_mix[i]), mc[0], mc[1])
        if i < DEPTH - 1:
            zeros = jnp.zeros((b, SSD_HEADS, SSD_HEAD_DIM, SSD_STATE), jnp.float32)
            yc, s_f, s_b = token_mixers(hc, w_in[i], conv_ssd_w[i], conv_ssd_b[i], dt_bias[i], a_log[i],
                                        d_skip[i], ssm_norm[i], w_pool[i], pool_scale[i], w_out[i],
                                        None, zeros, zeros)
            ctx = ctx + mc[2] * rmsnorm(yc, post_norm_mix[i])
            hc2 = modulate(rmsnorm(ctx, pre_norm_ffn[i]), mc[3], mc[4])
            ctx = ctx + mc[5] * rmsnorm(conv_ffn(hc2, w_up[i], conv_ffn_w[i], conv_ffn_b[i], w_down[i]),
                                        post_norm_ffn[i])
        else:
            s_f, s_b = context_scan_states(hc, w_in[i], conv_ssd_w[i], conv_ssd_b[i], dt_bias[i], a_log[i])

        hx = modulate(rmsnorm(x, pre_norm_mix[i]), mx[0], mx[1])
        yx, _, _ = token_mixers(hx, w_in[i], conv_ssd_w[i], conv_ssd_b[i], dt_bias[i], a_log[i],
                                d_skip[i], ssm_norm[i], w_pool[i], pool_scale[i], w_out[i],
                                rows, s_f, s_b)
        x = x + mx[2] * rmsnorm(yx, post_norm_mix[i])
        hx = modulate(rmsnorm(x, pre_norm_ffn[i]), mx[3], mx[4])
        x = x + mx[5] * rmsnorm(conv_ffn(hx, w_up[i], conv_ffn_w[i], conv_ffn_b[i], w_down[i]),
                                post_norm_ffn[i])
    return x
```

```python
import functools

import numpy as np
import jax
import jax.numpy as jnp
from jax import lax
from jax.experimental import pallas as pl
from jax.experimental.pallas import tpu as pltpu

F32 = jnp.float32
BF16 = jnp.bfloat16

EPS = 1e-6
GRID_W = 64
SSD_HEADS = 8
SSD_HEAD_DIM = 64
SSD_GROUPS = 2
SSD_STATE = 128
D_SSM = SSD_HEADS * SSD_HEAD_DIM
D_XB = D_SSM + SSD_GROUPS * SSD_STATE
D_XBC = D_SSM + 2 * SSD_GROUPS * SSD_STATE
POOL_WINDOWS = (2, 4, 8, 16)
D_POOL = 512
CHUNK = 128
LANES = 128
CONV_ROWS = 64
VMEM_LIMIT = 56 * 1024 * 1024


def _silu(v):
    return v / (1.0 + jnp.exp(-v))


def _rms(v, gain):
    ms = jnp.mean(v * v, axis=-1, keepdims=True)
    return v * lax.rsqrt(ms + EPS) * gain


def _ada_kernel(c_ref, w_ref, b_ref, o_ref):
    s = _silu(c_ref[...]).astype(BF16)
    o_ref[...] = jnp.dot(s, w_ref[...].astype(BF16), preferred_element_type=F32) + b_ref[...]


def _ada(cc, w_ada, b_ada):
    rows, d = cc.shape
    n = w_ada.shape[1]
    tn = 1536
    return pl.pallas_call(
        _ada_kernel,
        grid=(n // tn,),
        in_specs=[pl.BlockSpec((rows, d), lambda j: (0, 0)),
                  pl.BlockSpec((d, tn), lambda j: (0, j)),
                  pl.BlockSpec((1, tn), lambda j: (0, j))],
        out_specs=pl.BlockSpec((rows, tn), lambda j: (0, j)),
        out_shape=jax.ShapeDtypeStruct((rows, n), F32),
        compiler_params=pltpu.CompilerParams(dimension_semantics=("arbitrary",),
                                             vmem_limit_bytes=VMEM_LIMIT),
        name="ada",
    )(cc, w_ada, b_ada)


def _inproj_kernel(x_ref, shift_ref, scale_ref, g_ref, w_ref, wdt_ref, proj_ref, dt_ref, *, col_chunk):
    h = _rms(x_ref[0], g_ref[...])
    h = (h * (1.0 + scale_ref[0]) + shift_ref[0]).astype(BF16)
    ncols = w_ref.shape[1]
    for j in range(ncols // col_chunk):
        cols = slice(j * col_chunk, (j + 1) * col_chunk)
        proj_ref[0, :, cols] = jnp.dot(h, w_ref[:, cols], preferred_element_type=F32).astype(BF16)
    dt_ref[0] = jnp.dot(h, wdt_ref[...], preferred_element_type=F32)


def _inproj(x, shift, scale, gain, w, wdt, tm, col_chunk, name):
    b, n, d = x.shape
    ncols = w.shape[1]
    return pl.pallas_call(
        functools.partial(_inproj_kernel, col_chunk=col_chunk),
        grid=(b, n // tm),
        in_specs=[pl.BlockSpec((1, tm, d), lambda i, j: (i, j, 0)),
                  pl.BlockSpec((1, 1, d), lambda i, j: (i, 0, 0)),
                  pl.BlockSpec((1, 1, d), lambda i, j: (i, 0, 0)),
                  pl.BlockSpec((1, d), lambda i, j: (0, 0)),
                  pl.BlockSpec((d, ncols), lambda i, j: (0, 0)),
                  pl.BlockSpec((d, LANES), lambda i, j: (0, 0))],
        out_specs=[pl.BlockSpec((1, tm, ncols), lambda i, j: (i, j, 0)),
                   pl.BlockSpec((1, tm, LANES), lambda i, j: (i, j, 0))],
        out_shape=[jax.ShapeDtypeStruct((b, n, ncols), BF16),
                   jax.ShapeDtypeStruct((b, n, LANES), F32)],
        compiler_params=pltpu.CompilerParams(dimension_semantics=("arbitrary", "arbitrary"),
                                             vmem_limit_bytes=VMEM_LIMIT),
        name=name,
    )(x, shift, scale, gain, w, wdt)


def _conv_silu(load_rows, pad_ref, dst_ref, w_ref, b_ref, n_rows, n_ch):
    ktaps = w_ref.shape[0]
    half = ktaps // 2
    pad_ref[0:8, :] = jnp.zeros((8, n_ch), F32)
    pad_ref[n_rows + 8:n_rows + 16, :] = jnp.zeros((8, n_ch), F32)
    for c in range(n_rows // CONV_ROWS):
        r0 = c * CONV_ROWS
        pad_ref[8 + r0:8 + r0 + CONV_ROWS, :] = load_rows(r0, r0 + CONV_ROWS).astype(F32)
    for c in range(n_rows // CONV_ROWS):
        r0 = c * CONV_ROWS
        acc = b_ref[:, 0:n_ch] + w_ref[0:1, 0:n_ch] * pad_ref[8 + r0 - half:8 + r0 - half + CONV_ROWS, :]
        for k in range(1, ktaps):
            lo = 8 + r0 + k - half
            acc = acc + w_ref[k:k + 1, 0:n_ch] * pad_ref[lo:lo + CONV_ROWS, :]
        dst_ref[r0:r0 + CONV_ROWS, :] = _silu(acc).astype(BF16)


def _chunk_rows(dtraw, bias16, a16):
    raw = dtraw.T[0:2 * SSD_HEADS, :] + bias16
    dt = jnp.maximum(raw, 0.0) + jnp.log1p(jnp.exp(-jnp.abs(raw)))
    dta = dt * a16
    lane = lax.broadcasted_iota(jnp.int32, dta.shape, 1)
    cs = dta
    k = 1
    while k < CHUNK:
        cs = cs + jnp.where(lane >= k, pltpu.roll(cs, k, 1), 0.0)
        k *= 2
    tot = jnp.sum(jnp.where(lane == CHUNK - 1, cs, 0.0), axis=1, keepdims=True)
    h = SSD_HEADS
    dtf, dtb = dt[0:h], dt[h:2 * h]
    pf = cs[0:h]
    qb = cs[h:2 * h] - dta[h:2 * h]
    tf, tb = tot[0:h], tot[h:2 * h]
    wf = jnp.exp(tf - pf) * dtf
    wb = jnp.exp(qb) * dtb
    return dtf, dtb, pf, qb, tf, tb, wf, wb


def _pair_rhs(xs, p):
    xp = xs[:, p * LANES:(p + 1) * LANES]
    lane = lax.broadcasted_iota(jnp.int32, xp.shape, 1)
    zero = jnp.zeros_like(xp)
    return jnp.concatenate([jnp.where(lane < SSD_HEAD_DIM, xp, zero),
                            jnp.where(lane >= SSD_HEAD_DIM, xp, zero)], axis=0)


def _pair_select(lane_row, v0, v1):
    return jnp.where(lane_row < SSD_HEAD_DIM, v0, v1)


def _state_lhs(bt, w, p):
    h0, h1 = 2 * p, 2 * p + 1
    return jnp.concatenate([(bt * w[h0:h0 + 1, :]).astype(BF16),
                            (bt * w[h1:h1 + 1, :]).astype(BF16)], axis=1)


def _state_only_step(xs, bm, w, tot, s_ref):
    lane_row = lax.broadcasted_iota(jnp.int32, (1, LANES), 1)
    cd = jnp.exp(tot)
    for g in range(SSD_GROUPS):
        bt = bm[:, g * SSD_STATE:(g + 1) * SSD_STATE].astype(F32).T
        for q in range(2):
            p = 2 * g + q
            ds = jnp.dot(_state_lhs(bt, w, p), _pair_rhs(xs, p), preferred_element_type=F32)
            cdrow = _pair_select(lane_row, cd[2 * p:2 * p + 1, :], cd[2 * p + 1:2 * p + 2, :])
            s_ref[p] = s_ref[p] * cdrow + ds


def _ssd_kernel(xbc_ref, z_ref, dt_ref, cxb_ref, cdt_ref, cw_ref, cb_ref, bias_ref, alog_ref,
                dsk_ref, g_ref, y_ref,
                pad_scr, xbc_scr, cpad_scr, cxb_scr, sf_scr, sb_scr, sball_scr):
    n = xbc_ref.shape[1]
    n_ctx = cxb_ref.shape[1]
    nchunks = n // CHUNK

    _conv_silu(lambda r0, r1: xbc_ref[0, r0:r1, :], pad_scr, xbc_scr, cw_ref, cb_ref, n, D_XBC)
    _conv_silu(lambda r0, r1: cxb_ref[0, r0:r1, :], cpad_scr, cxb_scr, cw_ref, cb_ref, n_ctx, D_XB)

    bias16 = bias_ref[...]
    a16 = -jnp.exp(alog_ref[...])

    sf_scr[...] = jnp.zeros(sf_scr.shape, F32)
    sb_scr[...] = jnp.zeros(sb_scr.shape, F32)

    ctx_chunks = n_ctx // CHUNK
    for c in range(ctx_chunks):
        rows = slice(c * CHUNK, (c + 1) * CHUNK)
        _, _, _, _, tf, _, wf, _ = _chunk_rows(cdt_ref[0, rows, :], bias16, a16)
        _state_only_step(cxb_scr[rows, 0:D_SSM], cxb_scr[rows, D_SSM:D_XB], wf, tf, sf_scr)
    for c in reversed(range(ctx_chunks)):
        rows = slice(c * CHUNK, (c + 1) * CHUNK)
        _, _, _, _, _, tb, _, wb = _chunk_rows(cdt_ref[0, rows, :], bias16, a16)
        _state_only_step(cxb_scr[rows, 0:D_SSM], cxb_scr[rows, D_SSM:D_XB], wb, tb, sb_scr)

    def bwd_body(i, carry):
        c = nchunks - 1 - i
        rows = pl.ds(pl.multiple_of(c * CHUNK, CHUNK), CHUNK)
        _, _, _, _, _, tb, _, wb = _chunk_rows(dt_ref[0, rows, :], bias16, a16)
        for p in range(SSD_HEADS // 2):
            sball_scr[c, p] = sb_scr[p].astype(BF16)
        _state_only_step(xbc_scr[rows, 0:D_SSM], xbc_scr[rows, D_SSM:D_XB], wb, tb, sb_scr)
        return carry

    lax.fori_loop(0, nchunks, bwd_body, 0)

    row_i = lax.broadcasted_iota(jnp.int32, (CHUNK, CHUNK), 0)
    col_i = lax.broadcasted_iota(jnp.int32, (CHUNK, CHUNK), 1)
    lane_row = lax.broadcasted_iota(jnp.int32, (1, LANES), 1)
    lane_full = lax.broadcasted_iota(jnp.int32, (CHUNK, LANES), 1)

    def fwd_body(c, carry):
        rows = pl.ds(pl.multiple_of(c * CHUNK, CHUNK), CHUNK)
        xs = xbc_scr[rows, 0:D_SSM]
        bm = xbc_scr[rows, D_SSM:D_XB]
        cm = xbc_scr[rows, D_XB:D_XBC]
        dtf, dtb, pf, qb, tf, tb, wf, _ = _chunk_rows(dt_ref[0, rows, :], bias16, a16)
        stacked = jnp.concatenate([pf, qb, jnp.zeros((LANES - 2 * SSD_HEADS, CHUNK), F32)], axis=0)
        cols = stacked.T

        def colb(j):
            return jnp.sum(jnp.where(lane_full == j, cols, 0.0), axis=1, keepdims=True)

        cdf = jnp.exp(tf)
        y_pairs = []
        for g in range(SSD_GROUPS):
            bt = bm[:, g * SSD_STATE:(g + 1) * SSD_STATE].astype(F32).T
            cg = cm[:, g * SSD_STATE:(g + 1) * SSD_STATE]
            gmat = jnp.dot(cg, bt.astype(BF16), preferred_element_type=F32)
            scat = jnp.concatenate([sf_scr[2 * g].astype(BF16), sf_scr[2 * g + 1].astype(BF16),
                                    sball_scr[c, 2 * g], sball_scr[c, 2 * g + 1]], axis=1)
            yoff = jnp.dot(cg, scat, preferred_element_type=F32)
            for q in range(2):
                p = 2 * g + q
                ms, efs, ebs = [], [], []
                for h in (2 * p, 2 * p + 1):
                    pf_c = colb(h)
                    qb_c = colb(SSD_HEADS + h)
                    pf_r, qb_r = pf[h:h + 1, :], qb[h:h + 1, :]
                    dtf_r, dtb_r = dtf[h:h + 1, :], dtb[h:h + 1, :]
                    dm = jnp.where(row_i >= col_i, pf_c - pf_r, qb_r - qb_c)
                    dtw = jnp.where(row_i > col_i, dtf_r, jnp.where(row_i < col_i, dtb_r, dtf_r + dtb_r))
                    ms.append((gmat * jnp.exp(dm) * dtw).astype(BF16))
                    efs.append(jnp.exp(pf_c))
                    ebs.append(jnp.exp(tb[h:h + 1, :] - qb_c))
                lhs = jnp.concatenate([jnp.concatenate(ms, axis=1), _state_lhs(bt, wf, p)], axis=0)
                res = jnp.dot(lhs, _pair_rhs(xs, p), preferred_element_type=F32)
                ef = _pair_select(lane_full, efs[0], efs[1])
                eb = _pair_select(lane_full, ebs[0], ebs[1])
                y_pairs.append(res[0:CHUNK] + ef * yoff[:, q * LANES:(q + 1) * LANES]
                               + eb * yoff[:, (2 + q) * LANES:(3 + q) * LANES])
                cdrow = _pair_select(lane_row, cdf[2 * p:2 * p + 1, :], cdf[2 * p + 1:2 * p + 2, :])
                sf_scr[p] = sf_scr[p] * cdrow + res[CHUNK:2 * CHUNK]
        y = jnp.concatenate(y_pairs, axis=1) + dsk_ref[...] * xs.astype(F32)
        yz = y * _silu(z_ref[0, rows, :].astype(F32))
        y_ref[0, rows, :] = _rms(yz, g_ref[...]).astype(BF16)
        return carry

    lax.fori_loop(0, nchunks, fwd_body, 0)


def _ssd(proj, dtraw, cproj, cdtraw, conv_w, conv_b, bias16, alog16, dsk, gain):
    b, n, _ = proj.shape
    n_ctx = cproj.shape[1]
    npairs = SSD_HEADS // 2
    return pl.pallas_call(
        _ssd_kernel,
        grid=(b,),
        in_specs=[pl.BlockSpec((1, n, D_XBC), lambda i: (i, 0, 0)),
                  pl.BlockSpec((1, n, D_SSM), lambda i: (i, 0, 2)),
                  pl.BlockSpec((1, n, LANES), lambda i: (i, 0, 0)),
                  pl.BlockSpec((1, n_ctx, D_XB), lambda i: (i, 0, 0)),
                  pl.BlockSpec((1, n_ctx, LANES), lambda i: (i, 0, 0)),
                  pl.BlockSpec(conv_w.shape, lambda i: (0, 0)),
                  pl.BlockSpec(conv_b.shape, lambda i: (0, 0)),
                  pl.BlockSpec(bias16.shape, lambda i: (0, 0)),
                  pl.BlockSpec(alog16.shape, lambda i: (0, 0)),
                  pl.BlockSpec(dsk.shape, lambda i: (0, 0)),
                  pl.BlockSpec(gain.shape, lambda i: (0, 0))],
        out_specs=pl.BlockSpec((1, n, D_SSM), lambda i: (i, 0, 0)),
        out_shape=jax.ShapeDtypeStruct((b, n, D_SSM), BF16),
        scratch_shapes=[pltpu.VMEM((n + 16, D_XBC), F32),
                        pltpu.VMEM((n, D_XBC), BF16),
                        pltpu.VMEM((n_ctx + 16, D_XB), F32),
                        pltpu.VMEM((n_ctx, D_XB), BF16),
                        pltpu.VMEM((npairs, SSD_STATE, LANES), F32),
                        pltpu.VMEM((npairs, SSD_STATE, LANES), F32),
                        pltpu.VMEM((n // CHUNK, npairs, SSD_STATE, LANES), BF16)],
        compiler_params=pltpu.CompilerParams(dimension_semantics=("arbitrary",),
                                             vmem_limit_bytes=VMEM_LIMIT),
        name="ssd",
    )(proj, proj, dtraw, cproj, cdtraw, conv_w, conv_b, bias16, alog16, dsk, gain)


def _pool_out_kernel(y_ref, u_ref, x_ref, gate_ref, gn_ref, wpool_ref, pscale_ref, wout_ref,
                     cm_ref, inv_ref, o_ref, d_scr):
    n = u_ref.shape[1]
    tm = x_ref.shape[1]
    j = pl.program_id(1)

    @pl.when(j == 0)
    def _():
        for gi, w in enumerate(POOL_WINDOWS):
            cols = slice(gi * LANES, (gi + 1) * LANES)
            cmat = cm_ref[gi]
            parts = [jnp.dot(cmat, u_ref[0, i * LANES:(i + 1) * LANES, cols], preferred_element_type=F32)
                     for i in range(n // LANES)]
            cs = jnp.concatenate(parts, axis=0)
            acc = None
            for jj in range(-(w // 2), w - w // 2):
                sh = jj * GRID_W
                if sh == 0:
                    t = cs
                elif sh > 0:
                    t = jnp.concatenate([cs[sh:], jnp.zeros((sh, LANES), F32)], axis=0)
                else:
                    t = jnp.concatenate([jnp.zeros((-sh, LANES), F32), cs[:n + sh]], axis=0)
                acc = t if acc is None else acc + t
            m = acc * inv_ref[gi]
            d_scr[:, cols] = (m - u_ref[0, :, cols].astype(F32)).astype(BF16)

    rows = pl.ds(pl.multiple_of(j * tm, tm), tm)
    ps = [jnp.dot(d_scr[rows, gi * LANES:(gi + 1) * LANES], wpool_ref[gi], preferred_element_type=F32)
          for gi in range(len(POOL_WINDOWS))]
    p = jnp.concatenate(ps, axis=1) * pscale_ref[...]
    mix = jnp.concatenate([y_ref[0], p.astype(BF16)], axis=1)
    yx = jnp.dot(mix, wout_ref[...], preferred_element_type=F32)
    o_ref[0] = x_ref[0] + gate_ref[0] * _rms(yx, gn_ref[...])


def _pool_constants(n):
    rows = n // GRID_W
    t = np.arange(LANES)
    blk, col = t // GRID_W, t % GRID_W
    cms, invs = [], []
    r = np.arange(n) // GRID_W
    c = np.arange(n) % GRID_W
    for w in POOL_WINDOWS:
        d = col[None, :] - col[:, None]
        cms.append(((blk[:, None] == blk[None, :]) & (d >= -(w // 2)) & (d < w - w // 2)).astype(np.float32))
        cnt_r = np.minimum(r + w - w // 2, rows) - np.maximum(r - w // 2, 0)
        cnt_c = np.minimum(c + w - w // 2, GRID_W) - np.maximum(c - w // 2, 0)
        invs.append(np.broadcast_to((1.0 / (cnt_r * cnt_c))[:, None], (n, LANES)).astype(np.float32))
    return np.stack(cms), np.stack(invs)


def _pool_out(y, proj, x, gate, gn, wpool, pscale, wout, tm):
    b, n, d = x.shape
    cm_np, inv_np = _pool_constants(n)
    cm = jnp.asarray(cm_np, BF16)
    inv = jnp.asarray(inv_np, F32)
    return pl.pallas_call(
        _pool_out_kernel,
        grid=(b, n // tm),
        in_specs=[pl.BlockSpec((1, tm, D_SSM), lambda i, j: (i, j, 0)),
                  pl.BlockSpec((1, n, D_POOL), lambda i, j: (i, 0, 3)),
                  pl.BlockSpec((1, tm, d), lambda i, j: (i, j, 0)),
                  pl.BlockSpec((1, 1, d), lambda i, j: (i, 0, 0)),
                  pl.BlockSpec((1, d), lambda i, j: (0, 0)),
                  pl.BlockSpec(wpool.shape, lambda i, j: (0, 0, 0)),
                  pl.BlockSpec(pscale.shape, lambda i, j: (0, 0)),
                  pl.BlockSpec(wout.shape, lambda i, j: (0, 0)),
                  pl.BlockSpec(cm.shape, lambda i, j: (0, 0, 0)),
                  pl.BlockSpec(inv.shape, lambda i, j: (0, 0, 0))],
        out_specs=pl.BlockSpec((1, tm, d), lambda i, j: (i, j, 0)),
        out_shape=jax.ShapeDtypeStruct((b, n, d), F32),
        scratch_shapes=[pltpu.VMEM((n, D_POOL), BF16)],
        compiler_params=pltpu.CompilerParams(dimension_semantics=("arbitrary", "arbitrary"),
                                             vmem_limit_bytes=VMEM_LIMIT),
        name="pool_out",
    )(y, proj, x, gate, gn, wpool, pscale, wout, cm, inv)


FFN_HALO = 8
FFN_COLS = 256


def _ffn_kernel(x_ref, xp_ref, xn_ref, shift_ref, scale_ref, gate_ref, gpre_ref, gpost_ref,
                wup_ref, cw_ref, cb_ref, wdown_ref, o_ref, act_scr):
    tm = x_ref.shape[1]
    dff = wdown_ref.shape[0]
    j = pl.program_id(1)
    nj = pl.num_programs(1)

    def hmod(v):
        return _rms(v, gpre_ref[...]) * (1.0 + scale_ref[0]) + shift_ref[0]

    x = x_ref[0]
    hp = jnp.where(j > 0, hmod(xp_ref[0]), 0.0)
    hn = jnp.where(j < nj - 1, hmod(xn_ref[0]), 0.0)
    hext = jnp.concatenate([hp, hmod(x), hn], axis=0).astype(BF16)

    lo = FFN_HALO
    for cidx in range(dff // FFN_COLS):
        cv = slice(cidx * FFN_COLS, (cidx + 1) * FFN_COLS)
        cg = slice(dff + cidx * FFN_COLS, dff + (cidx + 1) * FFN_COLS)

        def conv(cols):
            up = jnp.dot(hext, wup_ref[:, cols], preferred_element_type=F32)
            return (cb_ref[:, cols]
                    + cw_ref[0:1, cols] * up[lo - 1:lo - 1 + tm]
                    + cw_ref[1:2, cols] * up[lo:lo + tm]
                    + cw_ref[2:3, cols] * up[lo + 1:lo + 1 + tm])

        act_scr[:, cv] = (_silu(conv(cg)) * conv(cv)).astype(BF16)

    f = jnp.dot(act_scr[...], wdown_ref[...], preferred_element_type=F32)
    o_ref[0] = x + gate_ref[0] * _rms(f, gpost_ref[...])


def _ffn(x, shift, scale, gate, gpre, gpost, wup, cw, cb, wdown, tm):
    b, n, d = x.shape
    dff = wdown.shape[0]
    hb = tm // FFN_HALO
    nb = n // FFN_HALO
    return pl.pallas_call(
        _ffn_kernel,
        grid=(b, n // tm),
        in_specs=[pl.BlockSpec((1, tm, d), lambda i, j: (i, j, 0)),
                  pl.BlockSpec((1, FFN_HALO, d), lambda i, j: (i, jnp.maximum(j * hb - 1, 0), 0)),
                  pl.BlockSpec((1, FFN_HALO, d), lambda i, j: (i, jnp.minimum((j + 1) * hb, nb - 1), 0)),
                  pl.BlockSpec((1, 1, d), lambda i, j: (i, 0, 0)),
                  pl.BlockSpec((1, 1, d), lambda i, j: (i, 0, 0)),
                  pl.BlockSpec((1, 1, d), lambda i, j: (i, 0, 0)),
                  pl.BlockSpec((1, d), lambda i, j: (0, 0)),
                  pl.BlockSpec((1, d), lambda i, j: (0, 0)),
                  pl.BlockSpec(wup.shape, lambda i, j: (0, 0), pipeline_mode=pl.Buffered(1)),
                  pl.BlockSpec(cw.shape, lambda i, j: (0, 0)),
                  pl.BlockSpec(cb.shape, lambda i, j: (0, 0)),
                  pl.BlockSpec(wdown.shape, lambda i, j: (0, 0), pipeline_mode=pl.Buffered(1))],
        out_specs=pl.BlockSpec((1, tm, d), lambda i, j: (i, j, 0)),
        out_shape=jax.ShapeDtypeStruct((b, n, d), F32),
        scratch_shapes=[pltpu.VMEM((tm, dff), BF16)],
        compiler_params=pltpu.CompilerParams(dimension_semantics=("arbitrary", "arbitrary"),
                                             vmem_limit_bytes=VMEM_LIMIT),
        name="ffn",
    )(x, x, x, shift, scale, gate, gpre, gpost, wup, cw, cb, wdown)


def kernel(x, c, ctx, c_ctx, pre_norm_mix, post_norm_mix, pre_norm_ffn, post_norm_ffn, w_ada, b_ada,
           w_in, conv_ssd_w, conv_ssd_b, dt_bias, a_log, d_skip, ssm_norm, w_pool, pool_scale, w_out,
           w_up, conv_ffn_w, conv_ffn_b, w_down):
    assert w_ada.shape[0] == 1, "single-layer block"
    b, n, d = x.shape
    n_ctx = ctx.shape[1]
    off_dt = D_SSM + D_XBC
    off_pool = off_dt + 2 * SSD_HEADS

    cc = jnp.zeros((16, d), F32).at[:b].set(c).at[b].set(c_ctx)
    mods = _ada(cc, w_ada[0], b_ada).reshape(16, 6, d)
    mx = [mods[:b, k][:, None, :] for k in range(6)]
    mc = [jnp.broadcast_to(mods[b, k][None, None, :], (b, 1, d)) for k in range(2)]

    wi = w_in[0]
    w_main = jnp.concatenate([wi[:, D_SSM:off_dt], wi[:, :D_SSM], wi[:, off_pool:]], axis=1).astype(BF16)
    w_dt = jnp.zeros((d, LANES), F32).at[:, :2 * SSD_HEADS].set(wi[:, off_dt:off_pool]).astype(BF16)
    w_ctx = wi[:, D_SSM:D_SSM + D_XB].astype(BF16)

    proj, dtraw = _inproj(x, mx[0], mx[1], pre_norm_mix, w_main, w_dt, 1024, 512, "inproj")
    cproj, cdtraw = _inproj(ctx, mc[0], mc[1], pre_norm_mix, w_ctx, w_dt, n_ctx, 256, "inproj_ctx")

    bias16 = jnp.broadcast_to(dt_bias[0].reshape(2 * SSD_HEADS, 1), (2 * SSD_HEADS, CHUNK))
    alog16 = jnp.broadcast_to(a_log[0].reshape(2 * SSD_HEADS, 1), (2 * SSD_HEADS, CHUNK))
    dsk = jnp.repeat(d_skip[0], SSD_HEAD_DIM)[None, :]
    y = _ssd(proj, dtraw, cproj, cdtraw, conv_ssd_w[0], conv_ssd_b, bias16, alog16, dsk, ssm_norm)

    x1 = _pool_out(y, proj, x, mx[2], post_norm_mix, w_pool[0].astype(BF16), pool_scale,
                   w_out[0].astype(BF16), 512)

    return _ffn(x1, mx[3], mx[4], mx[5], pre_norm_ffn, post_norm_ffn, w_up[0].astype(BF16),
                conv_ffn_w[0], conv_ffn_b, w_down[0].astype(BF16), 512)
```

```python
import functools

import numpy as np
import jax
import jax.numpy as jnp
from jax import lax
from jax.experimental import pallas as pl
from jax.experimental.pallas import tpu as pltpu

F32 = jnp.float32
BF16 = jnp.bfloat16

EPS = 1e-6
GRID_W = 64
SSD_HEADS = 8
SSD_HEAD_DIM = 64
SSD_GROUPS = 2
SSD_STATE = 128
SSD_PAIRS = SSD_HEADS // 2
D_SSM = SSD_HEADS * SSD_HEAD_DIM
D_XB = D_SSM + SSD_GROUPS * SSD_STATE
D_XBC = D_SSM + 2 * SSD_GROUPS * SSD_STATE
POOL_WINDOWS = (2, 4, 8, 16)
D_POOL = 512
CHUNK = 128
LANES = 128
SUBLANES = 8
VMEM_LIMIT = 56 * 1024 * 1024


def _silu(v):
    return v / (1.0 + jnp.exp(-v))


def _rms(v, gain):
    ms = jnp.mean(v * v, axis=-1, keepdims=True)
    return v * lax.rsqrt(ms + EPS) * gain


def _ada_kernel(c_ref, w_ref, b_ref, o_ref):
    s = _silu(c_ref[...]).astype(BF16)
    o_ref[...] = jnp.dot(s, w_ref[...].astype(BF16), preferred_element_type=F32) + b_ref[...]


def _ada(cc, w_ada, b_ada):
    rows, d = cc.shape
    n = w_ada.shape[1]
    tn = 1536
    return pl.pallas_call(
        _ada_kernel,
        grid=(n // tn,),
        in_specs=[pl.BlockSpec((rows, d), lambda j: (0, 0)),
                  pl.BlockSpec((d, tn), lambda j: (0, j)),
                  pl.BlockSpec((1, tn), lambda j: (0, j))],
        out_specs=pl.BlockSpec((rows, tn), lambda j: (0, j)),
        out_shape=jax.ShapeDtypeStruct((rows, n), F32),
        compiler_params=pltpu.CompilerParams(dimension_semantics=("arbitrary",),
                                             vmem_limit_bytes=VMEM_LIMIT),
        name="ada",
    )(cc, w_ada, b_ada)


def _inproj_kernel(x_ref, shift_ref, scale_ref, g_ref, w_ref, wdt_ref, proj_ref, dt_ref, *, col_chunk):
    h = _rms(x_ref[0], g_ref[...])
    h = (h * (1.0 + scale_ref[0]) + shift_ref[0]).astype(BF16)
    ncols = w_ref.shape[1]
    for j in range(ncols // col_chunk):
        cols = slice(j * col_chunk, (j + 1) * col_chunk)
        proj_ref[0, :, cols] = jnp.dot(h, w_ref[:, cols], preferred_element_type=F32).astype(BF16)
    dt_ref[0] = jnp.dot(h, wdt_ref[...], preferred_element_type=F32)


def _inproj(x, shift, scale, gain, w, wdt, tm, col_chunk, name):
    b, n, d = x.shape
    ncols = w.shape[1]
    return pl.pallas_call(
        functools.partial(_inproj_kernel, col_chunk=col_chunk),
        grid=(b, n // tm),
        in_specs=[pl.BlockSpec((1, tm, d), lambda i, j: (i, j, 0)),
                  pl.BlockSpec((1, 1, d), lambda i, j: (i, 0, 0)),
                  pl.BlockSpec((1, 1, d), lambda i, j: (i, 0, 0)),
                  pl.BlockSpec((1, d), lambda i, j: (0, 0)),
                  pl.BlockSpec((d, ncols), lambda i, j: (0, 0)),
                  pl.BlockSpec((d, LANES), lambda i, j: (0, 0))],
        out_specs=[pl.BlockSpec((1, tm, ncols), lambda i, j: (i, j, 0)),
                   pl.BlockSpec((1, tm, LANES), lambda i, j: (i, j, 0))],
        out_shape=[jax.ShapeDtypeStruct((b, n, ncols), BF16),
                   jax.ShapeDtypeStruct((b, n, LANES), F32)],
        compiler_params=pltpu.CompilerParams(dimension_semantics=("arbitrary", "arbitrary"),
                                             vmem_limit_bytes=VMEM_LIMIT),
        name=name,
    )(x, shift, scale, gain, w, wdt)


CONV_TAPS = 5
CONV_WIN = 256
CONV_COLS = 256


def _shift_matrices():
    mats = np.zeros((3, CHUNK * CONV_TAPS, CONV_WIN), np.float32)
    for v, off in enumerate((0, 64, 128)):
        for i in range(CHUNK // SUBLANES):
            for j in range(CONV_TAPS):
                for r in range(SUBLANES):
                    src = off + SUBLANES * i + r + j - CONV_TAPS // 2
                    if 0 <= src < CONV_WIN:
                        mats[v, (i * CONV_TAPS + j) * SUBLANES + r, src] = 1.0
    return mats


def _conv_chunk(load_window, smat, w_ref, b_ref, n_colblk, emit):
    for cb in range(n_colblk):
        cols = slice(cb * CONV_COLS, (cb + 1) * CONV_COLS)
        res = jnp.dot(smat, load_window(cols), preferred_element_type=F32)
        bias = b_ref[:, cols]
        wts = [w_ref[j:j + 1, cols] for j in range(CONV_TAPS)]
        outs = []
        for i in range(CHUNK // SUBLANES):
            base = i * CONV_TAPS * SUBLANES
            acc = bias + wts[0] * res[base:base + SUBLANES]
            for j in range(1, CONV_TAPS):
                lo = base + SUBLANES * j
                acc = acc + wts[j] * res[lo:lo + SUBLANES]
            outs.append(acc)
        emit(cb, _silu(jnp.concatenate(outs, axis=0)))


def _dt_rows(dtraw_chunks, bias, a):
    h = SSD_HEADS
    raw = jnp.concatenate([blk.T[0:2 * h, :] for blk in dtraw_chunks], axis=1) + bias
    t = raw.shape[1]
    dt = jnp.maximum(raw, 0.0) + jnp.log1p(jnp.exp(-jnp.abs(raw)))
    dta = dt * a
    seg = lax.broadcasted_iota(jnp.int32, dta.shape, 1) & (CHUNK - 1)
    cs, rcs = dta, dta
    k = 1
    while k < CHUNK:
        cs = cs + jnp.where(seg >= k, pltpu.roll(cs, k, 1), 0.0)
        rcs = rcs + jnp.where(seg < CHUNK - k, pltpu.roll(rcs, t - k, 1), 0.0)
        k *= 2
    cd = jnp.exp(cs + rcs - dta)
    wf = jnp.exp(rcs[0:h] - dta[0:h]) * dt[0:h]
    wb = jnp.exp(cs[h:] - dta[h:]) * dt[h:]
    return [dt[0:h], dt[h:], cs[0:h], rcs[h:], wf, wb, cd[0:h], cd[h:]]


def _pair_rhs(xs, p):
    xp = xs[:, p * LANES:(p + 1) * LANES]
    lane = lax.broadcasted_iota(jnp.int32, xp.shape, 1)
    zero = jnp.zeros_like(xp)
    return jnp.concatenate([jnp.where(lane < SSD_HEAD_DIM, xp, zero),
                            jnp.where(lane >= SSD_HEAD_DIM, xp, zero)], axis=0)


def _pair_select(lane, v0, v1):
    return jnp.where(lane < SSD_HEAD_DIM, v0, v1)


def _local_states(xs, bts, wf, wb):
    out_f, out_b = [], []
    for p in range(SSD_PAIRS):
        bt = bts[p // 2]
        h0, h1 = 2 * p, 2 * p + 1
        lhs = jnp.concatenate(
            [jnp.concatenate([(bt * w[h0:h0 + 1, :]).astype(BF16), (bt * w[h1:h1 + 1, :]).astype(BF16)], axis=1)
             for w in (wf, wb)], axis=0)
        res = jnp.dot(lhs, _pair_rhs(xs, p), preferred_element_type=F32)
        out_f.append(res[0:SSD_STATE])
        out_b.append(res[SSD_STATE:2 * SSD_STATE])
    return out_f, out_b


def _ssd_kernel(xbc_ref, z_ref, dt_ref, cxb_ref, cdt_ref, smat_ref, cw_ref, cb_ref, bias_ref, alog_ref,
                dsk_ref, g_ref, y_ref,
                xs_scr, c_scr, bt_scr, rows_scr, cols_scr, sf_scr, sb_scr, st_scr):
    n = xbc_ref.shape[1]
    n_ctx = cxb_ref.shape[1]
    nchunks = n // CHUNK
    nctx = n_ctx // CHUNK
    a_full = -jnp.exp(alog_ref[...])
    lane_row = lax.broadcasted_iota(jnp.int32, (1, LANES), 1)

    rows = _dt_rows([dt_ref[0, c * CHUNK:(c + 1) * CHUNK, :] for c in range(nchunks)], bias_ref[...], a_full)
    pad = jnp.zeros((LANES - 2 * SSD_HEADS, CHUNK), F32)
    for c in range(nchunks):
        lanes = slice(c * CHUNK, (c + 1) * CHUNK)
        for q, arr in enumerate(rows):
            rows_scr[c, q] = arr[:, lanes]
        cols_scr[c] = jnp.concatenate([rows[2][:, lanes], rows[3][:, lanes], pad], axis=0).T

    crow = _dt_rows([cdt_ref[0, c * CHUNK:(c + 1) * CHUNK, :] for c in range(nctx)],
                    bias_ref[:, 0:n_ctx], a_full[:, 0:n_ctx])
    ctx_f, ctx_b = [], []
    for c in range(nctx):
        lanes = slice(c * CHUNK, (c + 1) * CHUNK)
        parts = {}
        _conv_chunk(lambda cols: cxb_ref[0, :, cols], smat_ref[0 if c == 0 else 2], cw_ref, cb_ref,
                    D_XB // CONV_COLS, lambda cb, s, parts=parts: parts.__setitem__(cb, s))
        xs_c = jnp.concatenate([parts[0], parts[1]], axis=1).astype(BF16)
        bts = [parts[2][:, g * SSD_STATE:(g + 1) * SSD_STATE].T for g in range(SSD_GROUPS)]
        dsf, dsb = _local_states(xs_c, bts, crow[4][:, lanes], crow[5][:, lanes])
        ctx_f.append(dsf)
        ctx_b.append(dsb)
    for p in range(SSD_PAIRS):
        h0, h1 = 2 * p, 2 * p + 1
        sf = ctx_f[0][p]
        for c in range(1, nctx):
            lanes = slice(c * CHUNK, (c + 1) * CHUNK)
            sf = sf * _pair_select(lane_row, crow[6][h0:h0 + 1, lanes], crow[6][h1:h1 + 1, lanes]) + ctx_f[c][p]
        sb = ctx_b[nctx - 1][p]
        for c in reversed(range(nctx - 1)):
            lanes = slice(c * CHUNK, (c + 1) * CHUNK)
            sb = sb * _pair_select(lane_row, crow[7][h0:h0 + 1, lanes], crow[7][h1:h1 + 1, lanes]) + ctx_b[c][p]
        st_scr[0, p] = sf
        st_scr[1, p] = sb

    def local_states_of(c):
        r0 = pl.multiple_of(c * CHUNK, CHUNK)
        dsf, dsb = _local_states(xs_scr[pl.ds(r0, CHUNK), :], [bt_scr[c, 0], bt_scr[c, 1]],
                                 rows_scr[c, 4], rows_scr[c, 5])
        for p in range(SSD_PAIRS):
            sf_scr[c, p] = dsf[p]
            sb_scr[c, p] = dsb[p]

    def conv_of(c, w0, variant):
        r0 = pl.multiple_of(c * CHUNK, CHUNK)

        def emit(cb, s):
            if cb < 2:
                xs_scr[pl.ds(r0, CHUNK), cb * CONV_COLS:(cb + 1) * CONV_COLS] = s.astype(BF16)
            elif cb == 2:
                for g in range(SSD_GROUPS):
                    bt_scr[c, g] = s[:, g * SSD_STATE:(g + 1) * SSD_STATE].T
            else:
                c_scr[pl.ds(r0, CHUNK), :] = s.astype(BF16)

        _conv_chunk(lambda cols: xbc_ref[0, pl.ds(w0, CONV_WIN), cols], smat_ref[variant], cw_ref, cb_ref,
                    D_XBC // CONV_COLS, emit)

    def conv_body(c, carry):
        local_states_of(c - 1)
        conv_of(c, pl.multiple_of(jnp.minimum(c * CHUNK - 64, n - CONV_WIN), 64),
                jnp.where(c == nchunks - 1, 2, 1))
        return carry

    conv_of(0, 0, 0)
    lax.fori_loop(1, nchunks, conv_body, 0)
    local_states_of(nchunks - 1)

    def scan_body(i, carry):
        cf = i
        cbk = nchunks - 1 - i
        for p in range(SSD_PAIRS):
            h0, h1 = 2 * p, 2 * p + 1
            cdf = rows_scr[cf, 6]
            s_old, loc = st_scr[0, p], sf_scr[cf, p]
            sf_scr[cf, p] = s_old
            st_scr[0, p] = s_old * _pair_select(lane_row, cdf[h0:h0 + 1, :], cdf[h1:h1 + 1, :]) + loc
            cdb = rows_scr[cbk, 7]
            s_old, loc = st_scr[1, p], sb_scr[cbk, p]
            sb_scr[cbk, p] = s_old
            st_scr[1, p] = s_old * _pair_select(lane_row, cdb[h0:h0 + 1, :], cdb[h1:h1 + 1, :]) + loc
        return carry

    lax.fori_loop(0, nchunks, scan_body, 0)

    row_i = lax.broadcasted_iota(jnp.int32, (CHUNK, CHUNK), 0)
    col_i = lax.broadcasted_iota(jnp.int32, (CHUNK, CHUNK), 1)
    lane_full = lax.broadcasted_iota(jnp.int32, (CHUNK, LANES), 1)

    def out_body(c, carry):
        r0 = pl.multiple_of(c * CHUNK, CHUNK)
        xs = xs_scr[pl.ds(r0, CHUNK), :]
        cm = c_scr[pl.ds(r0, CHUNK), :]
        dtf, dtb, pf, ab = rows_scr[c, 0], rows_scr[c, 1], rows_scr[c, 2], rows_scr[c, 3]
        cols = cols_scr[c]

        def colb(j):
            return jnp.sum(jnp.where(lane_full == j, cols, 0.0), axis=1, keepdims=True)

        y_pairs = []
        for g in range(SSD_GROUPS):
            cg = cm[:, g * SSD_STATE:(g + 1) * SSD_STATE]
            gmat = jnp.dot(cg, bt_scr[c, g].astype(BF16), preferred_element_type=F32)
            scat = jnp.concatenate([sf_scr[c, 2 * g].astype(BF16), sf_scr[c, 2 * g + 1].astype(BF16),
                                    sb_scr[c, 2 * g].astype(BF16), sb_scr[c, 2 * g + 1].astype(BF16)], axis=1)
            yoff = jnp.dot(cg, scat, preferred_element_type=F32)
            for q in range(2):
                p = 2 * g + q
                ms, efs, ebs = [], [], []
                for h in (2 * p, 2 * p + 1):
                    pf_c = colb(h)
                    ab_c = colb(SSD_HEADS + h)
                    dtf_r, dtb_r = dtf[h:h + 1, :], dtb[h:h + 1, :]
                    dm = jnp.where(row_i >= col_i, pf_c - pf[h:h + 1, :], ab_c - ab[h:h + 1, :])
                    dtw = jnp.where(row_i > col_i, dtf_r, jnp.where(row_i < col_i, dtb_r, dtf_r + dtb_r))
                    ms.append((gmat * jnp.exp(dm) * dtw).astype(BF16))
                    efs.append(jnp.exp(pf_c))
                    ebs.append(jnp.exp(ab_c))
                ydiag = jnp.dot(jnp.concatenate(ms, axis=1), _pair_rhs(xs, p), preferred_element_type=F32)
                y_pairs.append(ydiag
                               + _pair_select(lane_full, efs[0], efs[1]) * yoff[:, q * LANES:(q + 1) * LANES]
                               + _pair_select(lane_full, ebs[0], ebs[1]) * yoff[:, (2 + q) * LANES:(3 + q) * LANES])
        y = jnp.concatenate(y_pairs, axis=1) + dsk_ref[...] * xs.astype(F32)
        yz = y * _silu(z_ref[0, pl.ds(r0, CHUNK), :].astype(F32))
        y_ref[0, pl.ds(r0, CHUNK), :] = _rms(yz, g_ref[...]).astype(BF16)
        return carry

    lax.fori_loop(0, nchunks, out_body, 0)


def _ssd(proj, dtraw, cproj, cdtraw, conv_w, conv_b, bias, alog, dsk, gain):
    b, n, _ = proj.shape
    n_ctx = cproj.shape[1]
    assert n_ctx == CONV_WIN and n % CHUNK == 0 and n >= 2 * CONV_WIN
    nchunks = n // CHUNK
    smat = jnp.asarray(_shift_matrices(), BF16)
    return pl.pallas_call(
        _ssd_kernel,
        grid=(b,),
        in_specs=[pl.BlockSpec((1, n, D_XBC), lambda i: (i, 0, 0)),
                  pl.BlockSpec((1, n, D_SSM), lambda i: (i, 0, 2)),
                  pl.BlockSpec((1, n, LANES), lambda i: (i, 0, 0)),
                  pl.BlockSpec((1, n_ctx, D_XB), lambda i: (i, 0, 0)),
                  pl.BlockSpec((1, n_ctx, LANES), lambda i: (i, 0, 0)),
                  pl.BlockSpec(smat.shape, lambda i: (0, 0, 0)),
                  pl.BlockSpec(conv_w.shape, lambda i: (0, 0)),
                  pl.BlockSpec(conv_b.shape, lambda i: (0, 0)),
                  pl.BlockSpec(bias.shape, lambda i: (0, 0)),
                  pl.BlockSpec(alog.shape, lambda i: (0, 0)),
                  pl.BlockSpec(dsk.shape, lambda i: (0, 0)),
                  pl.BlockSpec(gain.shape, lambda i: (0, 0))],
        out_specs=pl.BlockSpec((1, n, D_SSM), lambda i: (i, 0, 0)),
        out_shape=jax.ShapeDtypeStruct((b, n, D_SSM), BF16),
        scratch_shapes=[pltpu.VMEM((n, D_SSM), BF16),
                        pltpu.VMEM((n, SSD_GROUPS * SSD_STATE), BF16),
                        pltpu.VMEM((nchunks, SSD_GROUPS, SSD_STATE, CHUNK), F32),
                        pltpu.VMEM((nchunks, 8, SSD_HEADS, CHUNK), F32),
                        pltpu.VMEM((nchunks, CHUNK, LANES), F32),
                        pltpu.VMEM((nchunks, SSD_PAIRS, SSD_STATE, LANES), F32),
                        pltpu.VMEM((nchunks, SSD_PAIRS, SSD_STATE, LANES), F32),
                        pltpu.VMEM((2, SSD_PAIRS, SSD_STATE, LANES), F32)],
        compiler_params=pltpu.CompilerParams(dimension_semantics=("arbitrary",),
                                             vmem_limit_bytes=VMEM_LIMIT),
        name="ssd",
    )(proj, proj, dtraw, cproj, cdtraw, smat, conv_w, conv_b, bias, alog, dsk, gain)


def _pool_out_kernel(y_ref, u_ref, x_ref, gate_ref, gn_ref, wpool_ref, pscale_ref, wout_ref,
                     cm_ref, inv_ref, o_ref, d_scr):
    n = u_ref.shape[1]
    tm = x_ref.shape[1]
    j = pl.program_id(1)

    @pl.when(j == 0)
    def _():
        for gi, w in enumerate(POOL_WINDOWS):
            cols = slice(gi * LANES, (gi + 1) * LANES)
            cmat = cm_ref[gi]
            parts = [jnp.dot(cmat, u_ref[0, i * LANES:(i + 1) * LANES, cols], preferred_element_type=F32)
                     for i in range(n // LANES)]
            cs = jnp.concatenate(parts, axis=0)
            acc = None
            for jj in range(-(w // 2), w - w // 2):
                sh = jj * GRID_W
                if sh == 0:
                    t = cs
                elif sh > 0:
                    t = jnp.concatenate([cs[sh:], jnp.zeros((sh, LANES), F32)], axis=0)
                else:
                    t = jnp.concatenate([jnp.zeros((-sh, LANES), F32), cs[:n + sh]], axis=0)
                acc = t if acc is None else acc + t
            m = acc * inv_ref[gi]
            d_scr[:, cols] = (m - u_ref[0, :, cols].astype(F32)).astype(BF16)

    rows = pl.ds(pl.multiple_of(j * tm, tm), tm)
    ps = [jnp.dot(d_scr[rows, gi * LANES:(gi + 1) * LANES], wpool_ref[gi], preferred_element_type=F32)
          for gi in range(len(POOL_WINDOWS))]
    p = jnp.concatenate(ps, axis=1) * pscale_ref[...]
    mix = jnp.concatenate([y_ref[0], p.astype(BF16)], axis=1)
    yx = jnp.dot(mix, wout_ref[...], preferred_element_type=F32)
    o_ref[0] = x_ref[0] + gate_ref[0] * _rms(yx, gn_ref[...])


def _pool_constants(n):
    rows = n // GRID_W
    t = np.arange(LANES)
    blk, col = t // GRID_W, t % GRID_W
    cms, invs = [], []
    r = np.arange(n) // GRID_W
    c = np.arange(n) % GRID_W
    for w in POOL_WINDOWS:
        d = col[None, :] - col[:, None]
        cms.append(((blk[:, None] == blk[None, :]) & (d >= -(w // 2)) & (d < w - w // 2)).astype(np.float32))
        cnt_r = np.minimum(r + w - w // 2, rows) - np.maximum(r - w // 2, 0)
        cnt_c = np.minimum(c + w - w // 2, GRID_W) - np.maximum(c - w // 2, 0)
        invs.append(np.broadcast_to((1.0 / (cnt_r * cnt_c))[:, None], (n, LANES)).astype(np.float32))
    return np.stack(cms), np.stack(invs)


def _pool_out(y, proj, x, gate, gn, wpool, pscale, wout, tm):
    b, n, d = x.shape
    cm_np, inv_np = _pool_constants(n)
    cm = jnp.asarray(cm_np, BF16)
    inv = jnp.asarray(inv_np, F32)
    return pl.pallas_call(
        _pool_out_kernel,
        grid=(b, n // tm),
        in_specs=[pl.BlockSpec((1, tm, D_SSM), lambda i, j: (i, j, 0)),
                  pl.BlockSpec((1, n, D_POOL), lambda i, j: (i, 0, 3)),
                  pl.BlockSpec((1, tm, d), lambda i, j: (i, j, 0)),
                  pl.BlockSpec((1, 1, d), lambda i, j: (i, 0, 0)),
                  pl.BlockSpec((1, d), lambda i, j: (0, 0)),
                  pl.BlockSpec(wpool.shape, lambda i, j: (0, 0, 0)),
                  pl.BlockSpec(pscale.shape, lambda i, j: (0, 0)),
                  pl.BlockSpec(wout.shape, lambda i, j: (0, 0)),
                  pl.BlockSpec(cm.shape, lambda i, j: (0, 0, 0)),
                  pl.BlockSpec(inv.shape, lambda i, j: (0, 0, 0))],
        out_specs=pl.BlockSpec((1, tm, d), lambda i, j: (i, j, 0)),
        out_shape=jax.ShapeDtypeStruct((b, n, d), F32),
        scratch_shapes=[pltpu.VMEM((n, D_POOL), BF16)],
        compiler_params=pltpu.CompilerParams(dimension_semantics=("arbitrary", "arbitrary"),
                                             vmem_limit_bytes=VMEM_LIMIT),
        name="pool_out",
    )(y, proj, x, gate, gn, wpool, pscale, wout, cm, inv)


FFN_HALO = 8
FFN_COLS = 256


def _ffn_kernel(x_ref, xp_ref, xn_ref, shift_ref, scale_ref, gate_ref, gpre_ref, gpost_ref,
                wup_ref, cw_ref, cb_ref, wdown_ref, o_ref, act_scr):
    tm = x_ref.shape[1]
    dff = wdown_ref.shape[0]
    j = pl.program_id(1)
    nj = pl.num_programs(1)

    def hmod(v):
        return _rms(v, gpre_ref[...]) * (1.0 + scale_ref[0]) + shift_ref[0]

    x = x_ref[0]
    hp = jnp.where(j > 0, hmod(xp_ref[0]), 0.0)
    hn = jnp.where(j < nj - 1, hmod(xn_ref[0]), 0.0)
    hext = jnp.concatenate([hp, hmod(x), hn], axis=0).astype(BF16)

    lo = FFN_HALO
    for cidx in range(dff // FFN_COLS):
        cv = slice(cidx * FFN_COLS, (cidx + 1) * FFN_COLS)
        cg = slice(dff + cidx * FFN_COLS, dff + (cidx + 1) * FFN_COLS)

        def conv(cols):
            up = jnp.dot(hext, wup_ref[:, cols], preferred_element_type=F32)
            return (cb_ref[:, cols]
                    + cw_ref[0:1, cols] * up[lo - 1:lo - 1 + tm]
                    + cw_ref[1:2, cols] * up[lo:lo + tm]
                    + cw_ref[2:3, cols] * up[lo + 1:lo + 1 + tm])

        act_scr[:, cv] = (_silu(conv(cg)) * conv(cv)).astype(BF16)

    f = jnp.dot(act_scr[...], wdown_ref[...], preferred_element_type=F32)
    o_ref[0] = x + gate_ref[0] * _rms(f, gpost_ref[...])


def _ffn(x, shift, scale, gate, gpre, gpost, wup, cw, cb, wdown, tm):
    b, n, d = x.shape
    dff = wdown.shape[0]
    hb = tm // FFN_HALO
    nb = n // FFN_HALO
    return pl.pallas_call(
        _ffn_kernel,
        grid=(b, n // tm),
        in_specs=[pl.BlockSpec((1, tm, d), lambda i, j: (i, j, 0)),
                  pl.BlockSpec((1, FFN_HALO, d), lambda i, j: (i, jnp.maximum(j * hb - 1, 0), 0)),
                  pl.BlockSpec((1, FFN_HALO, d), lambda i, j: (i, jnp.minimum((j + 1) * hb, nb - 1), 0)),
                  pl.BlockSpec((1, 1, d), lambda i, j: (i, 0, 0)),
                  pl.BlockSpec((1, 1, d), lambda i, j: (i, 0, 0)),
                  pl.BlockSpec((1, 1, d), lambda i, j: (i, 0, 0)),
                  pl.BlockSpec((1, d), lambda i, j: (0, 0)),
                  pl.BlockSpec((1, d), lambda i, j: (0, 0)),
                  pl.BlockSpec(wup.shape, lambda i, j: (0, 0), pipeline_mode=pl.Buffered(1)),
                  pl.BlockSpec(cw.shape, lambda i, j: (0, 0)),
                  pl.BlockSpec(cb.shape, lambda i, j: (0, 0)),
                  pl.BlockSpec(wdown.shape, lambda i, j: (0, 0), pipeline_mode=pl.Buffered(1))],
        out_specs=pl.BlockSpec((1, tm, d), lambda i, j: (i, j, 0)),
        out_shape=jax.ShapeDtypeStruct((b, n, d), F32),
        scratch_shapes=[pltpu.VMEM((tm, dff), BF16)],
        compiler_params=pltpu.CompilerParams(dimension_semantics=("arbitrary", "arbitrary"),
                                             vmem_limit_bytes=VMEM_LIMIT),
        name="ffn",
    )(x, x, x, shift, scale, gate, gpre, gpost, wup, cw, cb, wdown)


def kernel(x, c, ctx, c_ctx, pre_norm_mix, post_norm_mix, pre_norm_ffn, post_norm_ffn, w_ada, b_ada,
           w_in, conv_ssd_w, conv_ssd_b, dt_bias, a_log, d_skip, ssm_norm, w_pool, pool_scale, w_out,
           w_up, conv_ffn_w, conv_ffn_b, w_down):
    assert w_ada.shape[0] == 1, "single-layer block"
    b, n, d = x.shape
    n_ctx = ctx.shape[1]
    off_dt = D_SSM + D_XBC
    off_pool = off_dt + 2 * SSD_HEADS

    cc = jnp.zeros((16, d), F32).at[:b].set(c).at[b].set(c_ctx)
    mods = _ada(cc, w_ada[0], b_ada).reshape(16, 6, d)
    mx = [mods[:b, k][:, None, :] for k in range(6)]
    mc = [jnp.broadcast_to(mods[b, k][None, None, :], (b, 1, d)) for k in range(2)]

    wi = w_in[0]
    w_main = jnp.concatenate([wi[:, D_SSM:off_dt], wi[:, :D_SSM], wi[:, off_pool:]], axis=1).astype(BF16)
    w_dt = jnp.zeros((d, LANES), F32).at[:, :2 * SSD_HEADS].set(wi[:, off_dt:off_pool]).astype(BF16)
    w_ctx = wi[:, D_SSM:D_SSM + D_XB].astype(BF16)

    proj, dtraw = _inproj(x, mx[0], mx[1], pre_norm_mix, w_main, w_dt, 1024, 512, "inproj")
    cproj, cdtraw = _inproj(ctx, mc[0], mc[1], pre_norm_mix, w_ctx, w_dt, n_ctx, 256, "inproj_ctx")

    bias = jnp.broadcast_to(dt_bias[0].reshape(2 * SSD_HEADS, 1), (2 * SSD_HEADS, n))
    alog = jnp.broadcast_to(a_log[0].reshape(2 * SSD_HEADS, 1), (2 * SSD_HEADS, n))
    dsk = jnp.repeat(d_skip[0], SSD_HEAD_DIM)[None, :]
    y = _ssd(proj, dtraw, cproj, cdtraw, conv_ssd_w[0], conv_ssd_b, bias, alog, dsk, ssm_norm)

    x1 = _pool_out(y, proj, x, mx[2], post_norm_mix, w_pool[0].astype(BF16), pool_scale,
                   w_out[0].astype(BF16), 512)

    return _ffn(x1, mx[3], mx[4], mx[5], pre_norm_ffn, post_norm_ffn, w_up[0].astype(BF16),
                conv_ffn_w[0], conv_ffn_b, w_down[0].astype(BF16), 512)
```

```python
import functools

import numpy as np
import jax
import jax.numpy as jnp
from jax import lax
from jax.experimental import pallas as pl
from jax.experimental.pallas import tpu as pltpu

F32 = jnp.float32
BF16 = jnp.bfloat16

EPS = 1e-6
GRID_W = 64
SSD_HEADS = 8
SSD_HEAD_DIM = 64
SSD_GROUPS = 2
SSD_STATE = 128
SSD_PAIRS = SSD_HEADS // 2
SSD_CONV = 5
D_SSM = SSD_HEADS * SSD_HEAD_DIM
D_BC = 2 * SSD_GROUPS * SSD_STATE
POOL_WINDOWS = (2, 4, 8, 16)
D_POOL = 512
CHUNK = 128
LANES = 128
HALO = 8
N_ROWS = 8
VMEM_LIMIT = 56 * 1024 * 1024


def _silu(v):
    return v / (1.0 + jnp.exp(-v))


def _rms(v, gain):
    ms = jnp.mean(v * v, axis=-1, keepdims=True)
    return v * lax.rsqrt(ms + EPS) * gain


def _halo_specs(tm, n, d):
    hb, nb = tm // HALO, n // HALO
    return [pl.BlockSpec((1, HALO, d), lambda i, j: (i, jnp.maximum(j * hb - 1, 0), 0)),
            pl.BlockSpec((1, HALO, d), lambda i, j: (i, jnp.minimum((j + 1) * hb, nb - 1), 0))]


def _ada_kernel(c_ref, w_ref, b_ref, o_ref):
    s = _silu(c_ref[...]).astype(BF16)
    o_ref[...] = jnp.dot(s, w_ref[...].astype(BF16), preferred_element_type=F32) + b_ref[...]


def _ada(cc, w_ada, b_ada):
    rows, d = cc.shape
    n = w_ada.shape[1]
    tn = 1536
    return pl.pallas_call(
        _ada_kernel,
        grid=(n // tn,),
        in_specs=[pl.BlockSpec((rows, d), lambda j: (0, 0)),
                  pl.BlockSpec((d, tn), lambda j: (0, j)),
                  pl.BlockSpec((1, tn), lambda j: (0, j))],
        out_specs=pl.BlockSpec((rows, tn), lambda j: (0, j)),
        out_shape=jax.ShapeDtypeStruct((rows, n), F32),
        compiler_params=pltpu.CompilerParams(dimension_semantics=("arbitrary",),
                                             vmem_limit_bytes=VMEM_LIMIT),
        name="ada",
    )(cc, w_ada, b_ada)


def _dt_rows(dtraw_chunks, bias, a):
    h = SSD_HEADS
    raw = jnp.concatenate([blk.T[0:2 * h, :] for blk in dtraw_chunks], axis=1) + bias
    t = raw.shape[1]
    dt = jnp.maximum(raw, 0.0) + jnp.log1p(jnp.exp(-jnp.abs(raw)))
    dta = dt * a
    seg = lax.broadcasted_iota(jnp.int32, dta.shape, 1) & (CHUNK - 1)
    cs, rcs = dta, dta
    k = 1
    while k < CHUNK:
        cs = cs + jnp.where(seg >= k, pltpu.roll(cs, k, 1), 0.0)
        rcs = rcs + jnp.where(seg < CHUNK - k, pltpu.roll(rcs, t - k, 1), 0.0)
        k *= 2
    cd = jnp.exp(cs + rcs - dta)
    wf = jnp.exp(rcs[0:h] - dta[0:h]) * dt[0:h]
    wb = jnp.exp(cs[h:] - dta[h:]) * dt[h:]
    return [dt[0:h], dt[h:], cs[0:h], rcs[h:], wf, wb, cd[0:h], cd[h:]]


def _pair_rhs(xs, p):
    xp = xs[:, p * LANES:(p + 1) * LANES]
    lane = lax.broadcasted_iota(jnp.int32, xp.shape, 1)
    zero = jnp.zeros_like(xp)
    return jnp.concatenate([jnp.where(lane < SSD_HEAD_DIM, xp, zero),
                            jnp.where(lane >= SSD_HEAD_DIM, xp, zero)], axis=0)


def _pair_select(lane, v0, v1):
    return jnp.where(lane < SSD_HEAD_DIM, v0, v1)


def _local_states(xs, bts, wf, wb):
    out_f, out_b = [], []
    for p in range(SSD_PAIRS):
        bt = bts[p // 2]
        h0, h1 = 2 * p, 2 * p + 1
        lhs = jnp.concatenate(
            [jnp.concatenate([(bt * w[h0:h0 + 1, :]).astype(BF16), (bt * w[h1:h1 + 1, :]).astype(BF16)], axis=1)
             for w in (wf, wb)], axis=0)
        res = jnp.dot(lhs, _pair_rhs(xs, p), preferred_element_type=F32)
        out_f.append(res[0:SSD_STATE])
        out_b.append(res[SSD_STATE:2 * SSD_STATE])
    return out_f, out_b


def _mix_in_kernel(*refs, conv_cols, plain_cols):
    (x_ref, xp_ref, xn_ref, shift_ref, scale_ref, g_ref, w_ref, wdt_ref, cw_ref, cb_ref,
     bias_ref, alog_ref) = refs[:12]
    if plain_cols:
        xs_ref, bc_ref, zu_ref, rows_ref, cols_ref, ds_ref, up_scr, cv_scr = refs[12:]
    else:
        xs_ref, bc_ref, rows_ref, cols_ref, ds_ref, up_scr, cv_scr = refs[12:]
    tm = x_ref.shape[1]
    nch = tm // CHUNK
    j = pl.program_id(1)
    nj = pl.num_programs(1)

    def hmod(v):
        return _rms(v, g_ref[...]) * (1.0 + scale_ref[0]) + shift_ref[0]

    xm = hmod(x_ref[0])
    hp = jnp.where(j > 0, hmod(xp_ref[0]), 0.0)
    hn = jnp.where(j < nj - 1, hmod(xn_ref[0]), 0.0)
    hext = jnp.concatenate([hp, xm, hn], axis=0).astype(BF16)
    hm = xm.astype(BF16)

    half = SSD_CONV // 2
    cc = 2 * LANES
    for cb in range(conv_cols // cc):
        up = jnp.dot(hext, w_ref[:, cb * cc:(cb + 1) * cc], preferred_element_type=F32)
        for s in range(cc // LANES):
            up_scr[cb * (cc // LANES) + s] = up[:, s * LANES:(s + 1) * LANES]
    for slab in range(conv_cols // LANES):
        lanes = slice(slab * LANES, (slab + 1) * LANES)
        for par in range(2):
            lo = HALO - half + par
            acc = cb_ref[:, lanes] + cw_ref[0:1, lanes] * up_scr[slab, pl.ds(lo, tm // 2, stride=2), :]
            for k in range(1, SSD_CONV):
                acc = acc + cw_ref[k:k + 1, lanes] * up_scr[slab, pl.ds(lo + k, tm // 2, stride=2), :]
            cv_scr[slab, pl.ds(par, tm // 2, stride=2), :] = _silu(acc)
        if slab < D_SSM // LANES:
            xs_ref[0, :, lanes] = cv_scr[slab].astype(BF16)
        else:
            bc_ref[0, :, slab * LANES - D_SSM:(slab + 1) * LANES - D_SSM] = cv_scr[slab].astype(BF16)

    pc = 4 * LANES
    for cb in range(plain_cols // pc):
        zu_ref[0, :, cb * pc:(cb + 1) * pc] = jnp.dot(
            hm, w_ref[:, conv_cols + cb * pc:conv_cols + (cb + 1) * pc], preferred_element_type=F32).astype(BF16)

    dtraw = jnp.dot(hm, wdt_ref[...], preferred_element_type=F32)
    rows = _dt_rows([dtraw[c * CHUNK:(c + 1) * CHUNK] for c in range(nch)], bias_ref[...],
                    -jnp.exp(alog_ref[...]))
    pad = jnp.zeros((LANES - 2 * SSD_HEADS, CHUNK), F32)
    for c in range(nch):
        lanes = slice(c * CHUNK, (c + 1) * CHUNK)
        for q, arr in enumerate(rows):
            rows_ref[0, c, q] = arr[:, lanes]
        cols_ref[0, c] = jnp.concatenate([rows[2][:, lanes], rows[3][:, lanes], pad], axis=0).T

    def state_body(c, carry):
        r0 = pl.multiple_of(c * CHUNK, CHUNK)
        bts = [cv_scr[D_SSM // LANES + g, pl.ds(r0, CHUNK), :].T for g in range(SSD_GROUPS)]
        dsf, dsb = _local_states(xs_ref[0, pl.ds(r0, CHUNK), :], bts, rows_ref[0, c, 4], rows_ref[0, c, 5])
        for p in range(SSD_PAIRS):
            ds_ref[0, c, 0, p] = dsf[p].astype(BF16)
            ds_ref[0, c, 1, p] = dsb[p].astype(BF16)
        return carry

    lax.fori_loop(0, nch, state_body, 0, unroll=2)


def _mix_in(x, shift, scale, gain, w, wdt, conv_w, conv_b, bias, alog, tm, conv_cols, name):
    b, n, d = x.shape
    plain_cols = w.shape[1] - conv_cols
    nch = tm // CHUNK
    bc_cols = conv_cols - D_SSM
    full = lambda a: pl.BlockSpec(a.shape, lambda i, j: (0,) * a.ndim)
    out_specs = [pl.BlockSpec((1, tm, D_SSM), lambda i, j: (i, j, 0)),
                 pl.BlockSpec((1, tm, bc_cols), lambda i, j: (i, j, 0))]
    out_shape = [jax.ShapeDtypeStruct((b, n, D_SSM), BF16),
                 jax.ShapeDtypeStruct((b, n, bc_cols), BF16)]
    if plain_cols:
        out_specs.append(pl.BlockSpec((1, tm, plain_cols), lambda i, j: (i, j, 0)))
        out_shape.append(jax.ShapeDtypeStruct((b, n, plain_cols), BF16))
    out_specs += [pl.BlockSpec((1, nch, N_ROWS, SSD_HEADS, CHUNK), lambda i, j: (i, j, 0, 0, 0)),
                  pl.BlockSpec((1, nch, CHUNK, LANES), lambda i, j: (i, j, 0, 0)),
                  pl.BlockSpec((1, nch, 2, SSD_PAIRS, SSD_STATE, LANES), lambda i, j: (i, j, 0, 0, 0, 0))]
    out_shape += [jax.ShapeDtypeStruct((b, n // CHUNK, N_ROWS, SSD_HEADS, CHUNK), F32),
                  jax.ShapeDtypeStruct((b, n // CHUNK, CHUNK, LANES), F32),
                  jax.ShapeDtypeStruct((b, n // CHUNK, 2, SSD_PAIRS, SSD_STATE, LANES), BF16)]
    return pl.pallas_call(
        functools.partial(_mix_in_kernel, conv_cols=conv_cols, plain_cols=plain_cols),
        grid=(b, n // tm),
        in_specs=[pl.BlockSpec((1, tm, d), lambda i, j: (i, j, 0))] + _halo_specs(tm, n, d) + [
            pl.BlockSpec((1, 1, d), lambda i, j: (i, 0, 0)),
            pl.BlockSpec((1, 1, d), lambda i, j: (i, 0, 0)),
            full(gain), full(w), full(wdt), full(conv_w), full(conv_b), full(bias), full(alog)],
        out_specs=out_specs,
        out_shape=out_shape,
        scratch_shapes=[pltpu.VMEM((conv_cols // LANES, tm + 2 * HALO, LANES), F32),
                        pltpu.VMEM((conv_cols // LANES, tm, LANES), F32)],
        compiler_params=pltpu.CompilerParams(dimension_semantics=("arbitrary", "arbitrary"),
                                             vmem_limit_bytes=VMEM_LIMIT),
        name=name,
    )(x, x, x, shift, scale, gain, w, wdt, conv_w, conv_b, bias, alog)


def _mix_out_kernel(xs_ref, bc_ref, z_ref, u_ref, rows_ref, cols_ref, ds_ref, cds_ref, crows_ref,
                    x_ref, gate_ref, gn_ref, dsk_ref, gssm_ref, wpool_ref, pscale_ref, wout_ref,
                    cm_ref, inv_ref, o_ref,
                    sf_scr, sb_scr, st_scr, d_scr):
    n = u_ref.shape[1]
    tm = x_ref.shape[1]
    nchunks = n // CHUNK
    nctx = cds_ref.shape[1]
    tch = tm // CHUNK
    j = pl.program_id(1)
    lane_row = lax.broadcasted_iota(jnp.int32, (1, LANES), 1)

    def cd_row(ref, c, q, p):
        cd = ref[0, c, q]
        return _pair_select(lane_row, cd[2 * p:2 * p + 1, :], cd[2 * p + 1:2 * p + 2, :])

    @pl.when(j == 0)
    def _():
        for p in range(SSD_PAIRS):
            sf = cds_ref[0, 0, 0, p].astype(F32)
            for c in range(1, nctx):
                sf = sf * cd_row(crows_ref, c, 6, p) + cds_ref[0, c, 0, p].astype(F32)
            sb = cds_ref[0, nctx - 1, 1, p].astype(F32)
            for c in reversed(range(nctx - 1)):
                sb = sb * cd_row(crows_ref, c, 7, p) + cds_ref[0, c, 1, p].astype(F32)
            st_scr[0, p] = sf
            st_scr[1, p] = sb

        def scan_body(i, carry):
            cf = i
            cbk = nchunks - 1 - i
            for p in range(SSD_PAIRS):
                s_old = st_scr[0, p]
                sf_scr[cf, p] = s_old
                st_scr[0, p] = s_old * cd_row(rows_ref, cf, 6, p) + ds_ref[0, cf, 0, p].astype(F32)
                s_old = st_scr[1, p]
                sb_scr[cbk, p] = s_old
                st_scr[1, p] = s_old * cd_row(rows_ref, cbk, 7, p) + ds_ref[0, cbk, 1, p].astype(F32)
            return carry

        lax.fori_loop(0, nchunks, scan_body, 0)

        for gi, w in enumerate(POOL_WINDOWS):
            cols = slice(gi * LANES, (gi + 1) * LANES)
            cmat = cm_ref[gi]
            parts = [jnp.dot(cmat, u_ref[0, i * LANES:(i + 1) * LANES, cols], preferred_element_type=F32)
                     for i in range(n // LANES)]
            cs = jnp.concatenate(parts, axis=0)
            acc = None
            for jj in range(-(w // 2), w - w // 2):
                sh = jj * GRID_W
                if sh == 0:
                    t = cs
                elif sh > 0:
                    t = jnp.concatenate([cs[sh:], jnp.zeros((sh, LANES), F32)], axis=0)
                else:
                    t = jnp.concatenate([jnp.zeros((-sh, LANES), F32), cs[:n + sh]], axis=0)
                acc = t if acc is None else acc + t
            m = acc * inv_ref[gi]
            d_scr[:, cols] = (m - u_ref[0, :, cols].astype(F32)).astype(BF16)

    row_i = lax.broadcasted_iota(jnp.int32, (CHUNK, CHUNK), 0)
    col_i = lax.broadcasted_iota(jnp.int32, (CHUNK, CHUNK), 1)
    lane_full = lax.broadcasted_iota(jnp.int32, (CHUNK, LANES), 1)

    for lc in range(tch):
        c = j * tch + lc
        r0 = lc * CHUNK
        xs = xs_ref[0, pl.ds(r0, CHUNK), :]
        dtf, dtb, pf, ab = rows_ref[0, c, 0], rows_ref[0, c, 1], rows_ref[0, c, 2], rows_ref[0, c, 3]
        cols = cols_ref[0, c]

        def colb(k):
            return jnp.sum(jnp.where(lane_full == k, cols, 0.0), axis=1, keepdims=True)

        y_pairs = []
        for g in range(SSD_GROUPS):
            bg = bc_ref[0, pl.ds(r0, CHUNK), g * SSD_STATE:(g + 1) * SSD_STATE]
            cg = bc_ref[0, pl.ds(r0, CHUNK), (SSD_GROUPS + g) * SSD_STATE:(SSD_GROUPS + g + 1) * SSD_STATE]
            gmat = lax.dot_general(cg, bg, (((1,), (1,)), ((), ())), preferred_element_type=F32)
            scat = jnp.concatenate([sf_scr[c, 2 * g].astype(BF16), sf_scr[c, 2 * g + 1].astype(BF16),
                                    sb_scr[c, 2 * g].astype(BF16), sb_scr[c, 2 * g + 1].astype(BF16)], axis=1)
            yoff = jnp.dot(cg, scat, preferred_element_type=F32)
            for q in range(2):
                p = 2 * g + q
                ms, efs, ebs = [], [], []
                for h in (2 * p, 2 * p + 1):
                    pf_c = colb(h)
                    ab_c = colb(SSD_HEADS + h)
                    dtf_r, dtb_r = dtf[h:h + 1, :], dtb[h:h + 1, :]
                    dm = jnp.where(row_i >= col_i, pf_c - pf[h:h + 1, :], ab_c - ab[h:h + 1, :])
                    dtw = jnp.where(row_i > col_i, dtf_r, jnp.where(row_i < col_i, dtb_r, dtf_r + dtb_r))
                    ms.append((gmat * jnp.exp(dm) * dtw).astype(BF16))
                    efs.append(jnp.exp(pf_c))
                    ebs.append(jnp.exp(ab_c))
                ydiag = jnp.dot(jnp.concatenate(ms, axis=1), _pair_rhs(xs, p), preferred_element_type=F32)
                y_pairs.append(ydiag
                               + _pair_select(lane_full, efs[0], efs[1]) * yoff[:, q * LANES:(q + 1) * LANES]
                               + _pair_select(lane_full, ebs[0], ebs[1]) * yoff[:, (2 + q) * LANES:(3 + q) * LANES])
        y = jnp.concatenate(y_pairs, axis=1) + dsk_ref[...] * xs.astype(F32)
        yz = y * _silu(z_ref[0, pl.ds(r0, CHUNK), :].astype(F32))
        yn = _rms(yz, gssm_ref[...]).astype(BF16)

        drows = pl.ds(pl.multiple_of(j * tm + r0, CHUNK), CHUNK)
        ps = [jnp.dot(d_scr[drows, gi * LANES:(gi + 1) * LANES], wpool_ref[gi], preferred_element_type=F32)
              for gi in range(len(POOL_WINDOWS))]
        pm = jnp.concatenate(ps, axis=1) * pscale_ref[...]
        mix = jnp.concatenate([yn, pm.astype(BF16)], axis=1)
        yx = jnp.dot(mix, wout_ref[...], preferred_element_type=F32)
        o_ref[0, pl.ds(r0, CHUNK), :] = x_ref[0, pl.ds(r0, CHUNK), :] + gate_ref[0] * _rms(yx, gn_ref[...])


def _pool_constants(n):
    rows = n // GRID_W
    t = np.arange(LANES)
    blk, col = t // GRID_W, t % GRID_W
    cms, invs = [], []
    r = np.arange(n) // GRID_W
    c = np.arange(n) % GRID_W
    for w in POOL_WINDOWS:
        d = col[None, :] - col[:, None]
        cms.append(((blk[:, None] == blk[None, :]) & (d >= -(w // 2)) & (d < w - w // 2)).astype(np.float32))
        cnt_r = np.minimum(r + w - w // 2, rows) - np.maximum(r - w // 2, 0)
        cnt_c = np.minimum(c + w - w // 2, GRID_W) - np.maximum(c - w // 2, 0)
        invs.append(np.broadcast_to((1.0 / (cnt_r * cnt_c))[:, None], (n, LANES)).astype(np.float32))
    return np.stack(cms), np.stack(invs)


def _mix_out(xs, bc, zu, rows, cols, ds, cds, crows, x, gate, gn, dsk, gssm, wpool, pscale, wout, tm):
    b, n, d = x.shape
    nchunks = n // CHUNK
    cm_np, inv_np = _pool_constants(n)
    cm = jnp.asarray(cm_np, BF16)
    inv = jnp.asarray(inv_np, F32)
    full = lambda a: pl.BlockSpec(a.shape, lambda i, j: (0,) * a.ndim)
    per_batch = lambda a: pl.BlockSpec((1,) + a.shape[1:], lambda i, j: (i,) + (0,) * (a.ndim - 1))
    return pl.pallas_call(
        _mix_out_kernel,
        grid=(b, n // tm),
        in_specs=[pl.BlockSpec((1, tm, D_SSM), lambda i, j: (i, j, 0)),
                  pl.BlockSpec((1, tm, D_BC), lambda i, j: (i, j, 0)),
                  pl.BlockSpec((1, tm, D_SSM), lambda i, j: (i, j, 0)),
                  pl.BlockSpec((1, n, D_POOL), lambda i, j: (i, 0, 1)),
                  per_batch(rows), per_batch(cols), per_batch(ds), per_batch(cds), per_batch(crows),
                  pl.BlockSpec((1, tm, d), lambda i, j: (i, j, 0)),
                  pl.BlockSpec((1, 1, d), lambda i, j: (i, 0, 0)),
                  full(gn), full(dsk), full(gssm), full(wpool), full(pscale), full(wout), full(cm),
                  pl.BlockSpec(inv.shape, lambda i, j: (0, 0, 0), pipeline_mode=pl.Buffered(1))],
        out_specs=pl.BlockSpec((1, tm, d), lambda i, j: (i, j, 0)),
        out_shape=jax.ShapeDtypeStruct((b, n, d), F32),
        scratch_shapes=[pltpu.VMEM((nchunks, SSD_PAIRS, SSD_STATE, LANES), F32),
                        pltpu.VMEM((nchunks, SSD_PAIRS, SSD_STATE, LANES), F32),
                        pltpu.VMEM((2, SSD_PAIRS, SSD_STATE, LANES), F32),
                        pltpu.VMEM((n, D_POOL), BF16)],
        compiler_params=pltpu.CompilerParams(dimension_semantics=("arbitrary", "arbitrary"),
                                             vmem_limit_bytes=VMEM_LIMIT),
        name="mix_out",
    )(xs, bc, zu, zu, rows, cols, ds, cds, crows, x, gate, gn, dsk, gssm, wpool, pscale, wout, cm, inv)


FFN_COLS = 2 * LANES
FFN_RING = 2


def _ffn_kernel(x_ref, xp_ref, xn_ref, shift_ref, scale_ref, gate_ref, gpre_ref, gpost_ref,
                wup_ref, cw_ref, cb_ref, wdown_ref, o_ref, act_scr, up_scr, af_scr):
    tm = x_ref.shape[1]
    dff = wdown_ref.shape[0]
    j = pl.program_id(1)
    nj = pl.num_programs(1)

    def hmod(v):
        return _rms(v, gpre_ref[...]) * (1.0 + scale_ref[0]) + shift_ref[0]

    x = x_ref[0]
    hp = jnp.where(j > 0, hmod(xp_ref[0]), 0.0)
    hn = jnp.where(j < nj - 1, hmod(xn_ref[0]), 0.0)
    hext = jnp.concatenate([hp, hmod(x), hn], axis=0).astype(BF16)

    nsl = FFN_COLS // LANES
    for cidx in range(dff // FFN_COLS):
        slot = cidx % FFN_RING
        conv = []
        for half, base in enumerate((cidx * FFN_COLS, dff + cidx * FFN_COLS)):
            up = jnp.dot(hext, wup_ref[:, base:base + FFN_COLS], preferred_element_type=F32)
            for s in range(nsl):
                lanes = slice(base + s * LANES, base + (s + 1) * LANES)
                slab = (2 * slot + half) * nsl + s
                up_scr[slab] = up[:, s * LANES:(s + 1) * LANES]
                conv.append([cb_ref[:, lanes] + sum(
                    cw_ref[k:k + 1, lanes] * up_scr[slab, pl.ds(HALO - 1 + par + k, tm // 2, stride=2), :]
                    for k in range(3)) for par in range(2)])
        for s in range(nsl):
            for par in range(2):
                af_scr[slot * nsl + s, pl.ds(par, tm // 2, stride=2), :] = (
                    _silu(conv[nsl + s][par]) * conv[s][par])
            lo = cidx * FFN_COLS + s * LANES
            act_scr[:, lo:lo + LANES] = af_scr[slot * nsl + s].astype(BF16)

    f = jnp.dot(act_scr[...], wdown_ref[...], preferred_element_type=F32)
    o_ref[0] = x + gate_ref[0] * _rms(f, gpost_ref[...])


def _ffn(x, shift, scale, gate, gpre, gpost, wup, cw, cb, wdown, tm):
    b, n, d = x.shape
    dff = wdown.shape[0]
    return pl.pallas_call(
        _ffn_kernel,
        grid=(b, n // tm),
        in_specs=[pl.BlockSpec((1, tm, d), lambda i, j: (i, j, 0))] + _halo_specs(tm, n, d) + [
            pl.BlockSpec((1, 1, d), lambda i, j: (i, 0, 0)),
            pl.BlockSpec((1, 1, d), lambda i, j: (i, 0, 0)),
            pl.BlockSpec((1, 1, d), lambda i, j: (i, 0, 0)),
            pl.BlockSpec((1, d), lambda i, j: (0, 0)),
            pl.BlockSpec((1, d), lambda i, j: (0, 0)),
            pl.BlockSpec(wup.shape, lambda i, j: (0, 0), pipeline_mode=pl.Buffered(1)),
            pl.BlockSpec(cw.shape, lambda i, j: (0, 0)),
            pl.BlockSpec(cb.shape, lambda i, j: (0, 0)),
            pl.BlockSpec(wdown.shape, lambda i, j: (0, 0), pipeline_mode=pl.Buffered(1))],
        out_specs=pl.BlockSpec((1, tm, d), lambda i, j: (i, j, 0)),
        out_shape=jax.ShapeDtypeStruct((b, n, d), F32),
        scratch_shapes=[pltpu.VMEM((tm, dff), BF16),
                        pltpu.VMEM((2 * FFN_RING * FFN_COLS // LANES, tm + 2 * HALO, LANES), F32),
                        pltpu.VMEM((FFN_RING * FFN_COLS // LANES, tm, LANES), F32)],
        compiler_params=pltpu.CompilerParams(dimension_semantics=("arbitrary", "arbitrary"),
                                             vmem_limit_bytes=VMEM_LIMIT),
        name="ffn",
    )(x, x, x, shift, scale, gate, gpre, gpost, wup, cw, cb, wdown)


def kernel(x, c, ctx, c_ctx, pre_norm_mix, post_norm_mix, pre_norm_ffn, post_norm_ffn, w_ada, b_ada,
           w_in, conv_ssd_w, conv_ssd_b, dt_bias, a_log, d_skip, ssm_norm, w_pool, pool_scale, w_out,
           w_up, conv_ffn_w, conv_ffn_b, w_down):
    assert w_ada.shape[0] == 1, "single-layer block"
    b, n, d = x.shape
    n_ctx = ctx.shape[1]
    d_xbc = D_SSM + D_BC
    d_xb = D_SSM + D_BC // 2
    off_dt = D_SSM + d_xbc
    off_pool = off_dt + 2 * SSD_HEADS
    tm_in = 1024

    cc = jnp.zeros((16, d), F32).at[:b].set(c).at[b].set(c_ctx)
    mods = _ada(cc, w_ada[0], b_ada).reshape(16, 6, d)
    mx = [mods[:b, k][:, None, :] for k in range(6)]
    mc = [jnp.broadcast_to(mods[b, k][None, None, :], (b, 1, d)) for k in range(2)]

    wi = w_in[0]
    w_main = jnp.concatenate([wi[:, D_SSM:off_dt], wi[:, :D_SSM], wi[:, off_pool:]], axis=1).astype(BF16)
    w_dt = jnp.zeros((d, LANES), F32).at[:, :2 * SSD_HEADS].set(wi[:, off_dt:off_pool]).astype(BF16)
    w_ctx = wi[:, D_SSM:D_SSM + d_xb].astype(BF16)
    bias = jnp.broadcast_to(dt_bias[0].reshape(2 * SSD_HEADS, 1), (2 * SSD_HEADS, tm_in))
    alog = jnp.broadcast_to(a_log[0].reshape(2 * SSD_HEADS, 1), (2 * SSD_HEADS, tm_in))

    xs, bc, zu, rows, cols, ds = _mix_in(x, mx[0], mx[1], pre_norm_mix, w_main, w_dt, conv_ssd_w[0],
                                         conv_ssd_b, bias, alog, tm_in, d_xbc, "mix_in")
    _, _, crows, _, cds = _mix_in(ctx, mc[0], mc[1], pre_norm_mix, w_ctx, w_dt, conv_ssd_w[0, :, :d_xb],
                                  conv_ssd_b[:, :d_xb], bias[:, :n_ctx], alog[:, :n_ctx], n_ctx, d_xb,
                                  "mix_in_ctx")

    dsk = jnp.repeat(d_skip[0], SSD_HEAD_DIM)[None, :]
    x1 = _mix_out(xs, bc, zu, rows, cols, ds, cds, crows, x, mx[2], post_norm_mix, dsk, ssm_norm,
                  w_pool[0].astype(BF16), pool_scale, w_out[0].astype(BF16), 512)

    return _ffn(x1, mx[3], mx[4], mx[5], pre_norm_ffn, post_norm_ffn, w_up[0].astype(BF16),
                conv_ffn_w[0], conv_ffn_b, w_down[0].astype(BF16), 512)
```

```python
import functools

import numpy as np
import jax
import jax.numpy as jnp
from jax import lax
from jax.experimental import pallas as pl
from jax.experimental.pallas import tpu as pltpu

F32 = jnp.float32
BF16 = jnp.bfloat16

EPS = 1e-6
GRID_W = 64
SSD_HEADS = 8
SSD_HEAD_DIM = 64
SSD_GROUPS = 2
SSD_STATE = 128
SSD_PAIRS = SSD_HEADS // 2
SSD_CONV = 5
D_SSM = SSD_HEADS * SSD_HEAD_DIM
D_BC = 2 * SSD_GROUPS * SSD_STATE
POOL_WINDOWS = (2, 4, 8, 16)
D_POOL = 512
CHUNK = 128
LANES = 128
HALO = 8
N_ROWS = 9
VMEM_LIMIT = 56 * 1024 * 1024


def _silu(v):
    return v / (1.0 + jnp.exp(-v))


def _rms(v, gain):
    ms = jnp.mean(v * v, axis=-1, keepdims=True)
    return v * lax.rsqrt(ms + EPS) * gain


def _halo_specs(tm, n, d):
    hb, nb = tm // HALO, n // HALO
    return [pl.BlockSpec((1, HALO, d), lambda i, j: (i, jnp.maximum(j * hb - 1, 0), 0)),
            pl.BlockSpec((1, HALO, d), lambda i, j: (i, jnp.minimum((j + 1) * hb, nb - 1), 0))]


def _ada_kernel(c_ref, w_ref, b_ref, o_ref):
    s = _silu(c_ref[...]).astype(BF16)
    o_ref[...] = jnp.dot(s, w_ref[...].astype(BF16), preferred_element_type=F32) + b_ref[...]


def _ada(cc, w_ada, b_ada):
    rows, d = cc.shape
    n = w_ada.shape[1]
    tn = 1536
    return pl.pallas_call(
        _ada_kernel,
        grid=(n // tn,),
        in_specs=[pl.BlockSpec((rows, d), lambda j: (0, 0)),
                  pl.BlockSpec((d, tn), lambda j: (0, j)),
                  pl.BlockSpec((1, tn), lambda j: (0, j))],
        out_specs=pl.BlockSpec((rows, tn), lambda j: (0, j)),
        out_shape=jax.ShapeDtypeStruct((rows, n), F32),
        compiler_params=pltpu.CompilerParams(dimension_semantics=("arbitrary",),
                                             vmem_limit_bytes=VMEM_LIMIT),
        name="ada",
    )(cc, w_ada, b_ada)


def _dt_rows(dtraw_chunks, bias, a):
    h = SSD_HEADS
    raw = jnp.concatenate([blk.T[0:2 * h, :] for blk in dtraw_chunks], axis=1) + bias
    t = raw.shape[1]
    dt = jnp.maximum(raw, 0.0) + jnp.log1p(jnp.exp(-jnp.abs(raw)))
    dta = dt * a
    seg = lax.broadcasted_iota(jnp.int32, dta.shape, 1) & (CHUNK - 1)
    cs, rcs = dta, dta
    k = 1
    while k < CHUNK:
        cs = cs + jnp.where(seg >= k, pltpu.roll(cs, k, 1), 0.0)
        rcs = rcs + jnp.where(seg < CHUNK - k, pltpu.roll(rcs, t - k, 1), 0.0)
        k *= 2
    cd = jnp.exp(cs + rcs - dta)
    wf = jnp.exp(rcs[0:h] - dta[0:h]) * dt[0:h]
    wb = jnp.exp(cs[h:] - dta[h:]) * dt[h:]
    log2e = 1.0 / np.log(2.0)
    pf2, ab2 = cs[0:h] * log2e, rcs[h:] * log2e
    ldt2 = jnp.log(dt) * log2e
    ldg2 = jnp.log(dt[0:h] + dt[h:]) * log2e
    return [pf2, ab2, pf2 - ldt2[0:h], ab2 - ldt2[h:], ldg2, wf, wb, cd[0:h], cd[h:]]


def _pair_rhs(xs, p):
    xp = xs[:, p * LANES:(p + 1) * LANES]
    lane = lax.broadcasted_iota(jnp.int32, xp.shape, 1)
    zero = jnp.zeros_like(xp)
    return jnp.concatenate([jnp.where(lane < SSD_HEAD_DIM, xp, zero),
                            jnp.where(lane >= SSD_HEAD_DIM, xp, zero)], axis=0)


def _pair_select(lane, v0, v1):
    return jnp.where(lane < SSD_HEAD_DIM, v0, v1)


def _local_states(xs, bts, wf, wb):
    out_f, out_b = [], []
    for p in range(SSD_PAIRS):
        bt = bts[p // 2]
        h0, h1 = 2 * p, 2 * p + 1
        lhs = jnp.concatenate(
            [jnp.concatenate([(bt * w[h0:h0 + 1, :]).astype(BF16), (bt * w[h1:h1 + 1, :]).astype(BF16)], axis=1)
             for w in (wf, wb)], axis=0)
        res = jnp.dot(lhs, _pair_rhs(xs, p), preferred_element_type=F32)
        out_f.append(res[0:SSD_STATE])
        out_b.append(res[SSD_STATE:2 * SSD_STATE])
    return out_f, out_b


def _mix_in_kernel(*refs, conv_cols, latent):
    (x_ref, xp_ref, xn_ref, shift_ref, scale_ref, g_ref, w_ref, wpd_ref, cw_ref, cb_ref,
     bias_ref, alog_ref) = refs[:12]
    if latent:
        xs_ref, bc_ref, zu_ref, rows_ref, cols_ref, ds_ref, up_scr, cv_scr = refs[12:]
    else:
        rows_ref, ds_ref, up_scr, cv_scr = refs[12:]
    nseg, seg, d = x_ref.shape
    tm = nseg * seg
    nch = tm // CHUNK
    cps = seg // CHUNK
    j = pl.program_id(1)
    nj = pl.num_programs(1)

    def hmod(v):
        return _rms(v, g_ref[...]) * (1.0 + scale_ref[0]) + shift_ref[0]

    xm = hmod(x_ref[...].reshape(tm, d))
    gap = jnp.zeros((HALO, d), F32)
    if latent:
        hp = jnp.where(j > 0, hmod(xp_ref[0]), 0.0)
        hn = jnp.where(j < nj - 1, hmod(xn_ref[0]), 0.0)
    else:
        hp = hn = gap
    pieces = [hp]
    for i in range(nseg):
        pieces += [xm[i * seg:(i + 1) * seg], gap if i < nseg - 1 else hn]
    ext_rows = up_scr.shape[1]
    if ext_rows > tm + (nseg + 1) * HALO:
        pieces.append(jnp.zeros((ext_rows - tm - (nseg + 1) * HALO, d), F32))
    hext = jnp.concatenate(pieces, axis=0).astype(BF16)
    hm = xm.astype(BF16)

    half = SSD_CONV // 2
    cc = 2 * LANES
    for cb in range(conv_cols // cc):
        up = jnp.dot(hext, w_ref[:, D_SSM + cb * cc:D_SSM + (cb + 1) * cc], preferred_element_type=F32)
        for s in range(cc // LANES):
            up_scr[cb * (cc // LANES) + s] = up[:, s * LANES:(s + 1) * LANES]
    for slab in range(conv_cols // LANES):
        lanes = slice(slab * LANES, (slab + 1) * LANES)
        for i in range(nseg):
            for par in range(2):
                lo = HALO + i * (seg + HALO) - half + par
                acc = cb_ref[:, lanes] + cw_ref[0:1, lanes] * up_scr[slab, pl.ds(lo, seg // 2, stride=2), :]
                for k in range(1, SSD_CONV):
                    acc = acc + cw_ref[k:k + 1, lanes] * up_scr[slab, pl.ds(lo + k, seg // 2, stride=2), :]
                cv_scr[slab, pl.ds(i * seg + par, seg // 2, stride=2), :] = _silu(acc)
        if latent:
            if slab < D_SSM // LANES:
                xs_ref[0, :, lanes] = cv_scr[slab].astype(BF16)
            else:
                bc_ref[0, :, slab * LANES - D_SSM:(slab + 1) * LANES - D_SSM] = cv_scr[slab].astype(BF16)

    if latent:
        zu_ref[0, :, 0:D_SSM] = jnp.dot(hm, w_ref[:, 0:D_SSM], preferred_element_type=F32).astype(BF16)
        pd = jnp.dot(hm, wpd_ref[...], preferred_element_type=F32)
        zu_ref[0, :, D_SSM:D_SSM + D_POOL] = pd[:, 0:D_POOL].astype(BF16)
        dtraw = pd[:, D_POOL:D_POOL + LANES]
    else:
        dtraw = jnp.dot(hm, wpd_ref[:, D_POOL:D_POOL + LANES], preferred_element_type=F32)

    rows = _dt_rows([dtraw[c * CHUNK:(c + 1) * CHUNK] for c in range(nch)], bias_ref[...],
                    -jnp.exp(alog_ref[...]))
    pad = jnp.zeros((LANES - 2 * SSD_HEADS, CHUNK), F32)
    for c in range(nch):
        lanes = slice(c * CHUNK, (c + 1) * CHUNK)
        for q, arr in enumerate(rows):
            rows_ref[c // cps, c % cps, q] = arr[:, lanes]
        if latent:
            cols_ref[0, c] = jnp.concatenate([rows[0][:, lanes], rows[1][:, lanes], pad], axis=0).T

    def state_body(c, carry):
        r0 = pl.multiple_of(c * CHUNK, CHUNK)
        ci, cj = c // cps, c % cps
        xs = jnp.concatenate([cv_scr[s, pl.ds(r0, CHUNK), :] for s in range(D_SSM // LANES)],
                             axis=1).astype(BF16)
        bts = [cv_scr[D_SSM // LANES + g, pl.ds(r0, CHUNK), :].T for g in range(SSD_GROUPS)]
        dsf, dsb = _local_states(xs, bts, rows_ref[ci, cj, 5], rows_ref[ci, cj, 6])
        for p in range(SSD_PAIRS):
            ds_ref[ci, cj, 0, p] = dsf[p].astype(BF16)
            ds_ref[ci, cj, 1, p] = dsb[p].astype(BF16)
        return carry

    lax.fori_loop(0, nch, state_body, 0, unroll=2)


def _mix_in(x, shift, scale, gain, w, wpd, conv_w, conv_b, bias, alog, tm, latent, name):
    b, n, d = x.shape
    conv_cols = conv_w.shape[1]
    nch = tm // CHUNK
    nseg = 1 if latent else tm // n
    seg = tm // nseg
    cps = seg // CHUNK
    ext_rows = -(-(tm + (nseg + 1) * HALO) // 16) * 16
    full = lambda a: pl.BlockSpec(a.shape, lambda i, j: (0,) * a.ndim)
    stat = lambda tail, dt: jax.ShapeDtypeStruct((b, n // CHUNK) + tail, dt)
    out_specs, out_shape = [], []
    if latent:
        grid = (b, n // tm)
        x_spec = pl.BlockSpec((1, tm, d), lambda i, j: (i, j, 0))
        mod_spec = pl.BlockSpec((1, 1, d), lambda i, j: (i, 0, 0))
        out_specs += [pl.BlockSpec((1, tm, D_SSM), lambda i, j: (i, j, 0)),
                      pl.BlockSpec((1, tm, conv_cols - D_SSM), lambda i, j: (i, j, 0)),
                      pl.BlockSpec((1, tm, D_SSM + D_POOL), lambda i, j: (i, j, 0))]
        out_shape += [jax.ShapeDtypeStruct((b, n, D_SSM), BF16),
                      jax.ShapeDtypeStruct((b, n, conv_cols - D_SSM), BF16),
                      jax.ShapeDtypeStruct((b, n, D_SSM + D_POOL), BF16)]
        blk = lambda tail: pl.BlockSpec((1, nch) + tail, lambda i, j: (i, j) + (0,) * len(tail))
    else:
        grid = (b // nseg, 1)
        x_spec = pl.BlockSpec((nseg, n, d), lambda i, j: (i, 0, 0))
        mod_spec = pl.BlockSpec((1, 1, d), lambda i, j: (0, 0, 0))
        blk = lambda tail: pl.BlockSpec((nseg, cps) + tail, lambda i, j: (i, 0) + (0,) * len(tail))
    out_specs.append(blk((N_ROWS, SSD_HEADS, CHUNK)))
    out_shape.append(stat((N_ROWS, SSD_HEADS, CHUNK), F32))
    if latent:
        out_specs.append(blk((CHUNK, LANES)))
        out_shape.append(stat((CHUNK, LANES), F32))
    out_specs.append(blk((2, SSD_PAIRS, SSD_STATE, LANES)))
    out_shape.append(stat((2, SSD_PAIRS, SSD_STATE, LANES), BF16))
    return pl.pallas_call(
        functools.partial(_mix_in_kernel, conv_cols=conv_cols, latent=latent),
        grid=grid,
        in_specs=[x_spec] + _halo_specs(seg, n, d) + [
            mod_spec, mod_spec,
            full(gain), full(w), full(wpd), full(conv_w), full(conv_b), full(bias), full(alog)],
        out_specs=out_specs,
        out_shape=out_shape,
        scratch_shapes=[pltpu.VMEM((conv_cols // LANES, ext_rows, LANES), F32),
                        pltpu.VMEM((conv_cols // LANES, tm, LANES), F32)],
        compiler_params=pltpu.CompilerParams(dimension_semantics=("arbitrary", "arbitrary"),
                                             vmem_limit_bytes=VMEM_LIMIT),
        name=name,
    )(x, x, x, shift, scale, gain, w, wpd, conv_w, conv_b, bias, alog)


def _mix_out_kernel(xs_ref, bc_ref, z_ref, u_ref, rows_ref, cols_ref, ds_ref, cds_ref, crows_ref,
                    x_ref, gate_ref, gn_ref, dsk_ref, gssm_ref, wpool_ref, pscale_ref, wout_ref,
                    cm_ref, inv_ref, o_ref,
                    sf_scr, sb_scr, st_scr, d_scr, mix_scr):
    n = u_ref.shape[1]
    tm = x_ref.shape[1]
    nchunks = n // CHUNK
    nctx = cds_ref.shape[1]
    tch = tm // CHUNK
    j = pl.program_id(1)
    lane_row = lax.broadcasted_iota(jnp.int32, (1, LANES), 1)

    def cd_row(ref, c, q, p):
        cd = ref[0, c, q]
        return _pair_select(lane_row, cd[2 * p:2 * p + 1, :], cd[2 * p + 1:2 * p + 2, :])

    @pl.when(j == 0)
    def _():
        for p in range(SSD_PAIRS):
            sf = cds_ref[0, 0, 0, p].astype(F32)
            for c in range(1, nctx):
                sf = sf * cd_row(crows_ref, c, 7, p) + cds_ref[0, c, 0, p].astype(F32)
            sb = cds_ref[0, nctx - 1, 1, p].astype(F32)
            for c in reversed(range(nctx - 1)):
                sb = sb * cd_row(crows_ref, c, 8, p) + cds_ref[0, c, 1, p].astype(F32)
            st_scr[0, p] = sf
            st_scr[1, p] = sb

        def scan_body(i, carry):
            cf = i
            cbk = nchunks - 1 - i
            for p in range(SSD_PAIRS):
                s_old = st_scr[0, p]
                sf_scr[cf, p] = s_old
                st_scr[0, p] = s_old * cd_row(rows_ref, cf, 7, p) + ds_ref[0, cf, 0, p].astype(F32)
                s_old = st_scr[1, p]
                sb_scr[cbk, p] = s_old
                st_scr[1, p] = s_old * cd_row(rows_ref, cbk, 8, p) + ds_ref[0, cbk, 1, p].astype(F32)
            return carry

        lax.fori_loop(0, nchunks, scan_body, 0)

        for gi, w in enumerate(POOL_WINDOWS):
            cols = slice(gi * LANES, (gi + 1) * LANES)
            cmat = cm_ref[gi]
            parts = [jnp.dot(cmat, u_ref[0, i * LANES:(i + 1) * LANES, cols], preferred_element_type=F32)
                     for i in range(n // LANES)]
            cs = jnp.concatenate(parts, axis=0)
            acc = None
            for jj in range(-(w // 2), w - w // 2):
                sh = jj * GRID_W
                if sh == 0:
                    t = cs
                elif sh > 0:
                    t = jnp.concatenate([cs[sh:], jnp.zeros((sh, LANES), F32)], axis=0)
                else:
                    t = jnp.concatenate([jnp.zeros((-sh, LANES), F32), cs[:n + sh]], axis=0)
                acc = t if acc is None else acc + t
            m = acc * inv_ref[gi]
            d_scr[:, cols] = (m - u_ref[0, :, cols].astype(F32)).astype(BF16)

    row_i = lax.broadcasted_iota(jnp.int32, (CHUNK, CHUNK), 0)
    col_i = lax.broadcasted_iota(jnp.int32, (CHUNK, CHUNK), 1)
    lane_full = lax.broadcasted_iota(jnp.int32, (CHUNK, LANES), 1)

    for lc in range(tch):
        c = j * tch + lc
        r0 = lc * CHUNK
        xs = xs_ref[0, pl.ds(r0, CHUNK), :]
        pfp, abp, ldg = rows_ref[0, c, 2], rows_ref[0, c, 3], rows_ref[0, c, 4]
        cols = cols_ref[0, c]

        def colb(k):
            return jnp.sum(jnp.where(lane_full == k, cols, 0.0), axis=1, keepdims=True)

        y_pairs = []
        for g in range(SSD_GROUPS):
            bg = bc_ref[0, pl.ds(r0, CHUNK), g * SSD_STATE:(g + 1) * SSD_STATE]
            cg = bc_ref[0, pl.ds(r0, CHUNK), (SSD_GROUPS + g) * SSD_STATE:(SSD_GROUPS + g + 1) * SSD_STATE]
            gmat = lax.dot_general(cg, bg, (((1,), (1,)), ((), ())), preferred_element_type=F32)
            scat = jnp.concatenate([sf_scr[c, 2 * g].astype(BF16), sf_scr[c, 2 * g + 1].astype(BF16),
                                    sb_scr[c, 2 * g].astype(BF16), sb_scr[c, 2 * g + 1].astype(BF16)], axis=1)
            yoff = jnp.dot(cg, scat, preferred_element_type=F32)
            for q in range(2):
                p = 2 * g + q
                ms, efs, ebs = [], [], []
                for h in (2 * p, 2 * p + 1):
                    pf_c = colb(h)
                    ab_c = colb(SSD_HEADS + h)
                    dm = jnp.where(row_i > col_i, pf_c - pfp[h:h + 1, :],
                                   jnp.where(row_i < col_i, ab_c - abp[h:h + 1, :], ldg[h:h + 1, :]))
                    ms.append((gmat * jnp.exp2(dm)).astype(BF16))
                    efs.append(jnp.exp2(pf_c))
                    ebs.append(jnp.exp2(ab_c))
                ydiag = jnp.dot(jnp.concatenate(ms, axis=1), _pair_rhs(xs, p), preferred_element_type=F32)
                y_pairs.append(ydiag
                               + _pair_select(lane_full, efs[0], efs[1]) * yoff[:, q * LANES:(q + 1) * LANES]
                               + _pair_select(lane_full, ebs[0], ebs[1]) * yoff[:, (2 + q) * LANES:(3 + q) * LANES])
        y = jnp.concatenate(y_pairs, axis=1) + dsk_ref[...] * xs.astype(F32)
        yz = y * _silu(z_ref[0, pl.ds(r0, CHUNK), :].astype(F32))
        yn = _rms(yz, gssm_ref[...]).astype(BF16)

        drows = pl.ds(pl.multiple_of(j * tm + r0, CHUNK), CHUNK)
        ps = [jnp.dot(d_scr[drows, gi * LANES:(gi + 1) * LANES], wpool_ref[gi], preferred_element_type=F32)
              for gi in range(len(POOL_WINDOWS))]
        pm = jnp.concatenate(ps, axis=1) * pscale_ref[...]
        mix_scr[pl.ds(r0, CHUNK), 0:D_SSM] = yn
        mix_scr[pl.ds(r0, CHUNK), D_SSM:D_SSM + D_POOL] = pm.astype(BF16)
        yx = jnp.dot(mix_scr[pl.ds(r0, CHUNK), :], wout_ref[...], preferred_element_type=F32)
        o_ref[0, pl.ds(r0, CHUNK), :] = x_ref[0, pl.ds(r0, CHUNK), :] + gate_ref[0] * _rms(yx, gn_ref[...])


def _pool_constants(n):
    rows = n // GRID_W
    t = np.arange(LANES)
    blk, col = t // GRID_W, t % GRID_W
    cms, invs = [], []
    r = np.arange(n) // GRID_W
    c = np.arange(n) % GRID_W
    for w in POOL_WINDOWS:
        d = col[None, :] - col[:, None]
        cms.append(((blk[:, None] == blk[None, :]) & (d >= -(w // 2)) & (d < w - w // 2)).astype(np.float32))
        cnt_r = np.minimum(r + w - w // 2, rows) - np.maximum(r - w // 2, 0)
        cnt_c = np.minimum(c + w - w // 2, GRID_W) - np.maximum(c - w // 2, 0)
        invs.append(np.broadcast_to((1.0 / (cnt_r * cnt_c))[:, None], (n, LANES)).astype(np.float32))
    return np.stack(cms), np.stack(invs)


def _mix_out(xs, bc, zu, rows, cols, ds, cds, crows, x, gate, gn, dsk, gssm, wpool, pscale, wout, tm):
    b, n, d = x.shape
    nchunks = n // CHUNK
    cm_np, inv_np = _pool_constants(n)
    cm = jnp.asarray(cm_np, BF16)
    inv = jnp.asarray(inv_np, F32)
    full = lambda a: pl.BlockSpec(a.shape, lambda i, j: (0,) * a.ndim)
    per_batch = lambda a: pl.BlockSpec((1,) + a.shape[1:], lambda i, j: (i,) + (0,) * (a.ndim - 1))
    return pl.pallas_call(
        _mix_out_kernel,
        grid=(b, n // tm),
        in_specs=[pl.BlockSpec((1, tm, D_SSM), lambda i, j: (i, j, 0)),
                  pl.BlockSpec((1, tm, D_BC), lambda i, j: (i, j, 0)),
                  pl.BlockSpec((1, tm, D_SSM), lambda i, j: (i, j, 0)),
                  pl.BlockSpec((1, n, D_POOL), lambda i, j: (i, 0, 1)),
                  per_batch(rows), per_batch(cols), per_batch(ds), per_batch(cds), per_batch(crows),
                  pl.BlockSpec((1, tm, d), lambda i, j: (i, j, 0)),
                  pl.BlockSpec((1, 1, d), lambda i, j: (i, 0, 0)),
                  full(gn), full(dsk), full(gssm), full(wpool), full(pscale), full(wout), full(cm),
                  pl.BlockSpec(inv.shape, lambda i, j: (0, 0, 0), pipeline_mode=pl.Buffered(1))],
        out_specs=pl.BlockSpec((1, tm, d), lambda i, j: (i, j, 0)),
        out_shape=jax.ShapeDtypeStruct((b, n, d), F32),
        scratch_shapes=[pltpu.VMEM((nchunks, SSD_PAIRS, SSD_STATE, LANES), F32),
                        pltpu.VMEM((nchunks, SSD_PAIRS, SSD_STATE, LANES), F32),
                        pltpu.VMEM((2, SSD_PAIRS, SSD_STATE, LANES), F32),
                        pltpu.VMEM((n, D_POOL), BF16),
                        pltpu.VMEM((tm, D_SSM + D_POOL), BF16)],
        compiler_params=pltpu.CompilerParams(dimension_semantics=("arbitrary", "arbitrary"),
                                             vmem_limit_bytes=VMEM_LIMIT),
        name="mix_out",
    )(xs, bc, zu, zu, rows, cols, ds, cds, crows, x, gate, gn, dsk, gssm, wpool, pscale, wout, cm, inv)


FFN_COLS = 2 * LANES
FFN_RING = 2


def _ffn_kernel(x_ref, xp_ref, xn_ref, shift_ref, scale_ref, gate_ref, gpre_ref, gpost_ref,
                wup_ref, cw_ref, cb_ref, wdown_ref, o_ref, act_scr, up_scr, af_scr):
    tm = x_ref.shape[1]
    dff = wdown_ref.shape[0]
    j = pl.program_id(1)
    nj = pl.num_programs(1)

    def hmod(v):
        return _rms(v, gpre_ref[...]) * (1.0 + scale_ref[0]) + shift_ref[0]

    x = x_ref[0]
    hp = jnp.where(j > 0, hmod(xp_ref[0]), 0.0)
    hn = jnp.where(j < nj - 1, hmod(xn_ref[0]), 0.0)
    hext = jnp.concatenate([hp, hmod(x), hn], axis=0).astype(BF16)

    nsl = FFN_COLS // LANES
    for cidx in range(dff // FFN_COLS):
        slot = cidx % FFN_RING
        conv = []
        for half, base in enumerate((cidx * FFN_COLS, dff + cidx * FFN_COLS)):
            up = jnp.dot(hext, wup_ref[:, base:base + FFN_COLS], preferred_element_type=F32)
            for s in range(nsl):
                lanes = slice(base + s * LANES, base + (s + 1) * LANES)
                slab = (2 * slot + half) * nsl + s
                up_scr[slab] = up[:, s * LANES:(s + 1) * LANES]
                conv.append([cb_ref[:, lanes] + sum(
                    cw_ref[k:k + 1, lanes] * up_scr[slab, pl.ds(HALO - 1 + par + k, tm // 2, stride=2), :]
                    for k in range(3)) for par in range(2)])
        for s in range(nsl):
            for par in range(2):
                af_scr[slot * nsl + s, pl.ds(par, tm // 2, stride=2), :] = (
                    _silu(conv[nsl + s][par]) * conv[s][par])
            lo = cidx * FFN_COLS + s * LANES
            act_scr[:, lo:lo + LANES] = af_scr[slot * nsl + s].astype(BF16)

    f = jnp.dot(act_scr[...], wdown_ref[...], preferred_element_type=F32)
    o_ref[0] = x + gate_ref[0] * _rms(f, gpost_ref[...])


def _ffn(x, shift, scale, gate, gpre, gpost, wup, cw, cb, wdown, tm):
    b, n, d = x.shape
    dff = wdown.shape[0]
    return pl.pallas_call(
        _ffn_kernel,
        grid=(b, n // tm),
        in_specs=[pl.BlockSpec((1, tm, d), lambda i, j: (i, j, 0))] + _halo_specs(tm, n, d) + [
            pl.BlockSpec((1, 1, d), lambda i, j: (i, 0, 0)),
            pl.BlockSpec((1, 1, d), lambda i, j: (i, 0, 0)),
            pl.BlockSpec((1, 1, d), lambda i, j: (i, 0, 0)),
            pl.BlockSpec((1, d), lambda i, j: (0, 0)),
            pl.BlockSpec((1, d), lambda i, j: (0, 0)),
            pl.BlockSpec(wup.shape, lambda i, j: (0, 0), pipeline_mode=pl.Buffered(1)),
            pl.BlockSpec(cw.shape, lambda i, j: (0, 0)),
            pl.BlockSpec(cb.shape, lambda i, j: (0, 0)),
            pl.BlockSpec(wdown.shape, lambda i, j: (0, 0), pipeline_mode=pl.Buffered(1))],
        out_specs=pl.BlockSpec((1, tm, d), lambda i, j: (i, j, 0)),
        out_shape=jax.ShapeDtypeStruct((b, n, d), F32),
        scratch_shapes=[pltpu.VMEM((tm, dff), BF16),
                        pltpu.VMEM((2 * FFN_RING * FFN_COLS // LANES, tm + 2 * HALO, LANES), F32),
                        pltpu.VMEM((FFN_RING * FFN_COLS // LANES, tm, LANES), F32)],
        compiler_params=pltpu.CompilerParams(dimension_semantics=("arbitrary", "arbitrary"),
                                             vmem_limit_bytes=VMEM_LIMIT),
        name="ffn",
    )(x, x, x, shift, scale, gate, gpre, gpost, wup, cw, cb, wdown)


def kernel(x, c, ctx, c_ctx, pre_norm_mix, post_norm_mix, pre_norm_ffn, post_norm_ffn, w_ada, b_ada,
           w_in, conv_ssd_w, conv_ssd_b, dt_bias, a_log, d_skip, ssm_norm, w_pool, pool_scale, w_out,
           w_up, conv_ffn_w, conv_ffn_b, w_down):
    assert w_ada.shape[0] == 1, "single-layer block"
    b, n, d = x.shape
    n_ctx = ctx.shape[1]
    d_xbc = D_SSM + D_BC
    d_xb = D_SSM + D_BC // 2
    off_dt = D_SSM + d_xbc
    off_pool = off_dt + 2 * SSD_HEADS
    tm_in = 1024

    cc = jnp.zeros((16, d), F32).at[:b].set(c).at[b].set(c_ctx)
    mods = _ada(cc, w_ada[0], b_ada).reshape(16, 6, d)
    mx = [mods[:b, k][:, None, :] for k in range(6)]
    mc = [jnp.broadcast_to(mods[b, k][None, None, :], (b, 1, d)) for k in range(2)]

    wi = w_in[0]
    w_all = wi.astype(BF16)
    w_pd = jnp.concatenate([wi[:, off_pool:], wi[:, off_dt:off_pool],
                            jnp.zeros((d, LANES - 2 * SSD_HEADS), F32)], axis=1).astype(BF16)
    bias = jnp.broadcast_to(dt_bias[0].reshape(2 * SSD_HEADS, 1), (2 * SSD_HEADS, tm_in))
    alog = jnp.broadcast_to(a_log[0].reshape(2 * SSD_HEADS, 1), (2 * SSD_HEADS, tm_in))

    xs, bc, zu, rows, cols, ds = _mix_in(x, mx[0], mx[1], pre_norm_mix, w_all, w_pd, conv_ssd_w[0],
                                         conv_ssd_b, bias, alog, tm_in, True, "mix_in")
    crows, cds = _mix_in(ctx, mc[0], mc[1], pre_norm_mix, w_all, w_pd, conv_ssd_w[0, :, :d_xb],
                         conv_ssd_b[:, :d_xb], bias, alog, tm_in, False, "mix_in_ctx")

    dsk = jnp.repeat(d_skip[0], SSD_HEAD_DIM)[None, :]
    x1 = _mix_out(xs, bc, zu, rows, cols, ds, cds, crows, x, mx[2], post_norm_mix, dsk, ssm_norm,
                  w_pool[0].astype(BF16), pool_scale, w_out[0].astype(BF16), 512)

    return _ffn(x1, mx[3], mx[4], mx[5], pre_norm_ffn, post_norm_ffn, w_up[0].astype(BF16),
                conv_ffn_w[0], conv_ffn_b, w_down[0].astype(BF16), 1024)
```

```python
import functools

import numpy as np
import jax
import jax.numpy as jnp
from jax import lax
from jax.experimental import pallas as pl
from jax.experimental.pallas import tpu as pltpu

F32 = jnp.float32
BF16 = jnp.bfloat16

EPS = 1e-6
GRID_W = 64
SSD_HEADS = 8
SSD_HEAD_DIM = 64
SSD_GROUPS = 2
SSD_STATE = 128
SSD_PAIRS = SSD_HEADS // 2
SSD_CONV = 5
D_SSM = SSD_HEADS * SSD_HEAD_DIM
D_BC = 2 * SSD_GROUPS * SSD_STATE
POOL_WINDOWS = (2, 4, 8, 16)
D_POOL = 512
CHUNK = 128
LANES = 128
HALO = 8
N_ROWS = 9
VMEM_LIMIT = 56 * 1024 * 1024


def _silu(v):
    return v / (1.0 + jnp.exp(-v))


def _rms(v, gain):
    ms = jnp.mean(v * v, axis=-1, keepdims=True)
    return v * lax.rsqrt(ms + EPS) * gain


def _halo_specs(tm, n, d):
    hb, nb = tm // HALO, n // HALO
    return [pl.BlockSpec((1, HALO, d), lambda i, j: (i, jnp.maximum(j * hb - 1, 0), 0)),
            pl.BlockSpec((1, HALO, d), lambda i, j: (i, jnp.minimum((j + 1) * hb, nb - 1), 0))]


def _ada_kernel(c_ref, w_ref, b_ref, o_ref):
    s = _silu(c_ref[...]).astype(BF16)
    o_ref[...] = jnp.dot(s, w_ref[...].astype(BF16), preferred_element_type=F32) + b_ref[...]


def _ada(cc, w_ada, b_ada):
    rows, d = cc.shape
    n = w_ada.shape[1]
    tn = 1536
    return pl.pallas_call(
        _ada_kernel,
        grid=(n // tn,),
        in_specs=[pl.BlockSpec((rows, d), lambda j: (0, 0)),
                  pl.BlockSpec((d, tn), lambda j: (0, j)),
                  pl.BlockSpec((1, tn), lambda j: (0, j))],
        out_specs=pl.BlockSpec((rows, tn), lambda j: (0, j)),
        out_shape=jax.ShapeDtypeStruct((rows, n), F32),
        compiler_params=pltpu.CompilerParams(dimension_semantics=("arbitrary",),
                                             vmem_limit_bytes=VMEM_LIMIT),
        name="ada",
    )(cc, w_ada, b_ada)


def _dt_rows(dtraw_chunks, bias, a):
    h = SSD_HEADS
    raw = jnp.concatenate([blk.T[0:2 * h, :] for blk in dtraw_chunks], axis=1) + bias
    t = raw.shape[1]
    dt = jnp.maximum(raw, 0.0) + jnp.log1p(jnp.exp(-jnp.abs(raw)))
    dta = dt * a
    seg = lax.broadcasted_iota(jnp.int32, dta.shape, 1) & (CHUNK - 1)
    cs, rcs = dta, dta
    k = 1
    while k < CHUNK:
        cs = cs + jnp.where(seg >= k, pltpu.roll(cs, k, 1), 0.0)
        rcs = rcs + jnp.where(seg < CHUNK - k, pltpu.roll(rcs, t - k, 1), 0.0)
        k *= 2
    cd = jnp.exp(cs + rcs - dta)
    wf = jnp.exp(rcs[0:h] - dta[0:h]) * dt[0:h]
    wb = jnp.exp(cs[h:] - dta[h:]) * dt[h:]
    log2e = 1.0 / np.log(2.0)
    pf2, ab2 = cs[0:h] * log2e, rcs[h:] * log2e
    ldt2 = jnp.log(dt) * log2e
    ldg2 = jnp.log(dt[0:h] + dt[h:]) * log2e
    return [pf2, ab2, pf2 - ldt2[0:h], ab2 - ldt2[h:], ldg2, wf, wb, cd[0:h], cd[h:]]


def _pair_rhs(xs, p):
    xp = xs[:, p * LANES:(p + 1) * LANES]
    lane = lax.broadcasted_iota(jnp.int32, xp.shape, 1)
    zero = jnp.zeros_like(xp)
    return jnp.concatenate([jnp.where(lane < SSD_HEAD_DIM, xp, zero),
                            jnp.where(lane >= SSD_HEAD_DIM, xp, zero)], axis=0)


def _pair_select(lane, v0, v1):
    return jnp.where(lane < SSD_HEAD_DIM, v0, v1)


def _local_states(xs, bts, wf, wb):
    out_f, out_b = [], []
    for p in range(SSD_PAIRS):
        bt = bts[p // 2]
        h0, h1 = 2 * p, 2 * p + 1
        lhs = jnp.concatenate(
            [jnp.concatenate([(bt * w[h0:h0 + 1, :]).astype(BF16), (bt * w[h1:h1 + 1, :]).astype(BF16)], axis=1)
             for w in (wf, wb)], axis=0)
        res = jnp.dot(lhs, _pair_rhs(xs, p), preferred_element_type=F32)
        out_f.append(res[0:SSD_STATE])
        out_b.append(res[SSD_STATE:2 * SSD_STATE])
    return out_f, out_b


def _mix_in_kernel(*refs, conv_cols, latent):
    (x_ref, xp_ref, xn_ref, shift_ref, scale_ref, g_ref, w_ref, wpd_ref, cw_ref, cb_ref,
     bias_ref, alog_ref) = refs[:12]
    if latent:
        xs_ref, bc_ref, zu_ref, rows_ref, cols_ref, ds_ref, up_scr, cv_scr = refs[12:]
    else:
        rows_ref, ds_ref, up_scr, cv_scr = refs[12:]
    nseg, seg, d = x_ref.shape
    tm = nseg * seg
    nch = tm // CHUNK
    cps = seg // CHUNK
    j = pl.program_id(1)
    nj = pl.num_programs(1)

    def hmod(v):
        return _rms(v, g_ref[...]) * (1.0 + scale_ref[0]) + shift_ref[0]

    xm = hmod(x_ref[...].reshape(tm, d))
    gap = jnp.zeros((HALO, d), F32)
    if latent:
        hp = jnp.where(j > 0, hmod(xp_ref[0]), 0.0)
        hn = jnp.where(j < nj - 1, hmod(xn_ref[0]), 0.0)
    else:
        hp = hn = gap
    pieces = [hp]
    for i in range(nseg):
        pieces += [xm[i * seg:(i + 1) * seg], gap if i < nseg - 1 else hn]
    ext_rows = up_scr.shape[1]
    if ext_rows > tm + (nseg + 1) * HALO:
        pieces.append(jnp.zeros((ext_rows - tm - (nseg + 1) * HALO, d), F32))
    hext = jnp.concatenate(pieces, axis=0).astype(BF16)
    hm = xm.astype(BF16)

    half = SSD_CONV // 2
    cc = 2 * LANES
    for cb in range(conv_cols // cc):
        up = jnp.dot(hext, w_ref[:, D_SSM + cb * cc:D_SSM + (cb + 1) * cc], preferred_element_type=F32)
        for s in range(cc // LANES):
            up_scr[cb * (cc // LANES) + s] = up[:, s * LANES:(s + 1) * LANES]
    for slab in range(conv_cols // LANES):
        lanes = slice(slab * LANES, (slab + 1) * LANES)
        for i in range(nseg):
            for par in range(2):
                lo = HALO + i * (seg + HALO) - half + par
                acc = cb_ref[:, lanes] + cw_ref[0:1, lanes] * up_scr[slab, pl.ds(lo, seg // 2, stride=2), :]
                for k in range(1, SSD_CONV):
                    acc = acc + cw_ref[k:k + 1, lanes] * up_scr[slab, pl.ds(lo + k, seg // 2, stride=2), :]
                cv_scr[slab, pl.ds(i * seg + par, seg // 2, stride=2), :] = _silu(acc)
        if latent:
            if slab < D_SSM // LANES:
                xs_ref[0, :, lanes] = cv_scr[slab].astype(BF16)
            else:
                bc_ref[0, :, slab * LANES - D_SSM:(slab + 1) * LANES - D_SSM] = cv_scr[slab].astype(BF16)

    if latent:
        zu_ref[0, :, 0:D_SSM] = jnp.dot(hm, w_ref[:, 0:D_SSM], preferred_element_type=F32).astype(BF16)
        pd = jnp.dot(hm, wpd_ref[...], preferred_element_type=F32)
        zu_ref[0, :, D_SSM:D_SSM + D_POOL] = pd[:, 0:D_POOL].astype(BF16)
        dtraw = pd[:, D_POOL:D_POOL + LANES]
    else:
        dtraw = jnp.dot(hm, wpd_ref[:, D_POOL:D_POOL + LANES], preferred_element_type=F32)

    rows = _dt_rows([dtraw[c * CHUNK:(c + 1) * CHUNK] for c in range(nch)], bias_ref[...],
                    -jnp.exp(alog_ref[...]))
    pad = jnp.zeros((LANES - 2 * SSD_HEADS, CHUNK), F32)
    for c in range(nch):
        lanes = slice(c * CHUNK, (c + 1) * CHUNK)
        for q, arr in enumerate(rows):
            rows_ref[c // cps, c % cps, q] = arr[:, lanes]
        if latent:
            cols_ref[0, c] = jnp.concatenate([rows[0][:, lanes], rows[1][:, lanes], pad], axis=0).T

    def state_body(c, carry):
        r0 = pl.multiple_of(c * CHUNK, CHUNK)
        ci, cj = c // cps, c % cps
        xs = jnp.concatenate([cv_scr[s, pl.ds(r0, CHUNK), :] for s in range(D_SSM // LANES)],
                             axis=1).astype(BF16)
        bts = [cv_scr[D_SSM // LANES + g, pl.ds(r0, CHUNK), :].T for g in range(SSD_GROUPS)]
        dsf, dsb = _local_states(xs, bts, rows_ref[ci, cj, 5], rows_ref[ci, cj, 6])
        for p in range(SSD_PAIRS):
            ds_ref[ci, cj, 0, p] = dsf[p].astype(BF16)
            ds_ref[ci, cj, 1, p] = dsb[p].astype(BF16)
        return carry

    lax.fori_loop(0, nch, state_body, 0, unroll=4)


def _mix_in(x, shift, scale, gain, w, wpd, conv_w, conv_b, bias, alog, tm, latent, name):
    b, n, d = x.shape
    conv_cols = conv_w.shape[1]
    nch = tm // CHUNK
    nseg = 1 if latent else tm // n
    seg = tm // nseg
    cps = seg // CHUNK
    ext_rows = -(-(tm + (nseg + 1) * HALO) // 16) * 16
    full = lambda a: pl.BlockSpec(a.shape, lambda i, j: (0,) * a.ndim)
    stat = lambda tail, dt: jax.ShapeDtypeStruct((b, n // CHUNK) + tail, dt)
    out_specs, out_shape = [], []
    if latent:
        grid = (b, n // tm)
        x_spec = pl.BlockSpec((1, tm, d), lambda i, j: (i, j, 0))
        mod_spec = pl.BlockSpec((1, 1, d), lambda i, j: (i, 0, 0))
        out_specs += [pl.BlockSpec((1, tm, D_SSM), lambda i, j: (i, j, 0)),
                      pl.BlockSpec((1, tm, conv_cols - D_SSM), lambda i, j: (i, j, 0)),
                      pl.BlockSpec((1, tm, D_SSM + D_POOL), lambda i, j: (i, j, 0))]
        out_shape += [jax.ShapeDtypeStruct((b, n, D_SSM), BF16),
                      jax.ShapeDtypeStruct((b, n, conv_cols - D_SSM), BF16),
                      jax.ShapeDtypeStruct((b, n, D_SSM + D_POOL), BF16)]
        blk = lambda tail: pl.BlockSpec((1, nch) + tail, lambda i, j: (i, j) + (0,) * len(tail))
    else:
        grid = (b // nseg, 1)
        x_spec = pl.BlockSpec((nseg, n, d), lambda i, j: (i, 0, 0))
        mod_spec = pl.BlockSpec((1, 1, d), lambda i, j: (0, 0, 0))
        blk = lambda tail: pl.BlockSpec((nseg, cps) + tail, lambda i, j: (i, 0) + (0,) * len(tail))
    out_specs.append(blk((N_ROWS, SSD_HEADS, CHUNK)))
    out_shape.append(stat((N_ROWS, SSD_HEADS, CHUNK), F32))
    if latent:
        out_specs.append(blk((CHUNK, LANES)))
        out_shape.append(stat((CHUNK, LANES), F32))
    out_specs.append(blk((2, SSD_PAIRS, SSD_STATE, LANES)))
    out_shape.append(stat((2, SSD_PAIRS, SSD_STATE, LANES), BF16))
    return pl.pallas_call(
        functools.partial(_mix_in_kernel, conv_cols=conv_cols, latent=latent),
        grid=grid,
        in_specs=[x_spec] + _halo_specs(seg, n, d) + [
            mod_spec, mod_spec,
            full(gain), full(w), full(wpd), full(conv_w), full(conv_b), full(bias), full(alog)],
        out_specs=out_specs,
        out_shape=out_shape,
        scratch_shapes=[pltpu.VMEM((conv_cols // LANES, ext_rows, LANES), F32),
                        pltpu.VMEM((conv_cols // LANES, tm, LANES), F32)],
        compiler_params=pltpu.CompilerParams(dimension_semantics=("arbitrary", "arbitrary"),
                                             vmem_limit_bytes=VMEM_LIMIT),
        name=name,
    )(x, x, x, shift, scale, gain, w, wpd, conv_w, conv_b, bias, alog)


def _mix_out_kernel(xs_ref, bc_ref, z_ref, u_ref, rows_ref, cols_ref, ds_ref, cds_ref, crows_ref,
                    x_ref, gate_ref, gn_ref, dsk_ref, gssm_ref, wpool_ref, pscale_ref, wout_ref,
                    cm_ref, inv_ref, o_ref,
                    sf_scr, sb_scr, st_scr, d_scr, mix_scr):
    n = u_ref.shape[1]
    tm = x_ref.shape[1]
    nchunks = n // CHUNK
    nctx = cds_ref.shape[1]
    tch = tm // CHUNK
    j = pl.program_id(1)
    lane_row = lax.broadcasted_iota(jnp.int32, (1, LANES), 1)

    def cd_row(ref, c, q, p):
        cd = ref[0, c, q]
        return _pair_select(lane_row, cd[2 * p:2 * p + 1, :], cd[2 * p + 1:2 * p + 2, :])

    @pl.when(j == 0)
    def _():
        for p in range(SSD_PAIRS):
            sf = cds_ref[0, 0, 0, p].astype(F32)
            for c in range(1, nctx):
                sf = sf * cd_row(crows_ref, c, 7, p) + cds_ref[0, c, 0, p].astype(F32)
            sb = cds_ref[0, nctx - 1, 1, p].astype(F32)
            for c in reversed(range(nctx - 1)):
                sb = sb * cd_row(crows_ref, c, 8, p) + cds_ref[0, c, 1, p].astype(F32)
            st_scr[0, p] = sf
            st_scr[1, p] = sb

        def scan_body(i, carry):
            cf = i
            cbk = nchunks - 1 - i
            for p in range(SSD_PAIRS):
                s_old = st_scr[0, p]
                sf_scr[cf, p] = s_old
                st_scr[0, p] = s_old * cd_row(rows_ref, cf, 7, p) + ds_ref[0, cf, 0, p].astype(F32)
                s_old = st_scr[1, p]
                sb_scr[cbk, p] = s_old
                st_scr[1, p] = s_old * cd_row(rows_ref, cbk, 8, p) + ds_ref[0, cbk, 1, p].astype(F32)
            return carry

        lax.fori_loop(0, nchunks, scan_body, 0)

        for gi, w in enumerate(POOL_WINDOWS):
            cols = slice(gi * LANES, (gi + 1) * LANES)
            cmat = cm_ref[gi]
            parts = [jnp.dot(cmat, u_ref[0, i * LANES:(i + 1) * LANES, cols], preferred_element_type=F32)
                     for i in range(n // LANES)]
            cs = jnp.concatenate(parts, axis=0)
            h = jnp.concatenate([jnp.zeros((w // 2 * GRID_W, LANES), F32), cs], axis=0)
            k = 1
            while k < w:
                sh = k * GRID_W
                h = h + jnp.concatenate([h[sh:], jnp.zeros((sh, LANES), F32)], axis=0)
                k *= 2
            m = h[0:n] * inv_ref[gi]
            d_scr[:, cols] = (m - u_ref[0, :, cols].astype(F32)).astype(BF16)

    row_i = lax.broadcasted_iota(jnp.int32, (CHUNK, CHUNK), 0)
    col_i = lax.broadcasted_iota(jnp.int32, (CHUNK, CHUNK), 1)
    lane_full = lax.broadcasted_iota(jnp.int32, (CHUNK, LANES), 1)

    for lc in range(tch):
        c = j * tch + lc
        r0 = lc * CHUNK
        xs = xs_ref[0, pl.ds(r0, CHUNK), :]
        pfp, abp, ldg = rows_ref[0, c, 2], rows_ref[0, c, 3], rows_ref[0, c, 4]
        cols = cols_ref[0, c]

        y_pairs = []
        for g in range(SSD_GROUPS):
            bg = bc_ref[0, pl.ds(r0, CHUNK), g * SSD_STATE:(g + 1) * SSD_STATE]
            cg = bc_ref[0, pl.ds(r0, CHUNK), (SSD_GROUPS + g) * SSD_STATE:(SSD_GROUPS + g + 1) * SSD_STATE]
            gmat = lax.dot_general(cg, bg, (((1,), (1,)), ((), ())), preferred_element_type=F32)
            scat = jnp.concatenate([sf_scr[c, 2 * g].astype(BF16), sf_scr[c, 2 * g + 1].astype(BF16),
                                    sb_scr[c, 2 * g].astype(BF16), sb_scr[c, 2 * g + 1].astype(BF16)], axis=1)
            yoff = jnp.dot(cg, scat, preferred_element_type=F32)
            for q in range(2):
                p = 2 * g + q
                ms, efs, ebs = [], [], []
                for h in (2 * p, 2 * p + 1):
                    pf_c = jnp.broadcast_to(cols[:, h:h + 1], (CHUNK, CHUNK))
                    ab_c = jnp.broadcast_to(cols[:, SSD_HEADS + h:SSD_HEADS + h + 1], (CHUNK, CHUNK))
                    dm = jnp.where(row_i > col_i, pf_c - pfp[h:h + 1, :],
                                   jnp.where(row_i < col_i, ab_c - abp[h:h + 1, :], ldg[h:h + 1, :]))
                    ms.append((gmat * jnp.exp2(dm)).astype(BF16))
                    efs.append(jnp.exp2(pf_c))
                    ebs.append(jnp.exp2(ab_c))
                ydiag = jnp.dot(jnp.concatenate(ms, axis=1), _pair_rhs(xs, p), preferred_element_type=F32)
                y_pairs.append(ydiag
                               + _pair_select(lane_full, efs[0], efs[1]) * yoff[:, q * LANES:(q + 1) * LANES]
                               + _pair_select(lane_full, ebs[0], ebs[1]) * yoff[:, (2 + q) * LANES:(3 + q) * LANES])
        y = jnp.concatenate(y_pairs, axis=1) + dsk_ref[...] * xs.astype(F32)
        yz = y * _silu(z_ref[0, pl.ds(r0, CHUNK), :].astype(F32))
        yn = _rms(yz, gssm_ref[...]).astype(BF16)

        drows = pl.ds(pl.multiple_of(j * tm + r0, CHUNK), CHUNK)
        ps = [jnp.dot(d_scr[drows, gi * LANES:(gi + 1) * LANES], wpool_ref[gi], preferred_element_type=F32)
              for gi in range(len(POOL_WINDOWS))]
        pm = jnp.concatenate(ps, axis=1) * pscale_ref[...]
        mix_scr[pl.ds(r0, CHUNK), 0:D_SSM] = yn
        mix_scr[pl.ds(r0, CHUNK), D_SSM:D_SSM + D_POOL] = pm.astype(BF16)
        yx = jnp.dot(mix_scr[pl.ds(r0, CHUNK), :], wout_ref[...], preferred_element_type=F32)
        o_ref[0, pl.ds(r0, CHUNK), :] = x_ref[0, pl.ds(r0, CHUNK), :] + gate_ref[0] * _rms(yx, gn_ref[...])


def _pool_constants(n):
    rows = n // GRID_W
    t = np.arange(LANES)
    blk, col = t // GRID_W, t % GRID_W
    cms, invs = [], []
    r = np.arange(n) // GRID_W
    c = np.arange(n) % GRID_W
    for w in POOL_WINDOWS:
        d = col[None, :] - col[:, None]
        cms.append(((blk[:, None] == blk[None, :]) & (d >= -(w // 2)) & (d < w - w // 2)).astype(np.float32))
        cnt_r = np.minimum(r + w - w // 2, rows) - np.maximum(r - w // 2, 0)
        cnt_c = np.minimum(c + w - w // 2, GRID_W) - np.maximum(c - w // 2, 0)
        invs.append(np.broadcast_to((1.0 / (cnt_r * cnt_c))[:, None], (n, LANES)).astype(np.float32))
    return np.stack(cms), np.stack(invs)


def _mix_out(xs, bc, zu, rows, cols, ds, cds, crows, x, gate, gn, dsk, gssm, wpool, pscale, wout, tm):
    b, n, d = x.shape
    nchunks = n // CHUNK
    cm_np, inv_np = _pool_constants(n)
    cm = jnp.asarray(cm_np, BF16)
    inv = jnp.asarray(inv_np, F32)
    full = lambda a: pl.BlockSpec(a.shape, lambda i, j: (0,) * a.ndim)
    per_batch = lambda a: pl.BlockSpec((1,) + a.shape[1:], lambda i, j: (i,) + (0,) * (a.ndim - 1))
    return pl.pallas_call(
        _mix_out_kernel,
        grid=(b, n // tm),
        in_specs=[pl.BlockSpec((1, tm, D_SSM), lambda i, j: (i, j, 0)),
                  pl.BlockSpec((1, tm, D_BC), lambda i, j: (i, j, 0)),
                  pl.BlockSpec((1, tm, D_SSM), lambda i, j: (i, j, 0)),
                  pl.BlockSpec((1, n, D_POOL), lambda i, j: (i, 0, 1)),
                  per_batch(rows), per_batch(cols), per_batch(ds), per_batch(cds), per_batch(crows),
                  pl.BlockSpec((1, tm, d), lambda i, j: (i, j, 0)),
                  pl.BlockSpec((1, 1, d), lambda i, j: (i, 0, 0)),
                  full(gn), full(dsk), full(gssm), full(wpool), full(pscale), full(wout), full(cm),
                  pl.BlockSpec(inv.shape, lambda i, j: (0, 0, 0), pipeline_mode=pl.Buffered(1))],
        out_specs=pl.BlockSpec((1, tm, d), lambda i, j: (i, j, 0)),
        out_shape=jax.ShapeDtypeStruct((b, n, d), F32),
        scratch_shapes=[pltpu.VMEM((nchunks, SSD_PAIRS, SSD_STATE, LANES), F32),
                        pltpu.VMEM((nchunks, SSD_PAIRS, SSD_STATE, LANES), F32),
                        pltpu.VMEM((2, SSD_PAIRS, SSD_STATE, LANES), F32),
                        pltpu.VMEM((n, D_POOL), BF16),
                        pltpu.VMEM((tm, D_SSM + D_POOL), BF16)],
        compiler_params=pltpu.CompilerParams(dimension_semantics=("arbitrary", "arbitrary"),
                                             vmem_limit_bytes=VMEM_LIMIT),
        name="mix_out",
    )(xs, bc, zu, zu, rows, cols, ds, cds, crows, x, gate, gn, dsk, gssm, wpool, pscale, wout, cm, inv)


FFN_COLS = 2 * LANES
FFN_RING = 2


def _ffn_kernel(x_ref, xp_ref, xn_ref, shift_ref, scale_ref, gate_ref, gpre_ref, gpost_ref,
                wup_ref, cw_ref, cb_ref, wdown_ref, o_ref, act_scr, up_scr, af_scr):
    tm = x_ref.shape[1]
    dff = wdown_ref.shape[0]
    j = pl.program_id(1)
    nj = pl.num_programs(1)

    def hmod(v):
        return _rms(v, gpre_ref[...]) * (1.0 + scale_ref[0]) + shift_ref[0]

    x = x_ref[0]
    hp = jnp.where(j > 0, hmod(xp_ref[0]), 0.0)
    hn = jnp.where(j < nj - 1, hmod(xn_ref[0]), 0.0)
    hext = jnp.concatenate([hp, hmod(x), hn], axis=0).astype(BF16)

    nsl = FFN_COLS // LANES
    for cidx in range(dff // FFN_COLS):
        slot = cidx % FFN_RING
        conv = []
        for half, base in enumerate((cidx * FFN_COLS, dff + cidx * FFN_COLS)):
            up = jnp.dot(hext, wup_ref[:, base:base + FFN_COLS], preferred_element_type=F32)
            for s in range(nsl):
                lanes = slice(base + s * LANES, base + (s + 1) * LANES)
                slab = (2 * slot + half) * nsl + s
                up_scr[slab] = up[:, s * LANES:(s + 1) * LANES]
                conv.append([cb_ref[:, lanes] + sum(
                    cw_ref[k:k + 1, lanes] * up_scr[slab, pl.ds(HALO - 1 + par + k, tm // 2, stride=2), :]
                    for k in range(3)) for par in range(2)])
        for s in range(nsl):
            for par in range(2):
                af_scr[slot * nsl + s, pl.ds(par, tm // 2, stride=2), :] = (
                    _silu(conv[nsl + s][par]) * conv[s][par])
            lo = cidx * FFN_COLS + s * LANES
            act_scr[:, lo:lo + LANES] = af_scr[slot * nsl + s].astype(BF16)

    f = jnp.dot(act_scr[...], wdown_ref[...], preferred_element_type=F32)
    o_ref[0] = x + gate_ref[0] * _rms(f, gpost_ref[...])


def _ffn(x, shift, scale, gate, gpre, gpost, wup, cw, cb, wdown, tm):
    b, n, d = x.shape
    dff = wdown.shape[0]
    return pl.pallas_call(
        _ffn_kernel,
        grid=(b, n // tm),
        in_specs=[pl.BlockSpec((1, tm, d), lambda i, j: (i, j, 0))] + _halo_specs(tm, n, d) + [
            pl.BlockSpec((1, 1, d), lambda i, j: (i, 0, 0)),
            pl.BlockSpec((1, 1, d), lambda i, j: (i, 0, 0)),
            pl.BlockSpec((1, 1, d), lambda i, j: (i, 0, 0)),
            pl.BlockSpec((1, d), lambda i, j: (0, 0)),
            pl.BlockSpec((1, d), lambda i, j: (0, 0)),
            pl.BlockSpec(wup.shape, lambda i, j: (0, 0), pipeline_mode=pl.Buffered(1)),
            pl.BlockSpec(cw.shape, lambda i, j: (0, 0)),
            pl.BlockSpec(cb.shape, lambda i, j: (0, 0)),
            pl.BlockSpec(wdown.shape, lambda i, j: (0, 0), pipeline_mode=pl.Buffered(1))],
        out_specs=pl.BlockSpec((1, tm, d), lambda i, j: (i, j, 0)),
        out_shape=jax.ShapeDtypeStruct((b, n, d), F32),
        scratch_shapes=[pltpu.VMEM((tm, dff), BF16),
                        pltpu.VMEM((2 * FFN_RING * FFN_COLS // LANES, tm + 2 * HALO, LANES), F32),
                        pltpu.VMEM((FFN_RING * FFN_COLS // LANES, tm, LANES), F32)],
        compiler_params=pltpu.CompilerParams(dimension_semantics=("arbitrary", "arbitrary"),
                                             vmem_limit_bytes=VMEM_LIMIT),
        name="ffn",
    )(x, x, x, shift, scale, gate, gpre, gpost, wup, cw, cb, wdown)


def kernel(x, c, ctx, c_ctx, pre_norm_mix, post_norm_mix, pre_norm_ffn, post_norm_ffn, w_ada, b_ada,
           w_in, conv_ssd_w, conv_ssd_b, dt_bias, a_log, d_skip, ssm_norm, w_pool, pool_scale, w_out,
           w_up, conv_ffn_w, conv_ffn_b, w_down):
    assert w_ada.shape[0] == 1, "single-layer block"
    b, n, d = x.shape
    n_ctx = ctx.shape[1]
    d_xbc = D_SSM + D_BC
    d_xb = D_SSM + D_BC // 2
    off_dt = D_SSM + d_xbc
    off_pool = off_dt + 2 * SSD_HEADS
    tm_in = 1024

    cc = jnp.zeros((16, d), F32).at[:b].set(c).at[b].set(c_ctx)
    mods = _ada(cc, w_ada[0], b_ada).reshape(16, 6, d)
    mx = [mods[:b, k][:, None, :] for k in range(6)]
    mc = [jnp.broadcast_to(mods[b, k][None, None, :], (b, 1, d)) for k in range(2)]

    wi = w_in[0]
    w_all = wi.astype(BF16)
    w_pd = jnp.concatenate([wi[:, off_pool:], wi[:, off_dt:off_pool],
                            jnp.zeros((d, LANES - 2 * SSD_HEADS), F32)], axis=1).astype(BF16)
    bias = jnp.broadcast_to(dt_bias[0].reshape(2 * SSD_HEADS, 1), (2 * SSD_HEADS, tm_in))
    alog = jnp.broadcast_to(a_log[0].reshape(2 * SSD_HEADS, 1), (2 * SSD_HEADS, tm_in))

    xs, bc, zu, rows, cols, ds = _mix_in(x, mx[0], mx[1], pre_norm_mix, w_all, w_pd, conv_ssd_w[0],
                                         conv_ssd_b, bias, alog, tm_in, True, "mix_in")
    crows, cds = _mix_in(ctx, mc[0], mc[1], pre_norm_mix, w_all, w_pd, conv_ssd_w[0, :, :d_xb],
                         conv_ssd_b[:, :d_xb], bias, alog, tm_in, False, "mix_in_ctx")

    dsk = jnp.repeat(d_skip[0], SSD_HEAD_DIM)[None, :]
    x1 = _mix_out(xs, bc, zu, rows, cols, ds, cds, crows, x, mx[2], post_norm_mix, dsk, ssm_norm,
                  w_pool[0].astype(BF16), pool_scale, w_out[0].astype(BF16), 512)

    return _ffn(x1, mx[3], mx[4], mx[5], pre_norm_ffn, post_norm_ffn, w_up[0].astype(BF16),
                conv_ffn_w[0], conv_ffn_b, w_down[0].astype(BF16), 1024)
```

```python
import functools

import numpy as np
import jax
import jax.numpy as jnp
from jax import lax
from jax.experimental import pallas as pl
from jax.experimental.pallas import tpu as pltpu

F32 = jnp.float32
BF16 = jnp.bfloat16

EPS = 1e-6
GRID_W = 64
SSD_HEADS = 8
SSD_HEAD_DIM = 64
SSD_GROUPS = 2
SSD_STATE = 128
SSD_PAIRS = SSD_HEADS // 2
SSD_CONV = 5
D_SSM = SSD_HEADS * SSD_HEAD_DIM
D_BC = 2 * SSD_GROUPS * SSD_STATE
POOL_WINDOWS = (2, 4, 8, 16)
D_POOL = 512
CHUNK = 128
LANES = 128
HALO = 8
N_ROWS = 9
VMEM_LIMIT = 56 * 1024 * 1024


def _silu(v):
    return v / (1.0 + jnp.exp(-v))


def _rms(v, gain):
    ms = jnp.mean(v * v, axis=-1, keepdims=True)
    return v * lax.rsqrt(ms + EPS) * gain


def _halo_specs(tm, n, d):
    hb, nb = tm // HALO, n // HALO
    return [pl.BlockSpec((1, HALO, d), lambda i, j: (i, jnp.maximum(j * hb - 1, 0), 0)),
            pl.BlockSpec((1, HALO, d), lambda i, j: (i, jnp.minimum((j + 1) * hb, nb - 1), 0))]


def _ada_kernel(c_ref, w_ref, b_ref, o_ref):
    s = _silu(c_ref[...]).astype(BF16)
    o_ref[...] = jnp.dot(s, w_ref[...].astype(BF16), preferred_element_type=F32) + b_ref[...]


def _ada(cc, w_ada, b_ada):
    rows, d = cc.shape
    n = w_ada.shape[1]
    tn = 1536
    return pl.pallas_call(
        _ada_kernel,
        grid=(n // tn,),
        in_specs=[pl.BlockSpec((rows, d), lambda j: (0, 0)),
                  pl.BlockSpec((d, tn), lambda j: (0, j)),
                  pl.BlockSpec((1, tn), lambda j: (0, j))],
        out_specs=pl.BlockSpec((rows, tn), lambda j: (0, j)),
        out_shape=jax.ShapeDtypeStruct((rows, n), F32),
        compiler_params=pltpu.CompilerParams(dimension_semantics=("arbitrary",),
                                             vmem_limit_bytes=VMEM_LIMIT),
        name="ada",
    )(cc, w_ada, b_ada)


def _dt_rows(dtraw_chunks, bias, a):
    h = SSD_HEADS
    raw = jnp.concatenate([blk.T[0:2 * h, :] for blk in dtraw_chunks], axis=1) + bias
    t = raw.shape[1]
    dt = jnp.maximum(raw, 0.0) + jnp.log1p(jnp.exp(-jnp.abs(raw)))
    dta = dt * a
    seg = lax.broadcasted_iota(jnp.int32, dta.shape, 1) & (CHUNK - 1)
    cs, rcs = dta, dta
    k = 1
    while k < CHUNK:
        cs = cs + jnp.where(seg >= k, pltpu.roll(cs, k, 1), 0.0)
        rcs = rcs + jnp.where(seg < CHUNK - k, pltpu.roll(rcs, t - k, 1), 0.0)
        k *= 2
    cd = jnp.exp(cs + rcs - dta)
    wf = jnp.exp(rcs[0:h] - dta[0:h]) * dt[0:h]
    wb = jnp.exp(cs[h:] - dta[h:]) * dt[h:]
    log2e = 1.0 / np.log(2.0)
    pf2, ab2 = cs[0:h] * log2e, rcs[h:] * log2e
    ldt2 = jnp.log(dt) * log2e
    ldg2 = jnp.log(dt[0:h] + dt[h:]) * log2e
    return [pf2, ab2, pf2 - ldt2[0:h], ab2 - ldt2[h:], ldg2, wf, wb, cd[0:h], cd[h:]]


def _pair_rhs(xs, p):
    xp = xs[:, p * LANES:(p + 1) * LANES]
    lane = lax.broadcasted_iota(jnp.int32, xp.shape, 1)
    zero = jnp.zeros_like(xp)
    return jnp.concatenate([jnp.where(lane < SSD_HEAD_DIM, xp, zero),
                            jnp.where(lane >= SSD_HEAD_DIM, xp, zero)], axis=0)


def _pair_select(lane, v0, v1):
    return jnp.where(lane < SSD_HEAD_DIM, v0, v1)


def _local_states(xs, bts, wf, wb):
    out_f, out_b = [], []
    for p in range(SSD_PAIRS):
        bt = bts[p // 2]
        h0, h1 = 2 * p, 2 * p + 1
        lhs = jnp.concatenate(
            [jnp.concatenate([(bt * w[h0:h0 + 1, :]).astype(BF16), (bt * w[h1:h1 + 1, :]).astype(BF16)], axis=1)
             for w in (wf, wb)], axis=0)
        res = jnp.dot(lhs, _pair_rhs(xs, p), preferred_element_type=F32)
        out_f.append(res[0:SSD_STATE])
        out_b.append(res[SSD_STATE:2 * SSD_STATE])
    return out_f, out_b


def _mix_in_kernel(*refs, conv_cols, latent):
    (x_ref, xp_ref, xn_ref, shift_ref, scale_ref, g_ref, w_ref, wpd_ref, cw_ref, cb_ref,
     bias_ref, alog_ref) = refs[:12]
    if latent:
        xs_ref, bc_ref, zu_ref, rows_ref, cols_ref, ds_ref, up_scr, cv_scr = refs[12:]
    else:
        rows_ref, ds_ref, up_scr, cv_scr = refs[12:]
    nseg, seg, d = x_ref.shape
    tm = nseg * seg
    nch = tm // CHUNK
    cps = seg // CHUNK
    j = pl.program_id(1)
    nj = pl.num_programs(1)

    def hmod(v):
        return _rms(v, g_ref[...]) * (1.0 + scale_ref[0]) + shift_ref[0]

    xm = hmod(x_ref[...].reshape(tm, d))
    gap = jnp.zeros((HALO, d), F32)
    if latent:
        hp = jnp.where(j > 0, hmod(xp_ref[0]), 0.0)
        hn = jnp.where(j < nj - 1, hmod(xn_ref[0]), 0.0)
    else:
        hp = hn = gap
    pieces = [hp]
    for i in range(nseg):
        pieces += [xm[i * seg:(i + 1) * seg], gap if i < nseg - 1 else hn]
    ext_rows = up_scr.shape[1]
    if ext_rows > tm + (nseg + 1) * HALO:
        pieces.append(jnp.zeros((ext_rows - tm - (nseg + 1) * HALO, d), F32))
    hext = jnp.concatenate(pieces, axis=0).astype(BF16)
    hm = xm.astype(BF16)

    half = SSD_CONV // 2
    cc = 2 * LANES
    for cb in range(conv_cols // cc):
        up = jnp.dot(hext, w_ref[:, D_SSM + cb * cc:D_SSM + (cb + 1) * cc], preferred_element_type=F32)
        for s in range(cc // LANES):
            up_scr[cb * (cc // LANES) + s] = up[:, s * LANES:(s + 1) * LANES]
    for slab in range(conv_cols // LANES):
        lanes = slice(slab * LANES, (slab + 1) * LANES)
        for i in range(nseg):
            for par in range(2):
                lo = HALO + i * (seg + HALO) - half + par
                acc = cb_ref[:, lanes] + cw_ref[0:1, lanes] * up_scr[slab, pl.ds(lo, seg // 2, stride=2), :]
                for k in range(1, SSD_CONV):
                    acc = acc + cw_ref[k:k + 1, lanes] * up_scr[slab, pl.ds(lo + k, seg // 2, stride=2), :]
                cv_scr[slab, pl.ds(i * seg + par, seg // 2, stride=2), :] = _silu(acc)
        if latent:
            if slab < D_SSM // LANES:
                xs_ref[0, :, lanes] = cv_scr[slab].astype(BF16)
            else:
                bc_ref[0, :, slab * LANES - D_SSM:(slab + 1) * LANES - D_SSM] = cv_scr[slab].astype(BF16)

    if latent:
        zu_ref[0, :, 0:D_SSM] = jnp.dot(hm, w_ref[:, 0:D_SSM], preferred_element_type=F32).astype(BF16)
        pd = jnp.dot(hm, wpd_ref[...], preferred_element_type=F32)
        zu_ref[0, :, D_SSM:D_SSM + D_POOL] = pd[:, 0:D_POOL].astype(BF16)
        dtraw = pd[:, D_POOL:D_POOL + LANES]
    else:
        dtraw = jnp.dot(hm, wpd_ref[:, D_POOL:D_POOL + LANES], preferred_element_type=F32)

    rows = _dt_rows([dtraw[c * CHUNK:(c + 1) * CHUNK] for c in range(nch)], bias_ref[...],
                    -jnp.exp(alog_ref[...]))
    pad = jnp.zeros((LANES - 2 * SSD_HEADS, CHUNK), F32)
    for c in range(nch):
        lanes = slice(c * CHUNK, (c + 1) * CHUNK)
        for q, arr in enumerate(rows):
            rows_ref[c // cps, c % cps, q] = arr[:, lanes]
        if latent:
            cols_ref[0, c] = jnp.concatenate([rows[0][:, lanes], rows[1][:, lanes], pad], axis=0).T

    def state_body(c, carry):
        r0 = pl.multiple_of(c * CHUNK, CHUNK)
        ci, cj = c // cps, c % cps
        xs = jnp.concatenate([cv_scr[s, pl.ds(r0, CHUNK), :] for s in range(D_SSM // LANES)],
                             axis=1).astype(BF16)
        bts = [cv_scr[D_SSM // LANES + g, pl.ds(r0, CHUNK), :].T for g in range(SSD_GROUPS)]
        dsf, dsb = _local_states(xs, bts, rows_ref[ci, cj, 5], rows_ref[ci, cj, 6])
        for p in range(SSD_PAIRS):
            ds_ref[ci, cj, 0, p] = dsf[p].astype(BF16)
            ds_ref[ci, cj, 1, p] = dsb[p].astype(BF16)
        return carry

    lax.fori_loop(0, nch, state_body, 0, unroll=4)


def _mix_in(x, shift, scale, gain, w, wpd, conv_w, conv_b, bias, alog, tm, latent, name):
    b, n, d = x.shape
    conv_cols = conv_w.shape[1]
    nch = tm // CHUNK
    nseg = 1 if latent else tm // n
    seg = tm // nseg
    cps = seg // CHUNK
    ext_rows = -(-(tm + (nseg + 1) * HALO) // 16) * 16
    full = lambda a: pl.BlockSpec(a.shape, lambda i, j: (0,) * a.ndim)
    stat = lambda tail, dt: jax.ShapeDtypeStruct((b, n // CHUNK) + tail, dt)
    out_specs, out_shape = [], []
    if latent:
        grid = (b, n // tm)
        x_spec = pl.BlockSpec((1, tm, d), lambda i, j: (i, j, 0))
        mod_spec = pl.BlockSpec((1, 1, d), lambda i, j: (i, 0, 0))
        out_specs += [pl.BlockSpec((1, tm, D_SSM), lambda i, j: (i, j, 0)),
                      pl.BlockSpec((1, tm, conv_cols - D_SSM), lambda i, j: (i, j, 0)),
                      pl.BlockSpec((1, tm, D_SSM + D_POOL), lambda i, j: (i, j, 0))]
        out_shape += [jax.ShapeDtypeStruct((b, n, D_SSM), BF16),
                      jax.ShapeDtypeStruct((b, n, conv_cols - D_SSM), BF16),
                      jax.ShapeDtypeStruct((b, n, D_SSM + D_POOL), BF16)]
        blk = lambda tail: pl.BlockSpec((1, nch) + tail, lambda i, j: (i, j) + (0,) * len(tail))
    else:
        grid = (b // nseg, 1)
        x_spec = pl.BlockSpec((nseg, n, d), lambda i, j: (i, 0, 0))
        mod_spec = pl.BlockSpec((1, 1, d), lambda i, j: (0, 0, 0))
        blk = lambda tail: pl.BlockSpec((nseg, cps) + tail, lambda i, j: (i, 0) + (0,) * len(tail))
    out_specs.append(blk((N_ROWS, SSD_HEADS, CHUNK)))
    out_shape.append(stat((N_ROWS, SSD_HEADS, CHUNK), F32))
    if latent:
        out_specs.append(blk((CHUNK, LANES)))
        out_shape.append(stat((CHUNK, LANES), F32))
    out_specs.append(blk((2, SSD_PAIRS, SSD_STATE, LANES)))
    out_shape.append(stat((2, SSD_PAIRS, SSD_STATE, LANES), BF16))
    return pl.pallas_call(
        functools.partial(_mix_in_kernel, conv_cols=conv_cols, latent=latent),
        grid=grid,
        in_specs=[x_spec] + _halo_specs(seg, n, d) + [
            mod_spec, mod_spec,
            full(gain), full(w), full(wpd), full(conv_w), full(conv_b), full(bias), full(alog)],
        out_specs=out_specs,
        out_shape=out_shape,
        scratch_shapes=[pltpu.VMEM((conv_cols // LANES, ext_rows, LANES), F32),
                        pltpu.VMEM((conv_cols // LANES, tm, LANES), F32)],
        compiler_params=pltpu.CompilerParams(dimension_semantics=("arbitrary", "arbitrary"),
                                             vmem_limit_bytes=VMEM_LIMIT),
        name=name,
    )(x, x, x, shift, scale, gain, w, wpd, conv_w, conv_b, bias, alog)


def _mix_out_kernel(xs_ref, bc_ref, z_ref, u_ref, rows_ref, cols_ref, ds_ref, cds_ref, crows_ref,
                    x_ref, gate_ref, gn_ref, dsk_ref, gssm_ref, wpool_ref, pscale_ref, wout_ref,
                    cm_ref, inv_ref, *rest):
    ncast = (len(rest) - 6) // 2
    o_ref = rest[ncast]
    sf_scr, sb_scr, st_scr, d_scr, mix_scr = rest[2 * ncast + 1:]
    for src, dst in zip(rest[:ncast], rest[ncast + 1:2 * ncast + 1]):
        dst[...] = src[...].astype(BF16)
    n = u_ref.shape[1]
    tm = x_ref.shape[1]
    nchunks = n // CHUNK
    nctx = cds_ref.shape[1]
    tch = tm // CHUNK
    j = pl.program_id(1)
    lane_row = lax.broadcasted_iota(jnp.int32, (1, LANES), 1)

    def cd_row(ref, c, q, p):
        cd = ref[0, c, q]
        return _pair_select(lane_row, cd[2 * p:2 * p + 1, :], cd[2 * p + 1:2 * p + 2, :])

    @pl.when(j == 0)
    def _():
        for p in range(SSD_PAIRS):
            sf = cds_ref[0, 0, 0, p].astype(F32)
            for c in range(1, nctx):
                sf = sf * cd_row(crows_ref, c, 7, p) + cds_ref[0, c, 0, p].astype(F32)
            sb = cds_ref[0, nctx - 1, 1, p].astype(F32)
            for c in reversed(range(nctx - 1)):
                sb = sb * cd_row(crows_ref, c, 8, p) + cds_ref[0, c, 1, p].astype(F32)
            st_scr[0, p] = sf
            st_scr[1, p] = sb

        def scan_body(i, carry):
            cf = i
            cbk = nchunks - 1 - i
            for p in range(SSD_PAIRS):
                s_old = st_scr[0, p]
                sf_scr[cf, p] = s_old
                st_scr[0, p] = s_old * cd_row(rows_ref, cf, 7, p) + ds_ref[0, cf, 0, p].astype(F32)
                s_old = st_scr[1, p]
                sb_scr[cbk, p] = s_old
                st_scr[1, p] = s_old * cd_row(rows_ref, cbk, 8, p) + ds_ref[0, cbk, 1, p].astype(F32)
            return carry

        lax.fori_loop(0, nchunks, scan_body, 0)

        for gi, w in enumerate(POOL_WINDOWS):
            cols = slice(gi * LANES, (gi + 1) * LANES)
            cmat = cm_ref[gi]
            parts = [jnp.dot(cmat, u_ref[0, i * LANES:(i + 1) * LANES, cols], preferred_element_type=F32)
                     for i in range(n // LANES)]
            cs = jnp.concatenate(parts, axis=0)
            h = jnp.concatenate([jnp.zeros((w // 2 * GRID_W, LANES), F32), cs], axis=0)
            k = 1
            while k < w:
                sh = k * GRID_W
                h = h + jnp.concatenate([h[sh:], jnp.zeros((sh, LANES), F32)], axis=0)
                k *= 2
            m = h[0:n] * inv_ref[gi]
            d_scr[:, cols] = (m - u_ref[0, :, cols].astype(F32)).astype(BF16)

    row_i = lax.broadcasted_iota(jnp.int32, (CHUNK, CHUNK), 0)
    col_i = lax.broadcasted_iota(jnp.int32, (CHUNK, CHUNK), 1)
    lane_full = lax.broadcasted_iota(jnp.int32, (CHUNK, LANES), 1)

    for lc in range(tch):
        c = j * tch + lc
        r0 = lc * CHUNK
        xs = xs_ref[0, pl.ds(r0, CHUNK), :]
        pfp, abp, ldg = rows_ref[0, c, 2], rows_ref[0, c, 3], rows_ref[0, c, 4]
        cols = cols_ref[0, c]

        y_pairs = []
        for g in range(SSD_GROUPS):
            bg = bc_ref[0, pl.ds(r0, CHUNK), g * SSD_STATE:(g + 1) * SSD_STATE]
            cg = bc_ref[0, pl.ds(r0, CHUNK), (SSD_GROUPS + g) * SSD_STATE:(SSD_GROUPS + g + 1) * SSD_STATE]
            gmat = lax.dot_general(cg, bg, (((1,), (1,)), ((), ())), preferred_element_type=F32)
            scat = jnp.concatenate([sf_scr[c, 2 * g].astype(BF16), sf_scr[c, 2 * g + 1].astype(BF16),
                                    sb_scr[c, 2 * g].astype(BF16), sb_scr[c, 2 * g + 1].astype(BF16)], axis=1)
            yoff = jnp.dot(cg, scat, preferred_element_type=F32)
            for q in range(2):
                p = 2 * g + q
                ms, efs, ebs = [], [], []
                for h in (2 * p, 2 * p + 1):
                    pf_c = jnp.broadcast_to(cols[:, h:h + 1], (CHUNK, CHUNK))
                    ab_c = jnp.broadcast_to(cols[:, SSD_HEADS + h:SSD_HEADS + h + 1], (CHUNK, CHUNK))
                    dm = jnp.where(row_i > col_i, pf_c - pfp[h:h + 1, :],
                                   jnp.where(row_i < col_i, ab_c - abp[h:h + 1, :], ldg[h:h + 1, :]))
                    ms.append((gmat * jnp.exp2(dm)).astype(BF16))
                    efs.append(jnp.exp2(pf_c))
                    ebs.append(jnp.exp2(ab_c))
                ydiag = jnp.dot(jnp.concatenate(ms, axis=1), _pair_rhs(xs, p), preferred_element_type=F32)
                y_pairs.append(ydiag
                               + _pair_select(lane_full, efs[0], efs[1]) * yoff[:, q * LANES:(q + 1) * LANES]
                               + _pair_select(lane_full, ebs[0], ebs[1]) * yoff[:, (2 + q) * LANES:(3 + q) * LANES])
        y = jnp.concatenate(y_pairs, axis=1) + dsk_ref[...] * xs.astype(F32)
        yz = y * _silu(z_ref[0, pl.ds(r0, CHUNK), :].astype(F32))
        yn = _rms(yz, gssm_ref[...]).astype(BF16)

        drows = pl.ds(pl.multiple_of(j * tm + r0, CHUNK), CHUNK)
        ps = [jnp.dot(d_scr[drows, gi * LANES:(gi + 1) * LANES], wpool_ref[gi], preferred_element_type=F32)
              for gi in range(len(POOL_WINDOWS))]
        pm = jnp.concatenate(ps, axis=1) * pscale_ref[...]
        mix_scr[pl.ds(r0, CHUNK), 0:D_SSM] = yn
        mix_scr[pl.ds(r0, CHUNK), D_SSM:D_SSM + D_POOL] = pm.astype(BF16)
        yx = jnp.dot(mix_scr[pl.ds(r0, CHUNK), :], wout_ref[...], preferred_element_type=F32)
        o_ref[0, pl.ds(r0, CHUNK), :] = x_ref[0, pl.ds(r0, CHUNK), :] + gate_ref[0] * _rms(yx, gn_ref[...])


def _pool_constants(n):
    rows = n // GRID_W
    t = np.arange(LANES)
    blk, col = t // GRID_W, t % GRID_W
    cms, invs = [], []
    r = np.arange(n) // GRID_W
    c = np.arange(n) % GRID_W
    for w in POOL_WINDOWS:
        d = col[None, :] - col[:, None]
        cms.append(((blk[:, None] == blk[None, :]) & (d >= -(w // 2)) & (d < w - w // 2)).astype(np.float32))
        cnt_r = np.minimum(r + w - w // 2, rows) - np.maximum(r - w // 2, 0)
        cnt_c = np.minimum(c + w - w // 2, GRID_W) - np.maximum(c - w // 2, 0)
        invs.append(np.broadcast_to((1.0 / (cnt_r * cnt_c))[:, None], (n, LANES)).astype(np.float32))
    return np.stack(cms), np.stack(invs)


def _mix_out(xs, bc, zu, rows, cols, ds, cds, crows, x, gate, gn, dsk, gssm, wpool, pscale, wout, casts, tm):
    b, n, d = x.shape
    nchunks = n // CHUNK
    nj = n // tm
    cm_np, inv_np = _pool_constants(n)
    cm = jnp.asarray(cm_np, BF16)
    inv = jnp.asarray(inv_np, F32)
    full = lambda a: pl.BlockSpec(a.shape, lambda i, j: (0,) * a.ndim)
    per_batch = lambda a: pl.BlockSpec((1,) + a.shape[1:], lambda i, j: (i,) + (0,) * (a.ndim - 1))
    cast_specs = []
    for a in casts:
        rps = next(r for r in range(16, a.shape[0] + 1, 16) if a.shape[0] % r == 0 and a.shape[0] // r <= b * nj)
        last = a.shape[0] // rps - 1
        cast_specs.append(pl.BlockSpec((rps, a.shape[1]), lambda i, j, last=last: (jnp.minimum(i * nj + j, last), 0)))
    outs = pl.pallas_call(
        _mix_out_kernel,
        grid=(b, n // tm),
        in_specs=[pl.BlockSpec((1, tm, D_SSM), lambda i, j: (i, j, 0)),
                  pl.BlockSpec((1, tm, D_BC), lambda i, j: (i, j, 0)),
                  pl.BlockSpec((1, tm, D_SSM), lambda i, j: (i, j, 0)),
                  pl.BlockSpec((1, n, D_POOL), lambda i, j: (i, 0, 1)),
                  per_batch(rows), per_batch(cols), per_batch(ds), per_batch(cds), per_batch(crows),
                  pl.BlockSpec((1, tm, d), lambda i, j: (i, j, 0)),
                  pl.BlockSpec((1, 1, d), lambda i, j: (i, 0, 0)),
                  full(gn), full(dsk), full(gssm), full(wpool), full(pscale), full(wout), full(cm),
                  pl.BlockSpec(inv.shape, lambda i, j: (0, 0, 0), pipeline_mode=pl.Buffered(1))] + cast_specs,
        out_specs=[pl.BlockSpec((1, tm, d), lambda i, j: (i, j, 0))] + cast_specs,
        out_shape=[jax.ShapeDtypeStruct((b, n, d), F32)] + [jax.ShapeDtypeStruct(a.shape, BF16) for a in casts],
        scratch_shapes=[pltpu.VMEM((nchunks, SSD_PAIRS, SSD_STATE, LANES), F32),
                        pltpu.VMEM((nchunks, SSD_PAIRS, SSD_STATE, LANES), F32),
                        pltpu.VMEM((2, SSD_PAIRS, SSD_STATE, LANES), F32),
                        pltpu.VMEM((n, D_POOL), BF16),
                        pltpu.VMEM((tm, D_SSM + D_POOL), BF16)],
        compiler_params=pltpu.CompilerParams(dimension_semantics=("arbitrary", "arbitrary"),
                                             vmem_limit_bytes=VMEM_LIMIT),
        name="mix_out",
    )(xs, bc, zu, zu, rows, cols, ds, cds, crows, x, gate, gn, dsk, gssm, wpool, pscale, wout, cm, inv, *casts)
    return outs[0], outs[1:]


FFN_COLS = 2 * LANES
FFN_RING = 2


def _ffn_kernel(x_ref, xp_ref, xn_ref, shift_ref, scale_ref, gate_ref, gpre_ref, gpost_ref,
                wup_ref, cw_ref, cb_ref, wdown_ref, o_ref, act_scr, up_scr, af_scr):
    tm = x_ref.shape[1]
    dff = wdown_ref.shape[0]
    j = pl.program_id(1)
    nj = pl.num_programs(1)

    def hmod(v):
        return _rms(v, gpre_ref[...]) * (1.0 + scale_ref[0]) + shift_ref[0]

    x = x_ref[0]
    hp = jnp.where(j > 0, hmod(xp_ref[0]), 0.0)
    hn = jnp.where(j < nj - 1, hmod(xn_ref[0]), 0.0)
    hext = jnp.concatenate([hp, hmod(x), hn], axis=0).astype(BF16)

    nsl = FFN_COLS // LANES
    for cidx in range(dff // FFN_COLS):
        slot = cidx % FFN_RING
        conv = []
        for half, base in enumerate((cidx * FFN_COLS, dff + cidx * FFN_COLS)):
            up = jnp.dot(hext, wup_ref[:, base:base + FFN_COLS], preferred_element_type=F32)
            for s in range(nsl):
                lanes = slice(base + s * LANES, base + (s + 1) * LANES)
                slab = (2 * slot + half) * nsl + s
                up_scr[slab] = up[:, s * LANES:(s + 1) * LANES]
                conv.append([cb_ref[:, lanes] + sum(
                    cw_ref[k:k + 1, lanes] * up_scr[slab, pl.ds(HALO - 1 + par + k, tm // 2, stride=2), :]
                    for k in range(3)) for par in range(2)])
        for s in range(nsl):
            for par in range(2):
                af_scr[slot * nsl + s, pl.ds(par, tm // 2, stride=2), :] = (
                    _silu(conv[nsl + s][par]) * conv[s][par])
            lo = cidx * FFN_COLS + s * LANES
            act_scr[:, lo:lo + LANES] = af_scr[slot * nsl + s].astype(BF16)

    f = jnp.dot(act_scr[...], wdown_ref[...], preferred_element_type=F32)
    o_ref[0] = x + gate_ref[0] * _rms(f, gpost_ref[...])


def _ffn(x, shift, scale, gate, gpre, gpost, wup, cw, cb, wdown, tm):
    b, n, d = x.shape
    dff = wdown.shape[0]
    return pl.pallas_call(
        _ffn_kernel,
        grid=(b, n // tm),
        in_specs=[pl.BlockSpec((1, tm, d), lambda i, j: (i, j, 0))] + _halo_specs(tm, n, d) + [
            pl.BlockSpec((1, 1, d), lambda i, j: (i, 0, 0)),
            pl.BlockSpec((1, 1, d), lambda i, j: (i, 0, 0)),
            pl.BlockSpec((1, 1, d), lambda i, j: (i, 0, 0)),
            pl.BlockSpec((1, d), lambda i, j: (0, 0)),
            pl.BlockSpec((1, d), lambda i, j: (0, 0)),
            pl.BlockSpec(wup.shape, lambda i, j: (0, 0), pipeline_mode=pl.Buffered(1)),
            pl.BlockSpec(cw.shape, lambda i, j: (0, 0)),
            pl.BlockSpec(cb.shape, lambda i, j: (0, 0)),
            pl.BlockSpec(wdown.shape, lambda i, j: (0, 0), pipeline_mode=pl.Buffered(1))],
        out_specs=pl.BlockSpec((1, tm, d), lambda i, j: (i, j, 0)),
        out_shape=jax.ShapeDtypeStruct((b, n, d), F32),
        scratch_shapes=[pltpu.VMEM((tm, dff), BF16),
                        pltpu.VMEM((2 * FFN_RING * FFN_COLS // LANES, tm + 2 * HALO, LANES), F32),
                        pltpu.VMEM((FFN_RING * FFN_COLS // LANES, tm, LANES), F32)],
        compiler_params=pltpu.CompilerParams(dimension_semantics=("arbitrary", "arbitrary"),
                                             vmem_limit_bytes=VMEM_LIMIT),
        name="ffn",
    )(x, x, x, shift, scale, gate, gpre, gpost, wup, cw, cb, wdown)


def kernel(x, c, ctx, c_ctx, pre_norm_mix, post_norm_mix, pre_norm_ffn, post_norm_ffn, w_ada, b_ada,
           w_in, conv_ssd_w, conv_ssd_b, dt_bias, a_log, d_skip, ssm_norm, w_pool, pool_scale, w_out,
           w_up, conv_ffn_w, conv_ffn_b, w_down):
    assert w_ada.shape[0] == 1, "single-layer block"
    b, n, d = x.shape
    n_ctx = ctx.shape[1]
    d_xbc = D_SSM + D_BC
    d_xb = D_SSM + D_BC // 2
    off_dt = D_SSM + d_xbc
    off_pool = off_dt + 2 * SSD_HEADS
    tm_in = 1024

    cc = jnp.zeros((16, d), F32).at[:b].set(c).at[b].set(c_ctx)
    mods = _ada(cc, w_ada[0], b_ada).reshape(16, 6, d)
    mx = [mods[:b, k][:, None, :] for k in range(6)]
    mc = [jnp.broadcast_to(mods[b, k][None, None, :], (b, 1, d)) for k in range(2)]

    wi = w_in[0]
    w_all = wi.astype(BF16)
    w_pd = jnp.concatenate([wi[:, off_pool:], wi[:, off_dt:off_pool],
                            jnp.zeros((d, LANES - 2 * SSD_HEADS), F32)], axis=1).astype(BF16)
    bias = jnp.broadcast_to(dt_bias[0].reshape(2 * SSD_HEADS, 1), (2 * SSD_HEADS, tm_in))
    alog = jnp.broadcast_to(a_log[0].reshape(2 * SSD_HEADS, 1), (2 * SSD_HEADS, tm_in))

    xs, bc, zu, rows, cols, ds = _mix_in(x, mx[0], mx[1], pre_norm_mix, w_all, w_pd, conv_ssd_w[0],
                                         conv_ssd_b, bias, alog, tm_in, True, "mix_in")
    crows, cds = _mix_in(ctx, mc[0], mc[1], pre_norm_mix, w_all, w_pd, conv_ssd_w[0, :, :d_xb],
                         conv_ssd_b[:, :d_xb], bias, alog, tm_in, False, "mix_in_ctx")

    dsk = jnp.repeat(d_skip[0], SSD_HEAD_DIM)[None, :]
    x1, (w_up_b, w_down_b) = _mix_out(xs, bc, zu, rows, cols, ds, cds, crows, x, mx[2], post_norm_mix, dsk,
                                      ssm_norm, w_pool[0].astype(BF16), pool_scale, w_out[0].astype(BF16),
                                      [w_up[0], w_down[0]], 512)

    return _ffn(x1, mx[3], mx[4], mx[5], pre_norm_ffn, post_norm_ffn, w_up_b,
                conv_ffn_w[0], conv_ffn_b, w_down_b, 1024)
```

```python
import functools

import numpy as np
import jax
import jax.numpy as jnp
from jax import lax
from jax.experimental import pallas as pl
from jax.experimental.pallas import tpu as pltpu

F32 = jnp.float32
BF16 = jnp.bfloat16

EPS = 1e-6
GRID_W = 64
SSD_HEADS = 8
SSD_HEAD_DIM = 64
SSD_GROUPS = 2
SSD_STATE = 128
SSD_PAIRS = SSD_HEADS // 2
SSD_CONV = 5
D_SSM = SSD_HEADS * SSD_HEAD_DIM
D_BC = 2 * SSD_GROUPS * SSD_STATE
POOL_WINDOWS = (2, 4, 8, 16)
D_POOL = 512
CHUNK = 128
LANES = 128
HALO = 8
N_ROWS = 9
VMEM_LIMIT = 56 * 1024 * 1024


def _silu(v):
    return v / (1.0 + jnp.exp(-v))


def _rms(v, gain):
    ms = jnp.mean(v * v, axis=-1, keepdims=True)
    return v * lax.rsqrt(ms + EPS) * gain


N_MODS = 6


def _mod_spec(d, k, row=None):
    if row is None:
        return pl.BlockSpec((1, 1, d), lambda i, j: (i * N_MODS + k, 0, 0))
    return pl.BlockSpec((1, 1, d), lambda i, j: (row * N_MODS + k, 0, 0))


def _halo_specs(tm, n, d):
    hb, nb = tm // HALO, n // HALO
    return [pl.BlockSpec((1, HALO, d), lambda i, j: (i, jnp.maximum(j * hb - 1, 0), 0)),
            pl.BlockSpec((1, HALO, d), lambda i, j: (i, jnp.minimum((j + 1) * hb, nb - 1), 0))]


def _ada_kernel(c_ref, w_ref, b_ref, o_ref):
    s = _silu(c_ref[...]).astype(BF16)
    o_ref[...] = jnp.dot(s, w_ref[...].astype(BF16), preferred_element_type=F32) + b_ref[...]


def _ada(cc, w_ada, b_ada):
    rows, d = cc.shape
    n = w_ada.shape[1]
    tn = 1536
    return pl.pallas_call(
        _ada_kernel,
        grid=(n // tn,),
        in_specs=[pl.BlockSpec((rows, d), lambda j: (0, 0)),
                  pl.BlockSpec((d, tn), lambda j: (0, j)),
                  pl.BlockSpec((1, tn), lambda j: (0, j))],
        out_specs=pl.BlockSpec((rows, tn), lambda j: (0, j)),
        out_shape=jax.ShapeDtypeStruct((rows, n), F32),
        compiler_params=pltpu.CompilerParams(dimension_semantics=("arbitrary",),
                                             vmem_limit_bytes=VMEM_LIMIT),
        name="ada",
    )(cc, w_ada, b_ada)


def _dt_rows(dtraw_chunks, bias, a):
    h = SSD_HEADS
    raw = jnp.concatenate([blk.T[0:2 * h, :] for blk in dtraw_chunks], axis=1) + bias
    t = raw.shape[1]
    dt = jnp.maximum(raw, 0.0) + jnp.log1p(jnp.exp(-jnp.abs(raw)))
    dta = dt * a
    seg = lax.broadcasted_iota(jnp.int32, dta.shape, 1) & (CHUNK - 1)
    cs, rcs = dta, dta
    k = 1
    while k < CHUNK:
        cs = cs + jnp.where(seg >= k, pltpu.roll(cs, k, 1), 0.0)
        rcs = rcs + jnp.where(seg < CHUNK - k, pltpu.roll(rcs, t - k, 1), 0.0)
        k *= 2
    cd = jnp.exp(cs + rcs - dta)
    wf = jnp.exp(rcs[0:h] - dta[0:h]) * dt[0:h]
    wb = jnp.exp(cs[h:] - dta[h:]) * dt[h:]
    log2e = 1.0 / np.log(2.0)
    pf2, ab2 = cs[0:h] * log2e, rcs[h:] * log2e
    ldt2 = jnp.log(dt) * log2e
    ldg2 = jnp.log(dt[0:h] + dt[h:]) * log2e
    return [pf2, ab2, pf2 - ldt2[0:h], ab2 - ldt2[h:], ldg2, wf, wb, cd[0:h], cd[h:]]


def _pair_rhs(xs, p):
    xp = xs[:, p * LANES:(p + 1) * LANES]
    lane = lax.broadcasted_iota(jnp.int32, xp.shape, 1)
    zero = jnp.zeros_like(xp)
    return jnp.concatenate([jnp.where(lane < SSD_HEAD_DIM, xp, zero),
                            jnp.where(lane >= SSD_HEAD_DIM, xp, zero)], axis=0)


def _pair_select(lane, v0, v1):
    return jnp.where(lane < SSD_HEAD_DIM, v0, v1)


def _local_states(xs, bts, wf, wb):
    out_f, out_b = [], []
    for p in range(SSD_PAIRS):
        bt = bts[p // 2]
        h0, h1 = 2 * p, 2 * p + 1
        lhs = jnp.concatenate(
            [jnp.concatenate([(bt * w[h0:h0 + 1, :]).astype(BF16), (bt * w[h1:h1 + 1, :]).astype(BF16)], axis=1)
             for w in (wf, wb)], axis=0)
        res = jnp.dot(lhs, _pair_rhs(xs, p), preferred_element_type=F32)
        out_f.append(res[0:SSD_STATE])
        out_b.append(res[SSD_STATE:2 * SSD_STATE])
    return out_f, out_b


def _mix_in_kernel(*refs, conv_cols, latent):
    (x_ref, xp_ref, xn_ref, shift_ref, scale_ref, g_ref, w_ref, wpd_ref, cw_ref, cb_ref,
     bias_ref, alog_ref) = refs[:12]
    if latent:
        xs_ref, bc_ref, zu_ref, rows_ref, cols_ref, ds_ref, up_scr, cv_scr = refs[12:]
    else:
        rows_ref, ds_ref, up_scr, cv_scr = refs[12:]
    nseg, seg, d = x_ref.shape
    tm = nseg * seg
    nch = tm // CHUNK
    cps = seg // CHUNK
    j = pl.program_id(1)
    nj = pl.num_programs(1)

    def hmod(v):
        return _rms(v, g_ref[...]) * (1.0 + scale_ref[0]) + shift_ref[0]

    xm = hmod(x_ref[...].reshape(tm, d))
    gap = jnp.zeros((HALO, d), F32)
    if latent:
        hp = jnp.where(j > 0, hmod(xp_ref[0]), 0.0)
        hn = jnp.where(j < nj - 1, hmod(xn_ref[0]), 0.0)
    else:
        hp = hn = gap
    hm = xm.astype(BF16)
    if latent:
        blocks = [(jnp.concatenate([hp, xm, hn], axis=0).astype(BF16), 0)]
        convs = [(HALO, 0, tm)]
    else:
        pieces = [gap]
        for i in range(nseg):
            pieces += [xm[i * seg:(i + 1) * seg], gap]
        ext_rows = up_scr.shape[1]
        if ext_rows > tm + (nseg + 1) * HALO:
            pieces.append(jnp.zeros((ext_rows - tm - (nseg + 1) * HALO, d), F32))
        blocks = [(jnp.concatenate(pieces, axis=0).astype(BF16), 0)]
        convs = [(HALO + i * (seg + HALO), i * seg, seg) for i in range(nseg)]

    half = SSD_CONV // 2
    cc = 2 * LANES
    for hblk, srow in blocks:
        for cb in range(conv_cols // cc):
            up = jnp.dot(hblk, w_ref[:, D_SSM + cb * cc:D_SSM + (cb + 1) * cc], preferred_element_type=F32)
            for s in range(cc // LANES):
                up_scr[cb * (cc // LANES) + s, srow:srow + hblk.shape[0], :] = up[:, s * LANES:(s + 1) * LANES]
    for slab in range(conv_cols // LANES):
        lanes = slice(slab * LANES, (slab + 1) * LANES)
        for drow, orow, nrows in convs:
            for par in range(2):
                lo = drow - half + par
                acc = cb_ref[:, lanes] + cw_ref[0:1, lanes] * up_scr[slab, pl.ds(lo, nrows // 2, stride=2), :]
                for k in range(1, SSD_CONV):
                    acc = acc + cw_ref[k:k + 1, lanes] * up_scr[slab, pl.ds(lo + k, nrows // 2, stride=2), :]
                cv_scr[slab, pl.ds(orow + par, nrows // 2, stride=2), :] = _silu(acc)
        if latent:
            if slab < D_SSM // LANES:
                xs_ref[0, :, lanes] = cv_scr[slab].astype(BF16)
            else:
                bc_ref[0, :, slab * LANES - D_SSM:(slab + 1) * LANES - D_SSM] = cv_scr[slab].astype(BF16)

    if latent:
        zu_ref[0, :, 0:D_SSM] = jnp.dot(hm, w_ref[:, 0:D_SSM], preferred_element_type=F32).astype(BF16)
        pd = jnp.dot(hm, wpd_ref[...], preferred_element_type=F32)
        zu_ref[0, :, D_SSM:D_SSM + D_POOL] = pd[:, 0:D_POOL].astype(BF16)
        dtraw = pd[:, D_POOL:D_POOL + LANES]
    else:
        dtraw = jnp.dot(hm, wpd_ref[:, D_POOL:D_POOL + LANES], preferred_element_type=F32)

    rows = _dt_rows([dtraw[c * CHUNK:(c + 1) * CHUNK] for c in range(nch)], bias_ref[...],
                    -jnp.exp(alog_ref[...]))
    pad = jnp.zeros((LANES - 2 * SSD_HEADS, CHUNK), F32)
    for c in range(nch):
        lanes = slice(c * CHUNK, (c + 1) * CHUNK)
        for q, arr in enumerate(rows):
            rows_ref[c // cps, c % cps, q] = arr[:, lanes]
        if latent:
            cols_ref[0, c] = jnp.concatenate([rows[0][:, lanes], rows[1][:, lanes], pad], axis=0).T

    def state_body(c, carry):
        r0 = pl.multiple_of(c * CHUNK, CHUNK)
        ci, cj = c // cps, c % cps
        xs = jnp.concatenate([cv_scr[s, pl.ds(r0, CHUNK), :] for s in range(D_SSM // LANES)],
                             axis=1).astype(BF16)
        bts = [cv_scr[D_SSM // LANES + g, pl.ds(r0, CHUNK), :].T for g in range(SSD_GROUPS)]
        dsf, dsb = _local_states(xs, bts, rows_ref[ci, cj, 5], rows_ref[ci, cj, 6])
        for p in range(SSD_PAIRS):
            ds_ref[ci, cj, 0, p] = dsf[p].astype(BF16)
            ds_ref[ci, cj, 1, p] = dsb[p].astype(BF16)
        return carry

    lax.fori_loop(0, nch, state_body, 0, unroll=4)


def _mix_in(x, mods, mod_row, gain, w, wpd, conv_w, conv_b, bias, alog, tm, latent, name):
    b, n, d = x.shape
    conv_cols = conv_w.shape[1]
    nch = tm // CHUNK
    nseg = 1 if latent else tm // n
    seg = tm // nseg
    cps = seg // CHUNK
    ext_rows = -(-(tm + (nseg + 1) * HALO) // 16) * 16
    full = lambda a: pl.BlockSpec(a.shape, lambda i, j: (0,) * a.ndim)
    stat = lambda tail, dt: jax.ShapeDtypeStruct((b, n // CHUNK) + tail, dt)
    out_specs, out_shape = [], []
    if latent:
        grid = (b, n // tm)
        x_spec = pl.BlockSpec((1, tm, d), lambda i, j: (i, j, 0))
        out_specs += [pl.BlockSpec((1, tm, D_SSM), lambda i, j: (i, j, 0)),
                      pl.BlockSpec((1, tm, conv_cols - D_SSM), lambda i, j: (i, j, 0)),
                      pl.BlockSpec((1, tm, D_SSM + D_POOL), lambda i, j: (i, j, 0))]
        out_shape += [jax.ShapeDtypeStruct((b, n, D_SSM), BF16),
                      jax.ShapeDtypeStruct((b, n, conv_cols - D_SSM), BF16),
                      jax.ShapeDtypeStruct((b, n, D_SSM + D_POOL), BF16)]
        blk = lambda tail: pl.BlockSpec((1, nch) + tail, lambda i, j: (i, j) + (0,) * len(tail))
    else:
        grid = (b // nseg, 1)
        x_spec = pl.BlockSpec((nseg, n, d), lambda i, j: (i, 0, 0))
        blk = lambda tail: pl.BlockSpec((nseg, cps) + tail, lambda i, j: (i, 0) + (0,) * len(tail))
    out_specs.append(blk((N_ROWS, SSD_HEADS, CHUNK)))
    out_shape.append(stat((N_ROWS, SSD_HEADS, CHUNK), F32))
    if latent:
        out_specs.append(blk((CHUNK, LANES)))
        out_shape.append(stat((CHUNK, LANES), F32))
    out_specs.append(blk((2, SSD_PAIRS, SSD_STATE, LANES)))
    out_shape.append(stat((2, SSD_PAIRS, SSD_STATE, LANES), BF16))
    return pl.pallas_call(
        functools.partial(_mix_in_kernel, conv_cols=conv_cols, latent=latent),
        grid=grid,
        in_specs=[x_spec] + _halo_specs(seg, n, d) + [
            _mod_spec(d, 0, mod_row), _mod_spec(d, 1, mod_row),
            full(gain), full(w), full(wpd), full(conv_w), full(conv_b), full(bias), full(alog)],
        out_specs=out_specs,
        out_shape=out_shape,
        scratch_shapes=[pltpu.VMEM((conv_cols // LANES, ext_rows, LANES), F32),
                        pltpu.VMEM((conv_cols // LANES, tm, LANES), F32)],
        compiler_params=pltpu.CompilerParams(dimension_semantics=("arbitrary", "arbitrary"),
                                             vmem_limit_bytes=VMEM_LIMIT),
        name=name,
    )(x, x, x, mods, mods, gain, w, wpd, conv_w, conv_b, bias, alog)


def _mix_out_kernel(xs_ref, bc_ref, z_ref, u_ref, rows_ref, cols_ref, ds_ref, cds_ref, crows_ref,
                    x_ref, gate_ref, gn_ref, dsk_ref, gssm_ref, wpool_ref, pscale_ref, wout_ref,
                    cm_ref, inv_ref, *rest):
    ncast = (len(rest) - 5) // 2
    o_ref = rest[ncast]
    sf_scr, sb_scr, d_scr, mix_scr = rest[2 * ncast + 1:]
    n = u_ref.shape[1]
    tm = x_ref.shape[1]
    nchunks = n // CHUNK
    nctx = cds_ref.shape[1]
    tch = tm // CHUNK
    j = pl.program_id(1)
    lane_row = lax.broadcasted_iota(jnp.int32, (1, LANES), 1)

    def cd_row(ref, c, q, p):
        cd = ref[0, c, q]
        return _pair_select(lane_row, cd[2 * p:2 * p + 1, :], cd[2 * p + 1:2 * p + 2, :])

    @pl.when(j == 0)
    def _():
        for p in range(SSD_PAIRS):
            sf = cds_ref[0, 0, 0, p].astype(F32)
            for c in range(1, nctx):
                sf = sf * cd_row(crows_ref, c, 7, p) + cds_ref[0, c, 0, p].astype(F32)
            sb = cds_ref[0, nctx - 1, 1, p].astype(F32)
            for c in reversed(range(nctx - 1)):
                sb = sb * cd_row(crows_ref, c, 8, p) + cds_ref[0, c, 1, p].astype(F32)
            sf_scr[0, p] = sf
            sb_scr[nchunks - 1, p] = sb

        def scan_body(i, carry):
            cf = i
            cbk = nchunks - 1 - i
            for p in range(SSD_PAIRS):
                sf_scr[cf + 1, p] = (sf_scr[cf, p] * cd_row(rows_ref, cf, 7, p)
                                     + ds_ref[0, cf, 0, p].astype(F32))
                sb_scr[cbk - 1, p] = (sb_scr[cbk, p] * cd_row(rows_ref, cbk, 8, p)
                                      + ds_ref[0, cbk, 1, p].astype(F32))
            return carry

        lax.fori_loop(0, nchunks - 1, scan_body, 0)

        for gi, w in enumerate(POOL_WINDOWS):
            cols = slice(gi * LANES, (gi + 1) * LANES)
            cmat = cm_ref[gi]
            parts = [jnp.dot(cmat, u_ref[0, i * LANES:(i + 1) * LANES, cols], preferred_element_type=F32)
                     for i in range(n // LANES)]
            cs = jnp.concatenate(parts, axis=0)
            h = jnp.concatenate([jnp.zeros((w // 2 * GRID_W, LANES), F32), cs], axis=0)
            k = 1
            while k < w:
                sh = k * GRID_W
                h = h + jnp.concatenate([h[sh:], jnp.zeros((sh, LANES), F32)], axis=0)
                k *= 2
            m = h[0:n] * inv_ref[gi]
            d_scr[:, cols] = (m - u_ref[0, :, cols].astype(F32)).astype(BF16)

    row_i = lax.broadcasted_iota(jnp.int32, (CHUNK, CHUNK), 0)
    col_i = lax.broadcasted_iota(jnp.int32, (CHUNK, CHUNK), 1)
    lane_full = lax.broadcasted_iota(jnp.int32, (CHUNK, LANES), 1)

    for lc in range(tch):
        c = j * tch + lc
        r0 = lc * CHUNK
        xs = xs_ref[0, pl.ds(r0, CHUNK), :]
        pfp, abp, ldg = rows_ref[0, c, 2], rows_ref[0, c, 3], rows_ref[0, c, 4]
        cols = cols_ref[0, c]

        y_pairs = []
        for g in range(SSD_GROUPS):
            bg = bc_ref[0, pl.ds(r0, CHUNK), g * SSD_STATE:(g + 1) * SSD_STATE]
            cg = bc_ref[0, pl.ds(r0, CHUNK), (SSD_GROUPS + g) * SSD_STATE:(SSD_GROUPS + g + 1) * SSD_STATE]
            gmat = lax.dot_general(cg, bg, (((1,), (1,)), ((), ())), preferred_element_type=F32)
            scat = jnp.concatenate([sf_scr[c, 2 * g].astype(BF16), sf_scr[c, 2 * g + 1].astype(BF16),
                                    sb_scr[c, 2 * g].astype(BF16), sb_scr[c, 2 * g + 1].astype(BF16)], axis=1)
            yoff = jnp.dot(cg, scat, preferred_element_type=F32)
            for q in range(2):
                p = 2 * g + q
                ms, efs, ebs = [], [], []
                for h in (2 * p, 2 * p + 1):
                    pf_c = jnp.broadcast_to(cols[:, h:h + 1], (CHUNK, CHUNK))
                    ab_c = jnp.broadcast_to(cols[:, SSD_HEADS + h:SSD_HEADS + h + 1], (CHUNK, CHUNK))
                    dm = jnp.where(row_i > col_i, pf_c - pfp[h:h + 1, :],
                                   jnp.where(row_i < col_i, ab_c - abp[h:h + 1, :], ldg[h:h + 1, :]))
                    ms.append((gmat * jnp.exp2(dm)).astype(BF16))
                    efs.append(jnp.exp2(pf_c))
                    ebs.append(jnp.exp2(ab_c))
                ydiag = jnp.dot(jnp.concatenate(ms, axis=1), _pair_rhs(xs, p), preferred_element_type=F32)
                y_pairs.append(ydiag
                               + _pair_select(lane_full, efs[0], efs[1]) * yoff[:, q * LANES:(q + 1) * LANES]
                               + _pair_select(lane_full, ebs[0], ebs[1]) * yoff[:, (2 + q) * LANES:(3 + q) * LANES])
        y = jnp.concatenate(y_pairs, axis=1) + dsk_ref[...] * xs.astype(F32)
        yz = y * _silu(z_ref[0, pl.ds(r0, CHUNK), :].astype(F32))
        yn = _rms(yz, gssm_ref[...]).astype(BF16)

        drows = pl.ds(pl.multiple_of(j * tm + r0, CHUNK), CHUNK)
        ps = [jnp.dot(d_scr[drows, gi * LANES:(gi + 1) * LANES], wpool_ref[gi], preferred_element_type=F32)
              for gi in range(len(POOL_WINDOWS))]
        pm = jnp.concatenate(ps, axis=1) * pscale_ref[...]
        mix_scr[pl.ds(r0, CHUNK), 0:D_SSM] = yn
        mix_scr[pl.ds(r0, CHUNK), D_SSM:D_SSM + D_POOL] = pm.astype(BF16)
        yx = jnp.dot(mix_scr[pl.ds(r0, CHUNK), :], wout_ref[...], preferred_element_type=F32)
        o_ref[0, pl.ds(r0, CHUNK), :] = x_ref[0, pl.ds(r0, CHUNK), :] + gate_ref[0] * _rms(yx, gn_ref[...])

    for src, dst in zip(rest[:ncast], rest[ncast + 1:2 * ncast + 1]):
        dst[...] = src[...].astype(BF16)


def _pool_constants(n):
    rows = n // GRID_W
    t = np.arange(LANES)
    blk, col = t // GRID_W, t % GRID_W
    cms, invs = [], []
    r = np.arange(n) // GRID_W
    c = np.arange(n) % GRID_W
    for w in POOL_WINDOWS:
        d = col[None, :] - col[:, None]
        cms.append(((blk[:, None] == blk[None, :]) & (d >= -(w // 2)) & (d < w - w // 2)).astype(np.float32))
        cnt_r = np.minimum(r + w - w // 2, rows) - np.maximum(r - w // 2, 0)
        cnt_c = np.minimum(c + w - w // 2, GRID_W) - np.maximum(c - w // 2, 0)
        invs.append(np.broadcast_to((1.0 / (cnt_r * cnt_c))[:, None], (n, LANES)).astype(np.float32))
    return np.stack(cms), np.stack(invs)


def _mix_out(xs, bc, zu, rows, cols, ds, cds, crows, x, mods, gn, dsk, gssm, wpool, pscale, wout, casts, tm):
    b, n, d = x.shape
    nchunks = n // CHUNK
    nj = n // tm
    cm_np, inv_np = _pool_constants(n)
    cm = jnp.asarray(cm_np, BF16)
    inv = jnp.asarray(inv_np, F32)
    full = lambda a: pl.BlockSpec(a.shape, lambda i, j: (0,) * a.ndim)
    per_batch = lambda a: pl.BlockSpec((1,) + a.shape[1:], lambda i, j: (i,) + (0,) * (a.ndim - 1))
    cast_specs = []
    for a in casts:
        rps = next(r for r in range(16, a.shape[0] + 1, 16) if a.shape[0] % r == 0 and a.shape[0] // r <= b * nj)
        last = a.shape[0] // rps - 1
        cast_specs.append(pl.BlockSpec((rps, a.shape[1]), lambda i, j, last=last: (jnp.minimum(i * nj + j, last), 0)))
    outs = pl.pallas_call(
        _mix_out_kernel,
        grid=(b, n // tm),
        in_specs=[pl.BlockSpec((1, tm, D_SSM), lambda i, j: (i, j, 0)),
                  pl.BlockSpec((1, tm, D_BC), lambda i, j: (i, j, 0)),
                  pl.BlockSpec((1, tm, D_SSM), lambda i, j: (i, j, 0)),
                  pl.BlockSpec((1, n, D_POOL), lambda i, j: (i, 0, 1)),
                  per_batch(rows), per_batch(cols), per_batch(ds), per_batch(cds), per_batch(crows),
                  pl.BlockSpec((1, tm, d), lambda i, j: (i, j, 0)),
                  _mod_spec(d, 2),
                  full(gn), full(dsk), full(gssm), full(wpool), full(pscale), full(wout), full(cm),
                  pl.BlockSpec(inv.shape, lambda i, j: (0, 0, 0), pipeline_mode=pl.Buffered(1))] + cast_specs,
        out_specs=[pl.BlockSpec((1, tm, d), lambda i, j: (i, j, 0))] + cast_specs,
        out_shape=[jax.ShapeDtypeStruct((b, n, d), F32)] + [jax.ShapeDtypeStruct(a.shape, BF16) for a in casts],
        scratch_shapes=[pltpu.VMEM((nchunks, SSD_PAIRS, SSD_STATE, LANES), F32),
                        pltpu.VMEM((nchunks, SSD_PAIRS, SSD_STATE, LANES), F32),
                        pltpu.VMEM((n, D_POOL), BF16),
                        pltpu.VMEM((tm, D_SSM + D_POOL), BF16)],
        compiler_params=pltpu.CompilerParams(dimension_semantics=("arbitrary", "arbitrary"),
                                             vmem_limit_bytes=VMEM_LIMIT),
        name="mix_out",
    )(xs, bc, zu, zu, rows, cols, ds, cds, crows, x, mods, gn, dsk, gssm, wpool, pscale, wout, cm, inv, *casts)
    return outs[0], outs[1:]


FFN_COLS = 2 * LANES
FFN_RING = 2


def _ffn_kernel(x_ref, xp_ref, xn_ref, shift_ref, scale_ref, gate_ref, gpre_ref, gpost_ref,
                wup_ref, cw_ref, cb_ref, wdown_ref, o_ref, act_scr, up_scr, af_scr):
    tm = x_ref.shape[1]
    dff = wdown_ref.shape[0]
    j = pl.program_id(1)
    nj = pl.num_programs(1)

    def hmod(v):
        return _rms(v, gpre_ref[...]) * (1.0 + scale_ref[0]) + shift_ref[0]

    x = x_ref[0]
    hp = jnp.where(j > 0, hmod(xp_ref[0]), 0.0)
    hn = jnp.where(j < nj - 1, hmod(xn_ref[0]), 0.0)
    hext = jnp.concatenate([hp, hmod(x), hn], axis=0).astype(BF16)

    nsl = FFN_COLS // LANES
    for cidx in range(dff // FFN_COLS):
        slot = cidx % FFN_RING
        conv = []
        for half, base in enumerate((cidx * FFN_COLS, dff + cidx * FFN_COLS)):
            up = jnp.dot(hext, wup_ref[:, base:base + FFN_COLS], preferred_element_type=F32)
            for s in range(nsl):
                lanes = slice(base + s * LANES, base + (s + 1) * LANES)
                slab = (2 * slot + half) * nsl + s
                up_scr[slab] = up[:, s * LANES:(s + 1) * LANES]
                conv.append([cb_ref[:, lanes] + sum(
                    cw_ref[k:k + 1, lanes] * up_scr[slab, pl.ds(HALO - 1 + par + k, tm // 2, stride=2), :]
                    for k in range(3)) for par in range(2)])
        for s in range(nsl):
            for par in range(2):
                af_scr[slot * nsl + s, pl.ds(par, tm // 2, stride=2), :] = (
                    _silu(conv[nsl + s][par]) * conv[s][par])
            lo = cidx * FFN_COLS + s * LANES
            act_scr[:, lo:lo + LANES] = af_scr[slot * nsl + s].astype(BF16)

    f = jnp.dot(act_scr[...], wdown_ref[...], preferred_element_type=F32)
    o_ref[0] = x + gate_ref[0] * _rms(f, gpost_ref[...])


def _ffn(x, mods, gpre, gpost, wup, cw, cb, wdown, tm):
    b, n, d = x.shape
    dff = wdown.shape[0]
    return pl.pallas_call(
        _ffn_kernel,
        grid=(b, n // tm),
        in_specs=[pl.BlockSpec((1, tm, d), lambda i, j: (i, j, 0))] + _halo_specs(tm, n, d) + [
            _mod_spec(d, 3), _mod_spec(d, 4), _mod_spec(d, 5),
            pl.BlockSpec((1, d), lambda i, j: (0, 0)),
            pl.BlockSpec((1, d), lambda i, j: (0, 0)),
            pl.BlockSpec(wup.shape, lambda i, j: (0, 0), pipeline_mode=pl.Buffered(1)),
            pl.BlockSpec(cw.shape, lambda i, j: (0, 0)),
            pl.BlockSpec(cb.shape, lambda i, j: (0, 0)),
            pl.BlockSpec(wdown.shape, lambda i, j: (0, 0), pipeline_mode=pl.Buffered(1))],
        out_specs=pl.BlockSpec((1, tm, d), lambda i, j: (i, j, 0)),
        out_shape=jax.ShapeDtypeStruct((b, n, d), F32),
        scratch_shapes=[pltpu.VMEM((tm, dff), BF16),
                        pltpu.VMEM((2 * FFN_RING * FFN_COLS // LANES, tm + 2 * HALO, LANES), F32),
                        pltpu.VMEM((FFN_RING * FFN_COLS // LANES, tm, LANES), F32)],
        compiler_params=pltpu.CompilerParams(dimension_semantics=("arbitrary", "arbitrary"),
                                             vmem_limit_bytes=VMEM_LIMIT),
        name="ffn",
    )(x, x, x, mods, mods, mods, gpre, gpost, wup, cw, cb, wdown)


def kernel(x, c, ctx, c_ctx, pre_norm_mix, post_norm_mix, pre_norm_ffn, post_norm_ffn, w_ada, b_ada,
           w_in, conv_ssd_w, conv_ssd_b, dt_bias, a_log, d_skip, ssm_norm, w_pool, pool_scale, w_out,
           w_up, conv_ffn_w, conv_ffn_b, w_down):
    assert w_ada.shape[0] == 1, "single-layer block"
    b, n, d = x.shape
    n_ctx = ctx.shape[1]
    d_xbc = D_SSM + D_BC
    d_xb = D_SSM + D_BC // 2
    off_dt = D_SSM + d_xbc
    off_pool = off_dt + 2 * SSD_HEADS
    tm_in = 1024

    ada_rows = 16
    cc = jnp.concatenate([c, c_ctx[None, :], jnp.zeros((ada_rows - b - 1, d), F32)], axis=0)
    mods = _ada(cc, w_ada[0], b_ada).reshape(ada_rows * N_MODS, 1, d)

    wi = w_in[0]
    w_all = wi.astype(BF16)
    w_pd = jnp.concatenate([wi[:, off_pool:], wi[:, off_dt:off_pool],
                            jnp.zeros((d, LANES - 2 * SSD_HEADS), F32)], axis=1).astype(BF16)
    bias = jnp.broadcast_to(dt_bias[0].reshape(2 * SSD_HEADS, 1), (2 * SSD_HEADS, tm_in))
    alog = jnp.broadcast_to(a_log[0].reshape(2 * SSD_HEADS, 1), (2 * SSD_HEADS, tm_in))

    xs, bc, zu, rows, cols, ds = _mix_in(x, mods, None, pre_norm_mix, w_all, w_pd, conv_ssd_w[0],
                                         conv_ssd_b, bias, alog, tm_in, True, "mix_in")
    crows, cds = _mix_in(ctx, mods, b, pre_norm_mix, w_all, w_pd, conv_ssd_w[0, :, :d_xb],
                         conv_ssd_b[:, :d_xb], bias, alog, tm_in, False, "mix_in_ctx")

    dsk = jnp.repeat(d_skip[0], SSD_HEAD_DIM)[None, :]
    x1, (w_up_b, w_down_b) = _mix_out(xs, bc, zu, rows, cols, ds, cds, crows, x, mods, post_norm_mix, dsk,
                                      ssm_norm, w_pool[0].astype(BF16), pool_scale, w_out[0].astype(BF16),
                                      [w_up[0], w_down[0]], 512)

    return _ffn(x1, mods, pre_norm_ffn, post_norm_ffn, w_up_b, conv_ffn_w[0], conv_ffn_b, w_down_b, 1024)
```

```python
import functools

import numpy as np
import jax
import jax.numpy as jnp
from jax import lax
from jax.experimental import pallas as pl
from jax.experimental.pallas import tpu as pltpu

F32 = jnp.float32
BF16 = jnp.bfloat16

EPS = 1e-6
GRID_W = 64
SSD_HEADS = 8
SSD_HEAD_DIM = 64
SSD_GROUPS = 2
SSD_STATE = 128
SSD_PAIRS = SSD_HEADS // 2
SSD_CONV = 5
D_SSM = SSD_HEADS * SSD_HEAD_DIM
D_BC = 2 * SSD_GROUPS * SSD_STATE
POOL_WINDOWS = (2, 4, 8, 16)
D_POOL = 512
CHUNK = 128
LANES = 128
HALO = 8
N_ROWS = 9
VMEM_LIMIT = 56 * 1024 * 1024


def _silu(v):
    return v / (1.0 + jnp.exp(-v))


def _rms(v, gain):
    ms = jnp.mean(v * v, axis=-1, keepdims=True)
    return v * lax.rsqrt(ms + EPS) * gain


N_MODS = 6
ADA_ROWS = 16


def _mod_spec(d, k, row=None):
    if row is None:
        return pl.BlockSpec((1, 1, d), lambda i, j: (k * ADA_ROWS + i, 0, 0))
    return pl.BlockSpec((1, 1, d), lambda i, j: (k * ADA_ROWS + row, 0, 0))


def _halo_specs(tm, n, d):
    hb, nb = tm // HALO, n // HALO
    return [pl.BlockSpec((1, HALO, d), lambda i, j: (i, jnp.maximum(j * hb - 1, 0), 0)),
            pl.BlockSpec((1, HALO, d), lambda i, j: (i, jnp.minimum((j + 1) * hb, nb - 1), 0))]


def _cast_blocks(a, steps):
    rps = next(r for r in range(16, a.shape[0] + 1, 16) if a.shape[0] % r == 0 and a.shape[0] // r <= steps)
    return rps, a.shape[0] // rps - 1


def _ada_kernel(c_ref, w_ref, b_ref, cast_ref, o_ref, cast_out_ref):
    s = _silu(c_ref[...]).astype(BF16)
    o_ref[:, 0, :] = jnp.dot(s, w_ref[...].astype(BF16), preferred_element_type=F32) + b_ref[...]
    cast_out_ref[...] = cast_ref[...].astype(BF16)


def _ada(cc, w_ada, b_ada, cast):
    rows, d = cc.shape
    n = w_ada.shape[1]
    rps, last = _cast_blocks(cast, n // d)
    cast_spec = pl.BlockSpec((rps, cast.shape[1]), lambda j: (jnp.minimum(j, last), 0))
    return pl.pallas_call(
        _ada_kernel,
        grid=(n // d,),
        in_specs=[pl.BlockSpec((rows, d), lambda j: (0, 0)),
                  pl.BlockSpec((d, d), lambda j: (0, j)),
                  pl.BlockSpec((1, d), lambda j: (0, j)),
                  cast_spec],
        out_specs=[pl.BlockSpec((rows, 1, d), lambda j: (j, 0, 0)), cast_spec],
        out_shape=[jax.ShapeDtypeStruct((n // d * rows, 1, d), F32), jax.ShapeDtypeStruct(cast.shape, BF16)],
        compiler_params=pltpu.CompilerParams(dimension_semantics=("arbitrary",),
                                             vmem_limit_bytes=VMEM_LIMIT),
        name="ada",
    )(cc, w_ada, b_ada, cast)


def _dt_rows(dtraw_chunks, bias, a):
    h = SSD_HEADS
    raw = jnp.concatenate([blk.T[0:2 * h, :] for blk in dtraw_chunks], axis=1) + bias
    t = raw.shape[1]
    dt = jnp.maximum(raw, 0.0) + jnp.log1p(jnp.exp(-jnp.abs(raw)))
    dta = dt * a
    seg = lax.broadcasted_iota(jnp.int32, dta.shape, 1) & (CHUNK - 1)
    cs, rcs = dta, dta
    k = 1
    while k < CHUNK:
        cs = cs + jnp.where(seg >= k, pltpu.roll(cs, k, 1), 0.0)
        rcs = rcs + jnp.where(seg < CHUNK - k, pltpu.roll(rcs, t - k, 1), 0.0)
        k *= 2
    cd = jnp.exp(cs + rcs - dta)
    wf = jnp.exp(rcs[0:h] - dta[0:h]) * dt[0:h]
    wb = jnp.exp(cs[h:] - dta[h:]) * dt[h:]
    log2e = 1.0 / np.log(2.0)
    pf2, ab2 = cs[0:h] * log2e, rcs[h:] * log2e
    ldt2 = jnp.log(dt) * log2e
    ldg2 = jnp.log(dt[0:h] + dt[h:]) * log2e
    return [pf2, ab2, pf2 - ldt2[0:h], ab2 - ldt2[h:], ldg2, wf, wb, cd[0:h], cd[h:]]


def _pair_rhs(xs, p):
    xp = xs[:, p * LANES:(p + 1) * LANES]
    lane = lax.broadcasted_iota(jnp.int32, xp.shape, 1)
    zero = jnp.zeros_like(xp)
    return jnp.concatenate([jnp.where(lane < SSD_HEAD_DIM, xp, zero),
                            jnp.where(lane >= SSD_HEAD_DIM, xp, zero)], axis=0)


def _pair_select(lane, v0, v1):
    return jnp.where(lane < SSD_HEAD_DIM, v0, v1)


def _local_states(xs, bts, wf, wb):
    out_f, out_b = [], []
    for p in range(SSD_PAIRS):
        bt = bts[p // 2]
        h0, h1 = 2 * p, 2 * p + 1
        lhs = jnp.concatenate(
            [jnp.concatenate([(bt * w[h0:h0 + 1, :]).astype(BF16), (bt * w[h1:h1 + 1, :]).astype(BF16)], axis=1)
             for w in (wf, wb)], axis=0)
        res = jnp.dot(lhs, _pair_rhs(xs, p), preferred_element_type=F32)
        out_f.append(res[0:SSD_STATE])
        out_b.append(res[SSD_STATE:2 * SSD_STATE])
    return out_f, out_b


def _mix_in_kernel(*refs, conv_cols, latent):
    (x_ref, xp_ref, xn_ref, shift_ref, scale_ref, g_ref, w_ref, wpd_ref, cw_ref, cb_ref,
     bias_ref, alog_ref) = refs[:12]
    if latent:
        xs_ref, bc_ref, zu_ref, rows_ref, cols_ref, ds_ref, up_scr, cv_scr = refs[12:]
    else:
        rows_ref, ds_ref, up_scr, cv_scr = refs[12:]
    nseg, seg, d = x_ref.shape
    tm = nseg * seg
    nch = tm // CHUNK
    cps = seg // CHUNK
    j = pl.program_id(1)
    nj = pl.num_programs(1)

    def hmod(v):
        return _rms(v, g_ref[...]) * (1.0 + scale_ref[0]) + shift_ref[0]

    xm = hmod(x_ref[...].reshape(tm, d))
    gap = jnp.zeros((HALO, d), F32)
    if latent:
        hp = jnp.where(j > 0, hmod(xp_ref[0]), 0.0)
        hn = jnp.where(j < nj - 1, hmod(xn_ref[0]), 0.0)
    else:
        hp = hn = gap
    hm = xm.astype(BF16)
    if latent:
        blocks = [(jnp.concatenate([hp, xm, hn], axis=0).astype(BF16), 0)]
        convs = [(HALO, 0, tm)]
    else:
        pieces = [gap]
        for i in range(nseg):
            pieces += [xm[i * seg:(i + 1) * seg], gap]
        ext_rows = up_scr.shape[1]
        if ext_rows > tm + (nseg + 1) * HALO:
            pieces.append(jnp.zeros((ext_rows - tm - (nseg + 1) * HALO, d), F32))
        blocks = [(jnp.concatenate(pieces, axis=0).astype(BF16), 0)]
        convs = [(HALO + i * (seg + HALO), i * seg, seg) for i in range(nseg)]

    half = SSD_CONV // 2
    cc = 2 * LANES
    for hblk, srow in blocks:
        for cb in range(conv_cols // cc):
            up = jnp.dot(hblk, w_ref[:, D_SSM + cb * cc:D_SSM + (cb + 1) * cc], preferred_element_type=F32)
            for s in range(cc // LANES):
                up_scr[cb * (cc // LANES) + s, srow:srow + hblk.shape[0], :] = up[:, s * LANES:(s + 1) * LANES]
    for slab in range(conv_cols // LANES):
        lanes = slice(slab * LANES, (slab + 1) * LANES)
        for drow, orow, nrows in convs:
            for par in range(2):
                lo = drow - half + par
                acc = cb_ref[:, lanes] + cw_ref[0:1, lanes] * up_scr[slab, pl.ds(lo, nrows // 2, stride=2), :]
                for k in range(1, SSD_CONV):
                    acc = acc + cw_ref[k:k + 1, lanes] * up_scr[slab, pl.ds(lo + k, nrows // 2, stride=2), :]
                cv_scr[slab, pl.ds(orow + par, nrows // 2, stride=2), :] = _silu(acc)
        if latent:
            if slab < D_SSM // LANES:
                xs_ref[0, :, lanes] = cv_scr[slab].astype(BF16)
            else:
                bc_ref[0, :, slab * LANES - D_SSM:(slab + 1) * LANES - D_SSM] = cv_scr[slab].astype(BF16)

    if latent:
        zu_ref[0, :, 0:D_SSM] = jnp.dot(hm, w_ref[:, 0:D_SSM], preferred_element_type=F32).astype(BF16)
        pd = jnp.dot(hm, wpd_ref[...], preferred_element_type=F32)
        zu_ref[0, :, D_SSM:D_SSM + D_POOL] = pd[:, 0:D_POOL].astype(BF16)
        dtraw = pd[:, D_POOL:D_POOL + LANES]
    else:
        dtraw = jnp.dot(hm, wpd_ref[:, D_POOL:D_POOL + LANES], preferred_element_type=F32)

    rows = _dt_rows([dtraw[c * CHUNK:(c + 1) * CHUNK] for c in range(nch)], bias_ref[...],
                    -jnp.exp(alog_ref[...]))
    pad = jnp.zeros((LANES - 2 * SSD_HEADS, CHUNK), F32)
    for c in range(nch):
        lanes = slice(c * CHUNK, (c + 1) * CHUNK)
        for q, arr in enumerate(rows):
            rows_ref[c // cps, c % cps, q] = arr[:, lanes]
        if latent:
            cols_ref[0, c] = jnp.concatenate([rows[0][:, lanes], rows[1][:, lanes], pad], axis=0).T

    def state_body(c, carry):
        r0 = pl.multiple_of(c * CHUNK, CHUNK)
        ci, cj = c // cps, c % cps
        xs = jnp.concatenate([cv_scr[s, pl.ds(r0, CHUNK), :] for s in range(D_SSM // LANES)],
                             axis=1).astype(BF16)
        bts = [cv_scr[D_SSM // LANES + g, pl.ds(r0, CHUNK), :].T for g in range(SSD_GROUPS)]
        dsf, dsb = _local_states(xs, bts, rows_ref[ci, cj, 5], rows_ref[ci, cj, 6])
        for p in range(SSD_PAIRS):
            ds_ref[ci, cj, 0, p] = dsf[p].astype(BF16)
            ds_ref[ci, cj, 1, p] = dsb[p].astype(BF16)
        return carry

    lax.fori_loop(0, nch, state_body, 0, unroll=4)


def _mix_in(x, mods, mod_row, gain, w, wpd, conv_w, conv_b, bias, alog, tm, latent, name):
    b, n, d = x.shape
    conv_cols = conv_w.shape[1]
    nch = tm // CHUNK
    nseg = 1 if latent else tm // n
    seg = tm // nseg
    cps = seg // CHUNK
    ext_rows = -(-(tm + (nseg + 1) * HALO) // 16) * 16
    full = lambda a: pl.BlockSpec(a.shape, lambda i, j: (0,) * a.ndim)
    stat = lambda tail, dt: jax.ShapeDtypeStruct((b, n // CHUNK) + tail, dt)
    out_specs, out_shape = [], []
    if latent:
        grid = (b, n // tm)
        x_spec = pl.BlockSpec((1, tm, d), lambda i, j: (i, j, 0))
        out_specs += [pl.BlockSpec((1, tm, D_SSM), lambda i, j: (i, j, 0)),
                      pl.BlockSpec((1, tm, conv_cols - D_SSM), lambda i, j: (i, j, 0)),
                      pl.BlockSpec((1, tm, D_SSM + D_POOL), lambda i, j: (i, j, 0))]
        out_shape += [jax.ShapeDtypeStruct((b, n, D_SSM), BF16),
                      jax.ShapeDtypeStruct((b, n, conv_cols - D_SSM), BF16),
                      jax.ShapeDtypeStruct((b, n, D_SSM + D_POOL), BF16)]
        blk = lambda tail: pl.BlockSpec((1, nch) + tail, lambda i, j: (i, j) + (0,) * len(tail))
    else:
        grid = (b // nseg, 1)
        x_spec = pl.BlockSpec((nseg, n, d), lambda i, j: (i, 0, 0))
        blk = lambda tail: pl.BlockSpec((nseg, cps) + tail, lambda i, j: (i, 0) + (0,) * len(tail))
    out_specs.append(blk((N_ROWS, SSD_HEADS, CHUNK)))
    out_shape.append(stat((N_ROWS, SSD_HEADS, CHUNK), F32))
    if latent:
        out_specs.append(blk((CHUNK, LANES)))
        out_shape.append(stat((CHUNK, LANES), F32))
    out_specs.append(blk((2, SSD_PAIRS, SSD_STATE, LANES)))
    out_shape.append(stat((2, SSD_PAIRS, SSD_STATE, LANES), BF16))
    return pl.pallas_call(
        functools.partial(_mix_in_kernel, conv_cols=conv_cols, latent=latent),
        grid=grid,
        in_specs=[x_spec] + _halo_specs(seg, n, d) + [
            _mod_spec(d, 0, mod_row), _mod_spec(d, 1, mod_row),
            full(gain), full(w), full(wpd), full(conv_w), full(conv_b), full(bias), full(alog)],
        out_specs=out_specs,
        out_shape=out_shape,
        scratch_shapes=[pltpu.VMEM((conv_cols // LANES, ext_rows, LANES), F32),
                        pltpu.VMEM((conv_cols // LANES, tm, LANES), F32)],
        compiler_params=pltpu.CompilerParams(dimension_semantics=("arbitrary", "arbitrary"),
                                             vmem_limit_bytes=VMEM_LIMIT),
        name=name,
    )(x, x, x, mods, mods, gain, w, wpd, conv_w, conv_b, bias, alog)


def _mix_out_kernel(xs_ref, bc_ref, z_ref, u_ref, rows_ref, cols_ref, ds_ref, cds_ref, crows_ref,
                    x_ref, gate_ref, gn_ref, dsk_ref, gssm_ref, wpool_ref, pscale_ref, wout_ref,
                    cm_ref, inv_ref, *rest):
    ncast = (len(rest) - 5) // 2
    o_ref = rest[ncast]
    sf_scr, sb_scr, d_scr, mix_scr = rest[2 * ncast + 1:]
    n = u_ref.shape[1]
    tm = x_ref.shape[1]
    nchunks = n // CHUNK
    nctx = cds_ref.shape[1]
    tch = tm // CHUNK
    j = pl.program_id(1)
    lane_row = lax.broadcasted_iota(jnp.int32, (1, LANES), 1)

    def cd_row(ref, c, q, p):
        cd = ref[0, c, q]
        return _pair_select(lane_row, cd[2 * p:2 * p + 1, :], cd[2 * p + 1:2 * p + 2, :])

    @pl.when(j == 0)
    def _():
        for p in range(SSD_PAIRS):
            sf = cds_ref[0, 0, 0, p].astype(F32)
            for c in range(1, nctx):
                sf = sf * cd_row(crows_ref, c, 7, p) + cds_ref[0, c, 0, p].astype(F32)
            sb = cds_ref[0, nctx - 1, 1, p].astype(F32)
            for c in reversed(range(nctx - 1)):
                sb = sb * cd_row(crows_ref, c, 8, p) + cds_ref[0, c, 1, p].astype(F32)
            sf_scr[0, p] = sf
            sb_scr[nchunks - 1, p] = sb

        def scan_body(i, carry):
            cf = i
            cbk = nchunks - 1 - i
            for p in range(SSD_PAIRS):
                sf_scr[cf + 1, p] = (sf_scr[cf, p] * cd_row(rows_ref, cf, 7, p)
                                     + ds_ref[0, cf, 0, p].astype(F32))
                sb_scr[cbk - 1, p] = (sb_scr[cbk, p] * cd_row(rows_ref, cbk, 8, p)
                                      + ds_ref[0, cbk, 1, p].astype(F32))
            return carry

        lax.fori_loop(0, nchunks - 1, scan_body, 0)

        for gi, w in enumerate(POOL_WINDOWS):
            cols = slice(gi * LANES, (gi + 1) * LANES)
            cmat = cm_ref[gi]
            parts = [jnp.dot(cmat, u_ref[0, i * LANES:(i + 1) * LANES, cols], preferred_element_type=F32)
                     for i in range(n // LANES)]
            cs = jnp.concatenate(parts, axis=0)
            h = jnp.concatenate([jnp.zeros((w // 2 * GRID_W, LANES), F32), cs], axis=0)
            k = 1
            while k < w:
                sh = k * GRID_W
                h = h + jnp.concatenate([h[sh:], jnp.zeros((sh, LANES), F32)], axis=0)
                k *= 2
            m = h[0:n] * inv_ref[gi]
            d_scr[:, cols] = (m - u_ref[0, :, cols].astype(F32)).astype(BF16)

    row_i = lax.broadcasted_iota(jnp.int32, (CHUNK, CHUNK), 0)
    col_i = lax.broadcasted_iota(jnp.int32, (CHUNK, CHUNK), 1)
    lane_full = lax.broadcasted_iota(jnp.int32, (CHUNK, LANES), 1)

    for lc in range(tch):
        c = j * tch + lc
        r0 = lc * CHUNK
        xs = xs_ref[0, pl.ds(r0, CHUNK), :]
        pfp, abp, ldg = rows_ref[0, c, 2], rows_ref[0, c, 3], rows_ref[0, c, 4]
        cols = cols_ref[0, c]

        y_pairs = []
        for g in range(SSD_GROUPS):
            bg = bc_ref[0, pl.ds(r0, CHUNK), g * SSD_STATE:(g + 1) * SSD_STATE]
            cg = bc_ref[0, pl.ds(r0, CHUNK), (SSD_GROUPS + g) * SSD_STATE:(SSD_GROUPS + g + 1) * SSD_STATE]
            gmat = lax.dot_general(cg, bg, (((1,), (1,)), ((), ())), preferred_element_type=F32)
            scat = jnp.concatenate([sf_scr[c, 2 * g].astype(BF16), sf_scr[c, 2 * g + 1].astype(BF16),
                                    sb_scr[c, 2 * g].astype(BF16), sb_scr[c, 2 * g + 1].astype(BF16)], axis=1)
            yoff = jnp.dot(cg, scat, preferred_element_type=F32)
            for q in range(2):
                p = 2 * g + q
                ms, efs, ebs = [], [], []
                for h in (2 * p, 2 * p + 1):
                    pf_c = jnp.broadcast_to(cols[:, h:h + 1], (CHUNK, CHUNK))
                    ab_c = jnp.broadcast_to(cols[:, SSD_HEADS + h:SSD_HEADS + h + 1], (CHUNK, CHUNK))
                    dm = jnp.where(row_i > col_i, pf_c - pfp[h:h + 1, :],
                                   jnp.where(row_i < col_i, ab_c - abp[h:h + 1, :], ldg[h:h + 1, :]))
                    ms.append((gmat * jnp.exp2(dm)).astype(BF16))
                    efs.append(jnp.exp2(pf_c))
                    ebs.append(jnp.exp2(ab_c))
                ydiag = jnp.dot(jnp.concatenate(ms, axis=1), _pair_rhs(xs, p), preferred_element_type=F32)
                y_pairs.append(ydiag
                               + _pair_select(lane_full, efs[0], efs[1]) * yoff[:, q * LANES:(q + 1) * LANES]
                               + _pair_select(lane_full, ebs[0], ebs[1]) * yoff[:, (2 + q) * LANES:(3 + q) * LANES])
        y = jnp.concatenate(y_pairs, axis=1) + dsk_ref[...] * xs.astype(F32)
        yz = y * _silu(z_ref[0, pl.ds(r0, CHUNK), :].astype(F32))
        yn = _rms(yz, gssm_ref[...]).astype(BF16)

        drows = pl.ds(pl.multiple_of(j * tm + r0, CHUNK), CHUNK)
        ps = [jnp.dot(d_scr[drows, gi * LANES:(gi + 1) * LANES], wpool_ref[gi], preferred_element_type=F32)
              for gi in range(len(POOL_WINDOWS))]
        pm = jnp.concatenate(ps, axis=1) * pscale_ref[...]
        mix_scr[pl.ds(r0, CHUNK), 0:D_SSM] = yn
        mix_scr[pl.ds(r0, CHUNK), D_SSM:D_SSM + D_POOL] = pm.astype(BF16)
        yx = jnp.dot(mix_scr[pl.ds(r0, CHUNK), :], wout_ref[...], preferred_element_type=F32)
        o_ref[0, pl.ds(r0, CHUNK), :] = x_ref[0, pl.ds(r0, CHUNK), :] + gate_ref[0] * _rms(yx, gn_ref[...])

    for src, dst in zip(rest[:ncast], rest[ncast + 1:2 * ncast + 1]):
        dst[...] = src[...].astype(BF16)


def _pool_constants(n):
    rows = n // GRID_W
    t = np.arange(LANES)
    blk, col = t // GRID_W, t % GRID_W
    cms, invs = [], []
    r = np.arange(n) // GRID_W
    c = np.arange(n) % GRID_W
    for w in POOL_WINDOWS:
        d = col[None, :] - col[:, None]
        cms.append(((blk[:, None] == blk[None, :]) & (d >= -(w // 2)) & (d < w - w // 2)).astype(np.float32))
        cnt_r = np.minimum(r + w - w // 2, rows) - np.maximum(r - w // 2, 0)
        cnt_c = np.minimum(c + w - w // 2, GRID_W) - np.maximum(c - w // 2, 0)
        invs.append(np.broadcast_to((1.0 / (cnt_r * cnt_c))[:, None], (n, LANES)).astype(np.float32))
    return np.stack(cms), np.stack(invs)


def _mix_out(xs, bc, zu, rows, cols, ds, cds, crows, x, mods, gn, dsk, gssm, wpool, pscale, wout, casts, tm):
    b, n, d = x.shape
    nchunks = n // CHUNK
    nj = n // tm
    cm_np, inv_np = _pool_constants(n)
    cm = jnp.asarray(cm_np, BF16)
    inv = jnp.asarray(inv_np, F32)
    full = lambda a: pl.BlockSpec(a.shape, lambda i, j: (0,) * a.ndim)
    per_batch = lambda a: pl.BlockSpec((1,) + a.shape[1:], lambda i, j: (i,) + (0,) * (a.ndim - 1))
    cast_specs = []
    for a in casts:
        rps, last = _cast_blocks(a, b * nj)
        cast_specs.append(pl.BlockSpec((rps, a.shape[1]), lambda i, j, last=last: (jnp.minimum(i * nj + j, last), 0)))
    outs = pl.pallas_call(
        _mix_out_kernel,
        grid=(b, n // tm),
        in_specs=[pl.BlockSpec((1, tm, D_SSM), lambda i, j: (i, j, 0)),
                  pl.BlockSpec((1, tm, D_BC), lambda i, j: (i, j, 0)),
                  pl.BlockSpec((1, tm, D_SSM), lambda i, j: (i, j, 0)),
                  pl.BlockSpec((1, n, D_POOL), lambda i, j: (i, 0, 1)),
                  per_batch(rows), per_batch(cols), per_batch(ds), per_batch(cds), per_batch(crows),
                  pl.BlockSpec((1, tm, d), lambda i, j: (i, j, 0)),
                  _mod_spec(d, 2),
                  full(gn), full(dsk), full(gssm), full(wpool), full(pscale), full(wout), full(cm),
                  pl.BlockSpec(inv.shape, lambda i, j: (0, 0, 0), pipeline_mode=pl.Buffered(1))] + cast_specs,
        out_specs=[pl.BlockSpec((1, tm, d), lambda i, j: (i, j, 0))] + cast_specs,
        out_shape=[jax.ShapeDtypeStruct((b, n, d), F32)] + [jax.ShapeDtypeStruct(a.shape, BF16) for a in casts],
        scratch_shapes=[pltpu.VMEM((nchunks, SSD_PAIRS, SSD_STATE, LANES), F32),
                        pltpu.VMEM((nchunks, SSD_PAIRS, SSD_STATE, LANES), F32),
                        pltpu.VMEM((n, D_POOL), BF16),
                        pltpu.VMEM((tm, D_SSM + D_POOL), BF16)],
        compiler_params=pltpu.CompilerParams(dimension_semantics=("arbitrary", "arbitrary"),
                                             vmem_limit_bytes=VMEM_LIMIT),
        name="mix_out",
    )(xs, bc, zu, zu, rows, cols, ds, cds, crows, x, mods, gn, dsk, gssm, wpool, pscale, wout, cm, inv, *casts)
    return outs[0], outs[1:]


FFN_COLS = 2 * LANES
FFN_RING = 2


def _ffn_kernel(x_ref, xp_ref, xn_ref, shift_ref, scale_ref, gate_ref, gpre_ref, gpost_ref,
                wup_ref, cw_ref, cb_ref, wdown_ref, o_ref, act_scr, up_scr, af_scr):
    tm = x_ref.shape[1]
    dff = wdown_ref.shape[0]
    j = pl.program_id(1)
    nj = pl.num_programs(1)

    def hmod(v):
        return _rms(v, gpre_ref[...]) * (1.0 + scale_ref[0]) + shift_ref[0]

    x = x_ref[0]
    hp = jnp.where(j > 0, hmod(xp_ref[0]), 0.0)
    hn = jnp.where(j < nj - 1, hmod(xn_ref[0]), 0.0)
    hext = jnp.concatenate([hp, hmod(x), hn], axis=0).astype(BF16)

    nsl = FFN_COLS // LANES
    for cidx in range(dff // FFN_COLS):
        slot = cidx % FFN_RING
        conv = []
        for half, base in enumerate((cidx * FFN_COLS, dff + cidx * FFN_COLS)):
            up = jnp.dot(hext, wup_ref[:, base:base + FFN_COLS], preferred_element_type=F32)
            for s in range(nsl):
                lanes = slice(base + s * LANES, base + (s + 1) * LANES)
                slab = (2 * slot + half) * nsl + s
                up_scr[slab] = up[:, s * LANES:(s + 1) * LANES]
                conv.append([cb_ref[:, lanes] + sum(
                    cw_ref[k:k + 1, lanes] * up_scr[slab, pl.ds(HALO - 1 + par + k, tm // 2, stride=2), :]
                    for k in range(3)) for par in range(2)])
        for s in range(nsl):
            for par in range(2):
                af_scr[slot * nsl + s, pl.ds(par, tm // 2, stride=2), :] = (
                    _silu(conv[nsl + s][par]) * conv[s][par])
            lo = cidx * FFN_COLS + s * LANES
            act_scr[:, lo:lo + LANES] = af_scr[slot * nsl + s].astype(BF16)

    f = jnp.dot(act_scr[...], wdown_ref[...], preferred_element_type=F32)
    o_ref[0] = x + gate_ref[0] * _rms(f, gpost_ref[...])


def _ffn(x, mods, gpre, gpost, wup, cw, cb, wdown, tm):
    b, n, d = x.shape
    dff = wdown.shape[0]
    return pl.pallas_call(
        _ffn_kernel,
        grid=(b, n // tm),
        in_specs=[pl.BlockSpec((1, tm, d), lambda i, j: (i, j, 0))] + _halo_specs(tm, n, d) + [
            _mod_spec(d, 3), _mod_spec(d, 4), _mod_spec(d, 5),
            pl.BlockSpec((1, d), lambda i, j: (0, 0)),
            pl.BlockSpec((1, d), lambda i, j: (0, 0)),
            pl.BlockSpec(wup.shape, lambda i, j: (0, 0), pipeline_mode=pl.Buffered(1)),
            pl.BlockSpec(cw.shape, lambda i, j: (0, 0)),
            pl.BlockSpec(cb.shape, lambda i, j: (0, 0)),
            pl.BlockSpec(wdown.shape, lambda i, j: (0, 0), pipeline_mode=pl.Buffered(1))],
        out_specs=pl.BlockSpec((1, tm, d), lambda i, j: (i, j, 0)),
        out_shape=jax.ShapeDtypeStruct((b, n, d), F32),
        scratch_shapes=[pltpu.VMEM((tm, dff), BF16),
                        pltpu.VMEM((2 * FFN_RING * FFN_COLS // LANES, tm + 2 * HALO, LANES), F32),
                        pltpu.VMEM((FFN_RING * FFN_COLS // LANES, tm, LANES), F32)],
        compiler_params=pltpu.CompilerParams(dimension_semantics=("arbitrary", "arbitrary"),
                                             vmem_limit_bytes=VMEM_LIMIT),
        name="ffn",
    )(x, x, x, mods, mods, mods, gpre, gpost, wup, cw, cb, wdown)


def kernel(x, c, ctx, c_ctx, pre_norm_mix, post_norm_mix, pre_norm_ffn, post_norm_ffn, w_ada, b_ada,
           w_in, conv_ssd_w, conv_ssd_b, dt_bias, a_log, d_skip, ssm_norm, w_pool, pool_scale, w_out,
           w_up, conv_ffn_w, conv_ffn_b, w_down):
    assert w_ada.shape[0] == 1, "single-layer block"
    b, n, d = x.shape
    n_ctx = ctx.shape[1]
    d_xbc = D_SSM + D_BC
    d_xb = D_SSM + D_BC // 2
    off_dt = D_SSM + d_xbc
    off_pool = off_dt + 2 * SSD_HEADS
    tm_in = 1024

    assert b < ADA_ROWS and w_ada.shape[2] == N_MODS * d
    cc = jnp.concatenate([c, c_ctx[None, :], jnp.zeros((ADA_ROWS - b - 1, d), F32)], axis=0)
    wi = w_in[0]
    mods, w_all = _ada(cc, w_ada[0], b_ada, wi)
    w_pd = jnp.concatenate([wi[:, off_pool:], wi[:, off_dt:off_pool],
                            jnp.zeros((d, LANES - 2 * SSD_HEADS), F32)], axis=1).astype(BF16)
    bias = jnp.broadcast_to(dt_bias[0].reshape(2 * SSD_HEADS, 1), (2 * SSD_HEADS, tm_in))
    alog = jnp.broadcast_to(a_log[0].reshape(2 * SSD_HEADS, 1), (2 * SSD_HEADS, tm_in))

    xs, bc, zu, rows, cols, ds = _mix_in(x, mods, None, pre_norm_mix, w_all, w_pd, conv_ssd_w[0],
                                         conv_ssd_b, bias, alog, tm_in, True, "mix_in")
    crows, cds = _mix_in(ctx, mods, b, pre_norm_mix, w_all, w_pd, conv_ssd_w[0, :, :d_xb],
                         conv_ssd_b[:, :d_xb], bias, alog, tm_in, False, "mix_in_ctx")

    dsk = jnp.repeat(d_skip[0], SSD_HEAD_DIM)[None, :]
    x1, (w_up_b, w_down_b) = _mix_out(xs, bc, zu, rows, cols, ds, cds, crows, x, mods, post_norm_mix, dsk,
                                      ssm_norm, w_pool[0].astype(BF16), pool_scale, w_out[0].astype(BF16),
                                      [w_up[0], w_down[0]], 512)

    return _ffn(x1, mods, pre_norm_ffn, post_norm_ffn, w_up_b, conv_ffn_w[0], conv_ffn_b, w_down_b, 1024)
```

```python
import functools

import numpy as np
import jax
import jax.numpy as jnp
from jax import lax
from jax.experimental import pallas as pl
from jax.experimental.pallas import tpu as pltpu

F32 = jnp.float32
BF16 = jnp.bfloat16

EPS = 1e-6
GRID_W = 64
SSD_HEADS = 8
SSD_HEAD_DIM = 64
SSD_GROUPS = 2
SSD_STATE = 128
SSD_PAIRS = SSD_HEADS // 2
SSD_CONV = 5
D_SSM = SSD_HEADS * SSD_HEAD_DIM
D_BC = 2 * SSD_GROUPS * SSD_STATE
POOL_WINDOWS = (2, 4, 8, 16)
D_POOL = 512
CHUNK = 128
LANES = 128
HALO = 8
N_ROWS = 9
VMEM_LIMIT = 56 * 1024 * 1024


def _silu(v):
    return v / (1.0 + jnp.exp(-v))


def _rms(v, gain):
    ms = jnp.mean(v * v, axis=-1, keepdims=True)
    return v * lax.rsqrt(ms + EPS) * gain


N_MODS = 6
ADA_ROWS = 16


def _mod_spec(d, k, row=None):
    if row is None:
        return pl.BlockSpec((1, 1, d), lambda i, j: (k * ADA_ROWS + i, 0, 0))
    return pl.BlockSpec((1, 1, d), lambda i, j: (k * ADA_ROWS + row, 0, 0))


def _halo_specs(tm, n, d):
    hb, nb = tm // HALO, n // HALO
    return [pl.BlockSpec((1, HALO, d), lambda i, j: (i, jnp.maximum(j * hb - 1, 0), 0)),
            pl.BlockSpec((1, HALO, d), lambda i, j: (i, jnp.minimum((j + 1) * hb, nb - 1), 0))]


def _cast_blocks(a, steps):
    rps = next(r for r in range(16, a.shape[0] + 1, 16) if a.shape[0] % r == 0 and a.shape[0] // r <= steps)
    return rps, a.shape[0] // rps - 1


def _ada_kernel(c_ref, w_ref, b_ref, o_ref):
    s = _silu(c_ref[...]).astype(BF16)
    o_ref[:, 0, :] = jnp.dot(s, w_ref[...].astype(BF16), preferred_element_type=F32) + b_ref[...]


def _ada(cc, w_ada, b_ada):
    rows, d = cc.shape
    n = w_ada.shape[1]
    return pl.pallas_call(
        _ada_kernel,
        grid=(n // d,),
        in_specs=[pl.BlockSpec((rows, d), lambda j: (0, 0)),
                  pl.BlockSpec((d, d), lambda j: (0, j)),
                  pl.BlockSpec((1, d), lambda j: (0, j))],
        out_specs=pl.BlockSpec((rows, 1, d), lambda j: (j, 0, 0)),
        out_shape=jax.ShapeDtypeStruct((n // d * rows, 1, d), F32),
        compiler_params=pltpu.CompilerParams(dimension_semantics=("arbitrary",),
                                             vmem_limit_bytes=VMEM_LIMIT),
        name="ada",
    )(cc, w_ada, b_ada)


def _dt_rows(dtraw_chunks, bias, a):
    h = SSD_HEADS
    raw = jnp.concatenate([blk.T[0:2 * h, :] for blk in dtraw_chunks], axis=1) + bias
    t = raw.shape[1]
    dt = jnp.maximum(raw, 0.0) + jnp.log1p(jnp.exp(-jnp.abs(raw)))
    dta = dt * a
    seg = lax.broadcasted_iota(jnp.int32, dta.shape, 1) & (CHUNK - 1)
    cs, rcs = dta, dta
    k = 1
    while k < CHUNK:
        cs = cs + jnp.where(seg >= k, pltpu.roll(cs, k, 1), 0.0)
        rcs = rcs + jnp.where(seg < CHUNK - k, pltpu.roll(rcs, t - k, 1), 0.0)
        k *= 2
    cd = jnp.exp(cs + rcs - dta)
    wf = jnp.exp(rcs[0:h] - dta[0:h]) * dt[0:h]
    wb = jnp.exp(cs[h:] - dta[h:]) * dt[h:]
    log2e = 1.0 / np.log(2.0)
    pf2, ab2 = cs[0:h] * log2e, rcs[h:] * log2e
    ldt2 = jnp.log(dt) * log2e
    ldg2 = jnp.log(dt[0:h] + dt[h:]) * log2e
    return [pf2, ab2, pf2 - ldt2[0:h], ab2 - ldt2[h:], ldg2, wf, wb, cd[0:h], cd[h:]]


def _pair_rhs(xs, p):
    xp = xs[:, p * LANES:(p + 1) * LANES]
    lane = lax.broadcasted_iota(jnp.int32, xp.shape, 1)
    zero = jnp.zeros_like(xp)
    return jnp.concatenate([jnp.where(lane < SSD_HEAD_DIM, xp, zero),
                            jnp.where(lane >= SSD_HEAD_DIM, xp, zero)], axis=0)


def _pair_select(lane, v0, v1):
    return jnp.where(lane < SSD_HEAD_DIM, v0, v1)


def _local_states(xs, bts, wf, wb):
    out_f, out_b = [], []
    for p in range(SSD_PAIRS):
        bt = bts[p // 2]
        h0, h1 = 2 * p, 2 * p + 1
        lhs = jnp.concatenate(
            [jnp.concatenate([(bt * w[h0:h0 + 1, :]).astype(BF16), (bt * w[h1:h1 + 1, :]).astype(BF16)], axis=1)
             for w in (wf, wb)], axis=0)
        res = jnp.dot(lhs, _pair_rhs(xs, p), preferred_element_type=F32)
        out_f.append(res[0:SSD_STATE])
        out_b.append(res[SSD_STATE:2 * SSD_STATE])
    return out_f, out_b


def _mix_in_kernel(*refs, conv_cols, latent):
    (x_ref, xp_ref, xn_ref, shift_ref, scale_ref, g_ref, w_ref, wpd_ref, cw_ref, cb_ref,
     bias_ref, alog_ref) = refs[:12]
    if latent:
        xs_ref, bc_ref, zu_ref, rows_ref, cols_ref, ds_ref, up_scr, cv_scr = refs[12:]
    else:
        rows_ref, ds_ref, up_scr, cv_scr = refs[12:]
    nseg, seg, d = x_ref.shape
    tm = nseg * seg
    nch = tm // CHUNK
    cps = seg // CHUNK
    j = pl.program_id(1)
    nj = pl.num_programs(1)

    def hmod(v):
        return _rms(v, g_ref[...]) * (1.0 + scale_ref[0]) + shift_ref[0]

    xm = hmod(x_ref[...].reshape(tm, d))
    gap = jnp.zeros((HALO, d), F32)
    if latent:
        hp = jnp.where(j > 0, hmod(xp_ref[0]), 0.0)
        hn = jnp.where(j < nj - 1, hmod(xn_ref[0]), 0.0)
    else:
        hp = hn = gap
    hm = xm.astype(BF16)
    if latent:
        blocks = [(jnp.concatenate([hp, xm, hn], axis=0).astype(BF16), 0)]
        convs = [(HALO, 0, tm)]
    else:
        pieces = [gap]
        for i in range(nseg):
            pieces += [xm[i * seg:(i + 1) * seg], gap]
        ext_rows = up_scr.shape[1]
        if ext_rows > tm + (nseg + 1) * HALO:
            pieces.append(jnp.zeros((ext_rows - tm - (nseg + 1) * HALO, d), F32))
        blocks = [(jnp.concatenate(pieces, axis=0).astype(BF16), 0)]
        convs = [(HALO + i * (seg + HALO), i * seg, seg) for i in range(nseg)]

    half = SSD_CONV // 2
    cc = 2 * LANES
    for hblk, srow in blocks:
        for cb in range(conv_cols // cc):
            up = jnp.dot(hblk, w_ref[:, D_SSM + cb * cc:D_SSM + (cb + 1) * cc], preferred_element_type=F32)
            for s in range(cc // LANES):
                up_scr[cb * (cc // LANES) + s, srow:srow + hblk.shape[0], :] = up[:, s * LANES:(s + 1) * LANES]
    for slab in range(conv_cols // LANES):
        lanes = slice(slab * LANES, (slab + 1) * LANES)
        for drow, orow, nrows in convs:
            for par in range(2):
                lo = drow - half + par
                acc = cb_ref[:, lanes] + cw_ref[0:1, lanes] * up_scr[slab, pl.ds(lo, nrows // 2, stride=2), :]
                for k in range(1, SSD_CONV):
                    acc = acc + cw_ref[k:k + 1, lanes] * up_scr[slab, pl.ds(lo + k, nrows // 2, stride=2), :]
                cv_scr[slab, pl.ds(orow + par, nrows // 2, stride=2), :] = _silu(acc)
        if latent:
            if slab < D_SSM // LANES:
                xs_ref[0, :, lanes] = cv_scr[slab].astype(BF16)
            else:
                bc_ref[0, :, slab * LANES - D_SSM:(slab + 1) * LANES - D_SSM] = cv_scr[slab].astype(BF16)

    if latent:
        zu_ref[0, :, 0:D_SSM] = jnp.dot(hm, w_ref[:, 0:D_SSM], preferred_element_type=F32).astype(BF16)
        pd = jnp.dot(hm, wpd_ref[...], preferred_element_type=F32)
        zu_ref[0, :, D_SSM:D_SSM + D_POOL] = pd[:, 0:D_POOL].astype(BF16)
        dtraw = pd[:, D_POOL:D_POOL + LANES]
    else:
        dtraw = jnp.dot(hm, wpd_ref[:, D_POOL:D_POOL + LANES], preferred_element_type=F32)

    rows = _dt_rows([dtraw[c * CHUNK:(c + 1) * CHUNK] for c in range(nch)], bias_ref[...],
                    -jnp.exp(alog_ref[...]))
    pad = jnp.zeros((LANES - 2 * SSD_HEADS, CHUNK), F32)
    for c in range(nch):
        lanes = slice(c * CHUNK, (c + 1) * CHUNK)
        for q, arr in enumerate(rows):
            rows_ref[c // cps, c % cps, q] = arr[:, lanes]
        if latent:
            cols_ref[0, c] = jnp.concatenate([rows[0][:, lanes], rows[1][:, lanes], pad], axis=0).T

    def state_body(c, carry):
        r0 = pl.multiple_of(c * CHUNK, CHUNK)
        ci, cj = c // cps, c % cps
        xs = jnp.concatenate([cv_scr[s, pl.ds(r0, CHUNK), :] for s in range(D_SSM // LANES)],
                             axis=1).astype(BF16)
        bts = [cv_scr[D_SSM // LANES + g, pl.ds(r0, CHUNK), :].T for g in range(SSD_GROUPS)]
        dsf, dsb = _local_states(xs, bts, rows_ref[ci, cj, 5], rows_ref[ci, cj, 6])
        for p in range(SSD_PAIRS):
            ds_ref[ci, cj, 0, p] = dsf[p].astype(BF16)
            ds_ref[ci, cj, 1, p] = dsb[p].astype(BF16)
        return carry

    lax.fori_loop(0, nch, state_body, 0, unroll=4)


def _mix_in(x, mods, mod_row, gain, w, wpd, conv_w, conv_b, bias, alog, tm, latent, name):
    b, n, d = x.shape
    conv_cols = conv_w.shape[1]
    nch = tm // CHUNK
    nseg = 1 if latent else tm // n
    seg = tm // nseg
    cps = seg // CHUNK
    ext_rows = -(-(tm + (nseg + 1) * HALO) // 16) * 16
    full = lambda a: pl.BlockSpec(a.shape, lambda i, j: (0,) * a.ndim)
    stat = lambda tail, dt: jax.ShapeDtypeStruct((b, n // CHUNK) + tail, dt)
    out_specs, out_shape = [], []
    if latent:
        grid = (b, n // tm)
        x_spec = pl.BlockSpec((1, tm, d), lambda i, j: (i, j, 0))
        out_specs += [pl.BlockSpec((1, tm, D_SSM), lambda i, j: (i, j, 0)),
                      pl.BlockSpec((1, tm, conv_cols - D_SSM), lambda i, j: (i, j, 0)),
                      pl.BlockSpec((1, tm, D_SSM + D_POOL), lambda i, j: (i, j, 0))]
        out_shape += [jax.ShapeDtypeStruct((b, n, D_SSM), BF16),
                      jax.ShapeDtypeStruct((b, n, conv_cols - D_SSM), BF16),
                      jax.ShapeDtypeStruct((b, n, D_SSM + D_POOL), BF16)]
        blk = lambda tail: pl.BlockSpec((1, nch) + tail, lambda i, j: (i, j) + (0,) * len(tail))
    else:
        grid = (b // nseg, 1)
        x_spec = pl.BlockSpec((nseg, n, d), lambda i, j: (i, 0, 0))
        blk = lambda tail: pl.BlockSpec((nseg, cps) + tail, lambda i, j: (i, 0) + (0,) * len(tail))
    out_specs.append(blk((N_ROWS, SSD_HEADS, CHUNK)))
    out_shape.append(stat((N_ROWS, SSD_HEADS, CHUNK), F32))
    if latent:
        out_specs.append(blk((CHUNK, LANES)))
        out_shape.append(stat((CHUNK, LANES), F32))
    out_specs.append(blk((2, SSD_PAIRS, SSD_STATE, LANES)))
    out_shape.append(stat((2, SSD_PAIRS, SSD_STATE, LANES), BF16))
    return pl.pallas_call(
        functools.partial(_mix_in_kernel, conv_cols=conv_cols, latent=latent),
        grid=grid,
        in_specs=[x_spec] + _halo_specs(seg, n, d) + [
            _mod_spec(d, 0, mod_row), _mod_spec(d, 1, mod_row),
            full(gain), full(w), full(wpd), full(conv_w), full(conv_b), full(bias), full(alog)],
        out_specs=out_specs,
        out_shape=out_shape,
        scratch_shapes=[pltpu.VMEM((conv_cols // LANES, ext_rows, LANES), F32),
                        pltpu.VMEM((conv_cols // LANES, tm, LANES), F32)],
        compiler_params=pltpu.CompilerParams(dimension_semantics=("arbitrary", "arbitrary"),
                                             vmem_limit_bytes=VMEM_LIMIT),
        name=name,
    )(x, x, x, mods, mods, gain, w, wpd, conv_w, conv_b, bias, alog)


def _mix_out_kernel(xs_ref, bc_ref, z_ref, u_ref, rows_ref, cols_ref, ds_ref, cds_ref, crows_ref,
                    x_ref, gate_ref, gn_ref, dsk_ref, gssm_ref, wpool_ref, pscale_ref, wout_ref,
                    cm_ref, inv_ref, *rest):
    ncast = (len(rest) - 5) // 2
    o_ref = rest[ncast]
    sf_scr, sb_scr, d_scr, mix_scr = rest[2 * ncast + 1:]
    n = u_ref.shape[1]
    tm = x_ref.shape[1]
    nchunks = n // CHUNK
    nctx = cds_ref.shape[1]
    tch = tm // CHUNK
    j = pl.program_id(1)
    lane_row = lax.broadcasted_iota(jnp.int32, (1, LANES), 1)

    def cd_row(ref, c, q, p):
        cd = ref[0, c, q]
        return _pair_select(lane_row, cd[2 * p:2 * p + 1, :], cd[2 * p + 1:2 * p + 2, :])

    @pl.when(j == 0)
    def _():
        for p in range(SSD_PAIRS):
            sf = cds_ref[0, 0, 0, p].astype(F32)
            for c in range(1, nctx):
                sf = sf * cd_row(crows_ref, c, 7, p) + cds_ref[0, c, 0, p].astype(F32)
            sb = cds_ref[0, nctx - 1, 1, p].astype(F32)
            for c in reversed(range(nctx - 1)):
                sb = sb * cd_row(crows_ref, c, 8, p) + cds_ref[0, c, 1, p].astype(F32)
            sf_scr[0, p] = sf
            sb_scr[nchunks - 1, p] = sb

        def scan_body(i, carry):
            cf = i
            cbk = nchunks - 1 - i
            for p in range(SSD_PAIRS):
                sf_scr[cf + 1, p] = (sf_scr[cf, p] * cd_row(rows_ref, cf, 7, p)
                                     + ds_ref[0, cf, 0, p].astype(F32))
                sb_scr[cbk - 1, p] = (sb_scr[cbk, p] * cd_row(rows_ref, cbk, 8, p)
                                      + ds_ref[0, cbk, 1, p].astype(F32))
            return carry

        lax.fori_loop(0, nchunks - 1, scan_body, 0)

        for gi, w in enumerate(POOL_WINDOWS):
            cols = slice(gi * LANES, (gi + 1) * LANES)
            cmat = cm_ref[gi]
            parts = [jnp.dot(cmat, u_ref[0, i * LANES:(i + 1) * LANES, cols], preferred_element_type=F32)
                     for i in range(n // LANES)]
            cs = jnp.concatenate(parts, axis=0)
            h = jnp.concatenate([jnp.zeros((w // 2 * GRID_W, LANES), F32), cs], axis=0)
            k = 1
            while k < w:
                sh = k * GRID_W
                h = h + jnp.concatenate([h[sh:], jnp.zeros((sh, LANES), F32)], axis=0)
                k *= 2
            m = h[0:n] * inv_ref[gi]
            d_scr[:, cols] = (m - u_ref[0, :, cols].astype(F32)).astype(BF16)

    row_i = lax.broadcasted_iota(jnp.int32, (CHUNK, CHUNK), 0)
    col_i = lax.broadcasted_iota(jnp.int32, (CHUNK, CHUNK), 1)
    lane_full = lax.broadcasted_iota(jnp.int32, (CHUNK, LANES), 1)

    for lc in range(tch):
        c = j * tch + lc
        r0 = lc * CHUNK
        xs = xs_ref[0, pl.ds(r0, CHUNK), :]
        pfp, abp, ldg = rows_ref[0, c, 2], rows_ref[0, c, 3], rows_ref[0, c, 4]
        cols = cols_ref[0, c]

        y_pairs = []
        for g in range(SSD_GROUPS):
            bg = bc_ref[0, pl.ds(r0, CHUNK), g * SSD_STATE:(g + 1) * SSD_STATE]
            cg = bc_ref[0, pl.ds(r0, CHUNK), (SSD_GROUPS + g) * SSD_STATE:(SSD_GROUPS + g + 1) * SSD_STATE]
            gmat = lax.dot_general(cg, bg, (((1,), (1,)), ((), ())), preferred_element_type=F32)
            scat = jnp.concatenate([sf_scr[c, 2 * g].astype(BF16), sf_scr[c, 2 * g + 1].astype(BF16),
                                    sb_scr[c, 2 * g].astype(BF16), sb_scr[c, 2 * g + 1].astype(BF16)], axis=1)
            yoff = jnp.dot(cg, scat, preferred_element_type=F32)
            for q in range(2):
                p = 2 * g + q
                ms, efs, ebs = [], [], []
                for h in (2 * p, 2 * p + 1):
                    pf_c = jnp.broadcast_to(cols[:, h:h + 1], (CHUNK, CHUNK))
                    ab_c = jnp.broadcast_to(cols[:, SSD_HEADS + h:SSD_HEADS + h + 1], (CHUNK, CHUNK))
                    dm = jnp.where(row_i > col_i, pf_c - pfp[h:h + 1, :],
                                   jnp.where(row_i < col_i, ab_c - abp[h:h + 1, :], ldg[h:h + 1, :]))
                    ms.append((gmat * jnp.exp2(dm)).astype(BF16))
                    efs.append(jnp.exp2(pf_c))
                    ebs.append(jnp.exp2(ab_c))
                ydiag = jnp.dot(jnp.concatenate(ms, axis=1), _pair_rhs(xs, p), preferred_element_type=F32)
                y_pairs.append(ydiag
                               + _pair_select(lane_full, efs[0], efs[1]) * yoff[:, q * LANES:(q + 1) * LANES]
                               + _pair_select(lane_full, ebs[0], ebs[1]) * yoff[:, (2 + q) * LANES:(3 + q) * LANES])
        y = jnp.concatenate(y_pairs, axis=1) + dsk_ref[...] * xs.astype(F32)
        yz = y * _silu(z_ref[0, pl.ds(r0, CHUNK), :].astype(F32))
        yn = _rms(yz, gssm_ref[...]).astype(BF16)

        drows = pl.ds(pl.multiple_of(j * tm + r0, CHUNK), CHUNK)
        ps = [jnp.dot(d_scr[drows, gi * LANES:(gi + 1) * LANES], wpool_ref[gi], preferred_element_type=F32)
              for gi in range(len(POOL_WINDOWS))]
        pm = jnp.concatenate(ps, axis=1) * pscale_ref[...]
        mix_scr[pl.ds(r0, CHUNK), 0:D_SSM] = yn
        mix_scr[pl.ds(r0, CHUNK), D_SSM:D_SSM + D_POOL] = pm.astype(BF16)
        yx = jnp.dot(mix_scr[pl.ds(r0, CHUNK), :], wout_ref[...], preferred_element_type=F32)
        o_ref[0, pl.ds(r0, CHUNK), :] = x_ref[0, pl.ds(r0, CHUNK), :] + gate_ref[0] * _rms(yx, gn_ref[...])

    for src, dst in zip(rest[:ncast], rest[ncast + 1:2 * ncast + 1]):
        dst[...] = src[...].astype(BF16)


def _pool_constants(n):
    rows = n // GRID_W
    t = np.arange(LANES)
    blk, col = t // GRID_W, t % GRID_W
    cms, invs = [], []
    r = np.arange(n) // GRID_W
    c = np.arange(n) % GRID_W
    for w in POOL_WINDOWS:
        d = col[None, :] - col[:, None]
        cms.append(((blk[:, None] == blk[None, :]) & (d >= -(w // 2)) & (d < w - w // 2)).astype(np.float32))
        cnt_r = np.minimum(r + w - w // 2, rows) - np.maximum(r - w // 2, 0)
        cnt_c = np.minimum(c + w - w // 2, GRID_W) - np.maximum(c - w // 2, 0)
        invs.append(np.broadcast_to((1.0 / (cnt_r * cnt_c))[:, None], (n, LANES)).astype(np.float32))
    return np.stack(cms), np.stack(invs)


def _mix_out(xs, bc, zu, rows, cols, ds, cds, crows, x, mods, gn, dsk, gssm, wpool, pscale, wout, casts, tm):
    b, n, d = x.shape
    nchunks = n // CHUNK
    nj = n // tm
    cm_np, inv_np = _pool_constants(n)
    cm = jnp.asarray(cm_np, BF16)
    inv = jnp.asarray(inv_np, F32)
    full = lambda a: pl.BlockSpec(a.shape, lambda i, j: (0,) * a.ndim)
    per_batch = lambda a: pl.BlockSpec((1,) + a.shape[1:], lambda i, j: (i,) + (0,) * (a.ndim - 1))
    cast_specs = []
    for a in casts:
        rps, last = _cast_blocks(a, b * nj)
        cast_specs.append(pl.BlockSpec((rps, a.shape[1]), lambda i, j, last=last: (jnp.minimum(i * nj + j, last), 0)))
    outs = pl.pallas_call(
        _mix_out_kernel,
        grid=(b, n // tm),
        in_specs=[pl.BlockSpec((1, tm, D_SSM), lambda i, j: (i, j, 0)),
                  pl.BlockSpec((1, tm, D_BC), lambda i, j: (i, j, 0)),
                  pl.BlockSpec((1, tm, D_SSM), lambda i, j: (i, j, 0)),
                  pl.BlockSpec((1, n, D_POOL), lambda i, j: (i, 0, 1)),
                  per_batch(rows), per_batch(cols), per_batch(ds), per_batch(cds), per_batch(crows),
                  pl.BlockSpec((1, tm, d), lambda i, j: (i, j, 0)),
                  _mod_spec(d, 2),
                  full(gn), full(dsk), full(gssm), full(wpool), full(pscale), full(wout), full(cm),
                  pl.BlockSpec(inv.shape, lambda i, j: (0, 0, 0), pipeline_mode=pl.Buffered(1))] + cast_specs,
        out_specs=[pl.BlockSpec((1, tm, d), lambda i, j: (i, j, 0))] + cast_specs,
        out_shape=[jax.ShapeDtypeStruct((b, n, d), F32)] + [jax.ShapeDtypeStruct(a.shape, BF16) for a in casts],
        scratch_shapes=[pltpu.VMEM((nchunks, SSD_PAIRS, SSD_STATE, LANES), F32),
                        pltpu.VMEM((nchunks, SSD_PAIRS, SSD_STATE, LANES), F32),
                        pltpu.VMEM((n, D_POOL), BF16),
                        pltpu.VMEM((tm, D_SSM + D_POOL), BF16)],
        compiler_params=pltpu.CompilerParams(dimension_semantics=("arbitrary", "arbitrary"),
                                             vmem_limit_bytes=VMEM_LIMIT),
        name="mix_out",
    )(xs, bc, zu, zu, rows, cols, ds, cds, crows, x, mods, gn, dsk, gssm, wpool, pscale, wout, cm, inv, *casts)
    return outs[0], outs[1:]


FFN_COLS = 2 * LANES
FFN_RING = 2


def _ffn_kernel(x_ref, xp_ref, xn_ref, shift_ref, scale_ref, gate_ref, gpre_ref, gpost_ref,
                wup_ref, cw_ref, cb_ref, wdown_ref, o_ref, act_scr, up_scr, af_scr):
    tm = x_ref.shape[1]
    dff = wdown_ref.shape[0]
    j = pl.program_id(1)
    nj = pl.num_programs(1)

    def hmod(v):
        return _rms(v, gpre_ref[...]) * (1.0 + scale_ref[0]) + shift_ref[0]

    x = x_ref[0]
    hp = jnp.where(j > 0, hmod(xp_ref[0]), 0.0)
    hn = jnp.where(j < nj - 1, hmod(xn_ref[0]), 0.0)
    hext = jnp.concatenate([hp, hmod(x), hn], axis=0).astype(BF16)

    nsl = FFN_COLS // LANES
    for cidx in range(dff // FFN_COLS):
        slot = cidx % FFN_RING
        conv = []
        for half, base in enumerate((cidx * FFN_COLS, dff + cidx * FFN_COLS)):
            up = jnp.dot(hext, wup_ref[:, base:base + FFN_COLS], preferred_element_type=F32)
            for s in range(nsl):
                lanes = slice(base + s * LANES, base + (s + 1) * LANES)
                slab = (2 * slot + half) * nsl + s
                up_scr[slab] = up[:, s * LANES:(s + 1) * LANES]
                conv.append([cb_ref[:, lanes] + sum(
                    cw_ref[k:k + 1, lanes] * up_scr[slab, pl.ds(HALO - 1 + par + k, tm // 2, stride=2), :]
                    for k in range(3)) for par in range(2)])
        for s in range(nsl):
            for par in range(2):
                af_scr[slot * nsl + s, pl.ds(par, tm // 2, stride=2), :] = (
                    _silu(conv[nsl + s][par]) * conv[s][par])
            lo = cidx * FFN_COLS + s * LANES
            act_scr[:, lo:lo + LANES] = af_scr[slot * nsl + s].astype(BF16)

    f = jnp.dot(act_scr[...], wdown_ref[...], preferred_element_type=F32)
    o_ref[0] = x + gate_ref[0] * _rms(f, gpost_ref[...])


def _ffn(x, mods, gpre, gpost, wup, cw, cb, wdown, tm):
    b, n, d = x.shape
    dff = wdown.shape[0]
    return pl.pallas_call(
        _ffn_kernel,
        grid=(b, n // tm),
        in_specs=[pl.BlockSpec((1, tm, d), lambda i, j: (i, j, 0))] + _halo_specs(tm, n, d) + [
            _mod_spec(d, 3), _mod_spec(d, 4), _mod_spec(d, 5),
            pl.BlockSpec((1, d), lambda i, j: (0, 0)),
            pl.BlockSpec((1, d), lambda i, j: (0, 0)),
            pl.BlockSpec(wup.shape, lambda i, j: (0, 0), pipeline_mode=pl.Buffered(1)),
            pl.BlockSpec(cw.shape, lambda i, j: (0, 0)),
            pl.BlockSpec(cb.shape, lambda i, j: (0, 0)),
            pl.BlockSpec(wdown.shape, lambda i, j: (0, 0), pipeline_mode=pl.Buffered(1))],
        out_specs=pl.BlockSpec((1, tm, d), lambda i, j: (i, j, 0)),
        out_shape=jax.ShapeDtypeStruct((b, n, d), F32),
        scratch_shapes=[pltpu.VMEM((tm, dff), BF16),
                        pltpu.VMEM((2 * FFN_RING * FFN_COLS // LANES, tm + 2 * HALO, LANES), F32),
                        pltpu.VMEM((FFN_RING * FFN_COLS // LANES, tm, LANES), F32)],
        compiler_params=pltpu.CompilerParams(dimension_semantics=("arbitrary", "arbitrary"),
                                             vmem_limit_bytes=VMEM_LIMIT),
        name="ffn",
    )(x, x, x, mods, mods, mods, gpre, gpost, wup, cw, cb, wdown)


def kernel(x, c, ctx, c_ctx, pre_norm_mix, post_norm_mix, pre_norm_ffn, post_norm_ffn, w_ada, b_ada,
           w_in, conv_ssd_w, conv_ssd_b, dt_bias, a_log, d_skip, ssm_norm, w_pool, pool_scale, w_out,
           w_up, conv_ffn_w, conv_ffn_b, w_down):
    assert w_ada.shape[0] == 1, "single-layer block"
    b, n, d = x.shape
    n_ctx = ctx.shape[1]
    d_xbc = D_SSM + D_BC
    d_xb = D_SSM + D_BC // 2
    off_dt = D_SSM + d_xbc
    off_pool = off_dt + 2 * SSD_HEADS
    tm_in = 1024

    assert b < ADA_ROWS and w_ada.shape[2] == N_MODS * d
    cc = jnp.concatenate([c, c_ctx[None, :], jnp.zeros((ADA_ROWS - b - 1, d), F32)], axis=0)
    mods = _ada(cc, w_ada[0], b_ada)

    wi = w_in[0]
    w_all = wi.astype(BF16)
    w_pd = jnp.concatenate([wi[:, off_pool:], wi[:, off_dt:off_pool],
                            jnp.zeros((d, LANES - 2 * SSD_HEADS), F32)], axis=1).astype(BF16)
    bias = jnp.broadcast_to(dt_bias[0].reshape(2 * SSD_HEADS, 1), (2 * SSD_HEADS, tm_in))
    alog = jnp.broadcast_to(a_log[0].reshape(2 * SSD_HEADS, 1), (2 * SSD_HEADS, tm_in))

    xs, bc, zu, rows, cols, ds = _mix_in(x, mods, None, pre_norm_mix, w_all, w_pd, conv_ssd_w[0],
                                         conv_ssd_b, bias, alog, tm_in, True, "mix_in")
    crows, cds = _mix_in(ctx, mods, b, pre_norm_mix, w_all, w_pd, conv_ssd_w[0, :, :d_xb],
                         conv_ssd_b[:, :d_xb], bias, alog, tm_in, False, "mix_in_ctx")

    dsk = jnp.repeat(d_skip[0], SSD_HEAD_DIM)[None, :]
    x1, (w_up_b, w_down_b) = _mix_out(xs, bc, zu, rows, cols, ds, cds, crows, x, mods, post_norm_mix, dsk,
                                      ssm_norm, w_pool[0].astype(BF16), pool_scale, w_out[0].astype(BF16),
                                      [w_up[0], w_down[0]], 512)

    return _ffn(x1, mods, pre_norm_ffn, post_norm_ffn, w_up_b, conv_ffn_w[0], conv_ffn_b, w_down_b, 1024)
```

```python
import functools

import numpy as np
import jax
import jax.numpy as jnp
from jax import lax
from jax.experimental import pallas as pl
from jax.experimental.pallas import tpu as pltpu

F32 = jnp.float32
BF16 = jnp.bfloat16

EPS = 1e-6
GRID_W = 64
SSD_HEADS = 8
SSD_HEAD_DIM = 64
SSD_GROUPS = 2
SSD_STATE = 128
SSD_PAIRS = SSD_HEADS // 2
SSD_CONV = 5
D_SSM = SSD_HEADS * SSD_HEAD_DIM
D_BC = 2 * SSD_GROUPS * SSD_STATE
POOL_WINDOWS = (2, 4, 8, 16)
D_POOL = 512
CHUNK = 128
LANES = 128
HALO = 8
N_ROWS = 9
VMEM_LIMIT = 56 * 1024 * 1024


def _silu(v):
    return v / (1.0 + jnp.exp(-v))


def _rms(v, gain):
    ms = jnp.mean(v * v, axis=-1, keepdims=True)
    return v * lax.rsqrt(ms + EPS) * gain


N_MODS = 6
ADA_ROWS = 16


def _mod_spec(d, k, row=None):
    if row is None:
        return pl.BlockSpec((1, 1, d), lambda i, j: (k * ADA_ROWS + i, 0, 0))
    return pl.BlockSpec((1, 1, d), lambda i, j: (k * ADA_ROWS + row, 0, 0))


def _halo_specs(tm, n, d):
    hb, nb = tm // HALO, n // HALO
    return [pl.BlockSpec((1, HALO, d), lambda i, j: (i, jnp.maximum(j * hb - 1, 0), 0)),
            pl.BlockSpec((1, HALO, d), lambda i, j: (i, jnp.minimum((j + 1) * hb, nb - 1), 0))]


def _cast_blocks(a, steps):
    rps = next(r for r in range(16, a.shape[0] + 1, 16) if a.shape[0] % r == 0 and a.shape[0] // r <= steps)
    return rps, a.shape[0] // rps - 1


def _ada_kernel(c_ref, w_ref, b_ref, o_ref):
    s = _silu(c_ref[...]).astype(BF16)
    o_ref[:, 0, :] = jnp.dot(s, w_ref[...].astype(BF16), preferred_element_type=F32) + b_ref[...]


def _ada(cc, w_ada, b_ada):
    rows, d = cc.shape
    n = w_ada.shape[1]
    return pl.pallas_call(
        _ada_kernel,
        grid=(n // d,),
        in_specs=[pl.BlockSpec((rows, d), lambda j: (0, 0)),
                  pl.BlockSpec((d, d), lambda j: (0, j)),
                  pl.BlockSpec((1, d), lambda j: (0, j))],
        out_specs=pl.BlockSpec((rows, 1, d), lambda j: (j, 0, 0)),
        out_shape=jax.ShapeDtypeStruct((n // d * rows, 1, d), F32),
        compiler_params=pltpu.CompilerParams(dimension_semantics=("arbitrary",),
                                             vmem_limit_bytes=VMEM_LIMIT),
        name="ada",
    )(cc, w_ada, b_ada)


def _dt_rows(dtraw_chunks, bias, a):
    h = SSD_HEADS
    raw = jnp.concatenate([blk.T[0:2 * h, :] for blk in dtraw_chunks], axis=1) + bias
    t = raw.shape[1]
    dt = jnp.maximum(raw, 0.0) + jnp.log1p(jnp.exp(-jnp.abs(raw)))
    dta = dt * a
    seg = lax.broadcasted_iota(jnp.int32, dta.shape, 1) & (CHUNK - 1)
    cs, rcs = dta, dta
    k = 1
    while k < CHUNK:
        cs = cs + jnp.where(seg >= k, pltpu.roll(cs, k, 1), 0.0)
        rcs = rcs + jnp.where(seg < CHUNK - k, pltpu.roll(rcs, t - k, 1), 0.0)
        k *= 2
    cd = jnp.exp(cs + rcs - dta)
    wf = jnp.exp(rcs[0:h] - dta[0:h]) * dt[0:h]
    wb = jnp.exp(cs[h:] - dta[h:]) * dt[h:]
    log2e = 1.0 / np.log(2.0)
    pf2, ab2 = cs[0:h] * log2e, rcs[h:] * log2e
    ldt2 = jnp.log(dt) * log2e
    ldg2 = jnp.log(dt[0:h] + dt[h:]) * log2e
    return [pf2, ab2, pf2 - ldt2[0:h], ab2 - ldt2[h:], ldg2, wf, wb, cd[0:h], cd[h:]]


def _pair_rhs(xs, p):
    xp = xs[:, p * LANES:(p + 1) * LANES]
    lane = lax.broadcasted_iota(jnp.int32, xp.shape, 1)
    zero = jnp.zeros_like(xp)
    return jnp.concatenate([jnp.where(lane < SSD_HEAD_DIM, xp, zero),
                            jnp.where(lane >= SSD_HEAD_DIM, xp, zero)], axis=0)


def _pair_select(lane, v0, v1):
    return jnp.where(lane < SSD_HEAD_DIM, v0, v1)


def _local_states(xs, bts, wf, wb):
    out_f, out_b = [], []
    for p in range(SSD_PAIRS):
        bt = bts[p // 2]
        h0, h1 = 2 * p, 2 * p + 1
        lhs = jnp.concatenate(
            [jnp.concatenate([(bt * w[h0:h0 + 1, :]).astype(BF16), (bt * w[h1:h1 + 1, :]).astype(BF16)], axis=1)
             for w in (wf, wb)], axis=0)
        res = jnp.dot(lhs, _pair_rhs(xs, p), preferred_element_type=F32)
        out_f.append(res[0:SSD_STATE])
        out_b.append(res[SSD_STATE:2 * SSD_STATE])
    return out_f, out_b


def _mix_in_kernel(*refs, conv_cols, latent):
    (x_ref, xp_ref, xn_ref, shift_ref, scale_ref, g_ref, w_ref, wpd_ref, cw_ref, cb_ref,
     bias_ref, alog_ref) = refs[:12]
    if latent:
        xs_ref, bc_ref, zu_ref, rows_ref, cols_ref, ds_ref, up_scr, cv_scr = refs[12:]
    else:
        rows_ref, ds_ref, up_scr, cv_scr = refs[12:]
    nseg, seg, d = x_ref.shape
    tm = nseg * seg
    nch = tm // CHUNK
    cps = seg // CHUNK
    j = pl.program_id(1)
    nj = pl.num_programs(1)

    def hmod(v):
        return _rms(v, g_ref[...]) * (1.0 + scale_ref[0]) + shift_ref[0]

    xm = hmod(x_ref[...].reshape(tm, d))
    gap = jnp.zeros((HALO, d), F32)
    if latent:
        hp = jnp.where(j > 0, hmod(xp_ref[0]), 0.0)
        hn = jnp.where(j < nj - 1, hmod(xn_ref[0]), 0.0)
    else:
        hp = hn = gap
    hm = xm.astype(BF16)
    if latent:
        blocks = [(jnp.concatenate([hp, xm, hn], axis=0).astype(BF16), 0)]
        convs = [(HALO, 0, tm)]
    else:
        pieces = [gap]
        for i in range(nseg):
            pieces += [xm[i * seg:(i + 1) * seg], gap]
        ext_rows = up_scr.shape[1]
        if ext_rows > tm + (nseg + 1) * HALO:
            pieces.append(jnp.zeros((ext_rows - tm - (nseg + 1) * HALO, d), F32))
        blocks = [(jnp.concatenate(pieces, axis=0).astype(BF16), 0)]
        convs = [(HALO + i * (seg + HALO), i * seg, seg) for i in range(nseg)]

    def zero_of(v):
        u = pltpu.bitcast(v[0:8, 0:LANES], jnp.uint32)
        u = lax.shift_right_logical(lax.shift_right_logical(u, jnp.uint32(16)), jnp.uint32(16))
        return pltpu.bitcast(u, F32)[0:1, :]

    ties = {}
    if latent:
        pd = jnp.dot(hm, wpd_ref[...], preferred_element_type=F32)
        zu_ref[0, :, D_SSM:D_SSM + D_POOL] = pd[:, 0:D_POOL].astype(BF16)
        dtraw = pd[:, D_POOL:D_POOL + LANES]
        for hz in range(2):
            zc = slice(hz * 2 * LANES, (hz + 1) * 2 * LANES)
            zp = jnp.dot(hm, w_ref[:, zc], preferred_element_type=F32)
            zu_ref[0, :, zc] = zp.astype(BF16)
            ties[conv_cols // LANES - 2 + hz] = zero_of(zp)
    else:
        dtraw = jnp.dot(hm, wpd_ref[:, D_POOL:D_POOL + LANES], preferred_element_type=F32)

    half = SSD_CONV // 2
    cc = 2 * LANES
    for hblk, srow in blocks:
        for cb in range(conv_cols // cc):
            up = jnp.dot(hblk, w_ref[:, D_SSM + cb * cc:D_SSM + (cb + 1) * cc], preferred_element_type=F32)
            for s in range(cc // LANES):
                up_scr[cb * (cc // LANES) + s, srow:srow + hblk.shape[0], :] = up[:, s * LANES:(s + 1) * LANES]
    for slab in range(conv_cols // LANES):
        lanes = slice(slab * LANES, (slab + 1) * LANES)
        for drow, orow, nrows in convs:
            for par in range(2):
                lo = drow - half + par
                acc = cb_ref[:, lanes] + cw_ref[0:1, lanes] * up_scr[slab, pl.ds(lo, nrows // 2, stride=2), :]
                for k in range(1, SSD_CONV):
                    acc = acc + cw_ref[k:k + 1, lanes] * up_scr[slab, pl.ds(lo + k, nrows // 2, stride=2), :]
                if par == 0 and slab in ties:
                    acc = acc + ties[slab]
                cv_scr[slab, pl.ds(orow + par, nrows // 2, stride=2), :] = _silu(acc)
        if latent:
            if slab < D_SSM // LANES:
                xs_ref[0, :, lanes] = cv_scr[slab].astype(BF16)
            else:
                bc_ref[0, :, slab * LANES - D_SSM:(slab + 1) * LANES - D_SSM] = cv_scr[slab].astype(BF16)

    rows = _dt_rows([dtraw[c * CHUNK:(c + 1) * CHUNK] for c in range(nch)], bias_ref[...],
                    -jnp.exp(alog_ref[...]))
    pad = jnp.zeros((LANES - 2 * SSD_HEADS, CHUNK), F32)
    for c in range(nch):
        lanes = slice(c * CHUNK, (c + 1) * CHUNK)
        for q, arr in enumerate(rows):
            rows_ref[c // cps, c % cps, q] = arr[:, lanes]
        if latent:
            cols_ref[0, c] = jnp.concatenate([rows[0][:, lanes], rows[1][:, lanes], pad], axis=0).T

    def state_body(c, carry):
        r0 = pl.multiple_of(c * CHUNK, CHUNK)
        ci, cj = c // cps, c % cps
        xs = jnp.concatenate([cv_scr[s, pl.ds(r0, CHUNK), :] for s in range(D_SSM // LANES)],
                             axis=1).astype(BF16)
        bts = [cv_scr[D_SSM // LANES + g, pl.ds(r0, CHUNK), :].T for g in range(SSD_GROUPS)]
        dsf, dsb = _local_states(xs, bts, rows_ref[ci, cj, 5], rows_ref[ci, cj, 6])
        for p in range(SSD_PAIRS):
            ds_ref[ci, cj, 0, p] = dsf[p].astype(BF16)
            ds_ref[ci, cj, 1, p] = dsb[p].astype(BF16)
        return carry

    lax.fori_loop(0, nch, state_body, 0, unroll=4)


def _mix_in(x, mods, mod_row, gain, w, wpd, conv_w, conv_b, bias, alog, tm, latent, name):
    b, n, d = x.shape
    conv_cols = conv_w.shape[1]
    nch = tm // CHUNK
    nseg = 1 if latent else tm // n
    seg = tm // nseg
    cps = seg // CHUNK
    ext_rows = -(-(tm + (nseg + 1) * HALO) // 16) * 16
    full = lambda a: pl.BlockSpec(a.shape, lambda i, j: (0,) * a.ndim)
    stat = lambda tail, dt: jax.ShapeDtypeStruct((b, n // CHUNK) + tail, dt)
    out_specs, out_shape = [], []
    if latent:
        grid = (b, n // tm)
        x_spec = pl.BlockSpec((1, tm, d), lambda i, j: (i, j, 0))
        out_specs += [pl.BlockSpec((1, tm, D_SSM), lambda i, j: (i, j, 0)),
                      pl.BlockSpec((1, tm, conv_cols - D_SSM), lambda i, j: (i, j, 0)),
                      pl.BlockSpec((1, tm, D_SSM + D_POOL), lambda i, j: (i, j, 0))]
        out_shape += [jax.ShapeDtypeStruct((b, n, D_SSM), BF16),
                      jax.ShapeDtypeStruct((b, n, conv_cols - D_SSM), BF16),
                      jax.ShapeDtypeStruct((b, n, D_SSM + D_POOL), BF16)]
        blk = lambda tail: pl.BlockSpec((1, nch) + tail, lambda i, j: (i, j) + (0,) * len(tail))
    else:
        grid = (b // nseg, 1)
        x_spec = pl.BlockSpec((nseg, n, d), lambda i, j: (i, 0, 0))
        blk = lambda tail: pl.BlockSpec((nseg, cps) + tail, lambda i, j: (i, 0) + (0,) * len(tail))
    out_specs.append(blk((N_ROWS, SSD_HEADS, CHUNK)))
    out_shape.append(stat((N_ROWS, SSD_HEADS, CHUNK), F32))
    if latent:
        out_specs.append(blk((CHUNK, LANES)))
        out_shape.append(stat((CHUNK, LANES), F32))
    out_specs.append(blk((2, SSD_PAIRS, SSD_STATE, LANES)))
    out_shape.append(stat((2, SSD_PAIRS, SSD_STATE, LANES), BF16))
    return pl.pallas_call(
        functools.partial(_mix_in_kernel, conv_cols=conv_cols, latent=latent),
        grid=grid,
        in_specs=[x_spec] + _halo_specs(seg, n, d) + [
            _mod_spec(d, 0, mod_row), _mod_spec(d, 1, mod_row),
            full(gain), full(w), full(wpd), full(conv_w), full(conv_b), full(bias), full(alog)],
        out_specs=out_specs,
        out_shape=out_shape,
        scratch_shapes=[pltpu.VMEM((conv_cols // LANES, ext_rows, LANES), F32),
                        pltpu.VMEM((conv_cols // LANES, tm, LANES), F32)],
        compiler_params=pltpu.CompilerParams(dimension_semantics=("arbitrary", "arbitrary"),
                                             vmem_limit_bytes=VMEM_LIMIT),
        name=name,
    )(x, x, x, mods, mods, gain, w, wpd, conv_w, conv_b, bias, alog)


def _mix_out_kernel(xs_ref, bc_ref, z_ref, u_ref, rows_ref, cols_ref, ds_ref, cds_ref, crows_ref,
                    x_ref, gate_ref, gn_ref, dsk_ref, gssm_ref, wpool_ref, pscale_ref, wout_ref,
                    cm_ref, inv_ref, *rest):
    ncast = (len(rest) - 5) // 2
    o_ref = rest[ncast]
    sf_scr, sb_scr, d_scr, mix_scr = rest[2 * ncast + 1:]
    n = u_ref.shape[1]
    tm = x_ref.shape[1]
    nchunks = n // CHUNK
    nctx = cds_ref.shape[1]
    tch = tm // CHUNK
    j = pl.program_id(1)
    lane_row = lax.broadcasted_iota(jnp.int32, (1, LANES), 1)

    def cd_row(ref, c, q, p):
        cd = ref[0, c, q]
        return _pair_select(lane_row, cd[2 * p:2 * p + 1, :], cd[2 * p + 1:2 * p + 2, :])

    @pl.when(j == 0)
    def _():
        for p in range(SSD_PAIRS):
            sf = cds_ref[0, 0, 0, p].astype(F32)
            for c in range(1, nctx):
                sf = sf * cd_row(crows_ref, c, 7, p) + cds_ref[0, c, 0, p].astype(F32)
            sb = cds_ref[0, nctx - 1, 1, p].astype(F32)
            for c in reversed(range(nctx - 1)):
                sb = sb * cd_row(crows_ref, c, 8, p) + cds_ref[0, c, 1, p].astype(F32)
            sf_scr[0, p] = sf
            sb_scr[nchunks - 1, p] = sb

        def scan_body(i, carry):
            cf = i
            cbk = nchunks - 1 - i
            for p in range(SSD_PAIRS):
                sf_scr[cf + 1, p] = (sf_scr[cf, p] * cd_row(rows_ref, cf, 7, p)
                                     + ds_ref[0, cf, 0, p].astype(F32))
                sb_scr[cbk - 1, p] = (sb_scr[cbk, p] * cd_row(rows_ref, cbk, 8, p)
                                      + ds_ref[0, cbk, 1, p].astype(F32))
            return carry

        lax.fori_loop(0, nchunks - 1, scan_body, 0)

        for gi, w in enumerate(POOL_WINDOWS):
            cols = slice(gi * LANES, (gi + 1) * LANES)
            cmat = cm_ref[gi]
            parts = [jnp.dot(cmat, u_ref[0, i * LANES:(i + 1) * LANES, cols], preferred_element_type=F32)
                     for i in range(n // LANES)]
            cs = jnp.concatenate(parts, axis=0)
            h = jnp.concatenate([jnp.zeros((w // 2 * GRID_W, LANES), F32), cs], axis=0)
            k = 1
            while k < w:
                sh = k * GRID_W
                h = h + jnp.concatenate([h[sh:], jnp.zeros((sh, LANES), F32)], axis=0)
                k *= 2
            m = h[0:n] * inv_ref[gi]
            d_scr[:, cols] = (m - u_ref[0, :, cols].astype(F32)).astype(BF16)

    row_i = lax.broadcasted_iota(jnp.int32, (CHUNK, CHUNK), 0)
    col_i = lax.broadcasted_iota(jnp.int32, (CHUNK, CHUNK), 1)
    lane_full = lax.broadcasted_iota(jnp.int32, (CHUNK, LANES), 1)

    for lc in range(tch):
        c = j * tch + lc
        r0 = lc * CHUNK
        xs = xs_ref[0, pl.ds(r0, CHUNK), :]
        pfp, abp, ldg = rows_ref[0, c, 2], rows_ref[0, c, 3], rows_ref[0, c, 4]
        cols = cols_ref[0, c]

        y_pairs = []
        for g in range(SSD_GROUPS):
            bg = bc_ref[0, pl.ds(r0, CHUNK), g * SSD_STATE:(g + 1) * SSD_STATE]
            cg = bc_ref[0, pl.ds(r0, CHUNK), (SSD_GROUPS + g) * SSD_STATE:(SSD_GROUPS + g + 1) * SSD_STATE]
            gmat = lax.dot_general(cg, bg, (((1,), (1,)), ((), ())), preferred_element_type=F32)
            scat = jnp.concatenate([sf_scr[c, 2 * g].astype(BF16), sf_scr[c, 2 * g + 1].astype(BF16),
                                    sb_scr[c, 2 * g].astype(BF16), sb_scr[c, 2 * g + 1].astype(BF16)], axis=1)
            yoff = jnp.dot(cg, scat, preferred_element_type=F32)
            for q in range(2):
                p = 2 * g + q
                ms, efs, ebs = [], [], []
                for h in (2 * p, 2 * p + 1):
                    pf_c = jnp.broadcast_to(cols[:, h:h + 1], (CHUNK, CHUNK))
                    ab_c = jnp.broadcast_to(cols[:, SSD_HEADS + h:SSD_HEADS + h + 1], (CHUNK, CHUNK))
                    dm = jnp.where(row_i > col_i, pf_c - pfp[h:h + 1, :],
                                   jnp.where(row_i < col_i, ab_c - abp[h:h + 1, :], ldg[h:h + 1, :]))
                    ms.append((gmat * jnp.exp2(dm)).astype(BF16))
                    efs.append(jnp.exp2(pf_c))
                    ebs.append(jnp.exp2(ab_c))
                ydiag = jnp.dot(jnp.concatenate(ms, axis=1), _pair_rhs(xs, p), preferred_element_type=F32)
                y_pairs.append(ydiag
                               + _pair_select(lane_full, efs[0], efs[1]) * yoff[:, q * LANES:(q + 1) * LANES]
                               + _pair_select(lane_full, ebs[0], ebs[1]) * yoff[:, (2 + q) * LANES:(3 + q) * LANES])
        y = jnp.concatenate(y_pairs, axis=1) + dsk_ref[...] * xs.astype(F32)
        yz = y * _silu(z_ref[0, pl.ds(r0, CHUNK), :].astype(F32))
        yn = _rms(yz, gssm_ref[...]).astype(BF16)

        drows = pl.ds(pl.multiple_of(j * tm + r0, CHUNK), CHUNK)
        ps = [jnp.dot(d_scr[drows, gi * LANES:(gi + 1) * LANES], wpool_ref[gi], preferred_element_type=F32)
              for gi in range(len(POOL_WINDOWS))]
        pm = jnp.concatenate(ps, axis=1) * pscale_ref[...]
        mix_scr[pl.ds(r0, CHUNK), 0:D_SSM] = yn
        mix_scr[pl.ds(r0, CHUNK), D_SSM:D_SSM + D_POOL] = pm.astype(BF16)
        yx = jnp.dot(mix_scr[pl.ds(r0, CHUNK), :], wout_ref[...], preferred_element_type=F32)
        o_ref[0, pl.ds(r0, CHUNK), :] = x_ref[0, pl.ds(r0, CHUNK), :] + gate_ref[0] * _rms(yx, gn_ref[...])

    for src, dst in zip(rest[:ncast], rest[ncast + 1:2 * ncast + 1]):
        dst[...] = src[...].astype(BF16)


def _pool_constants(n):
    rows = n // GRID_W
    t = np.arange(LANES)
    blk, col = t // GRID_W, t % GRID_W
    cms, invs = [], []
    r = np.arange(n) // GRID_W
    c = np.arange(n) % GRID_W
    for w in POOL_WINDOWS:
        d = col[None, :] - col[:, None]
        cms.append(((blk[:, None] == blk[None, :]) & (d >= -(w // 2)) & (d < w - w // 2)).astype(np.float32))
        cnt_r = np.minimum(r + w - w // 2, rows) - np.maximum(r - w // 2, 0)
        cnt_c = np.minimum(c + w - w // 2, GRID_W) - np.maximum(c - w // 2, 0)
        invs.append(np.broadcast_to((1.0 / (cnt_r * cnt_c))[:, None], (n, LANES)).astype(np.float32))
    return np.stack(cms), np.stack(invs)


def _mix_out(xs, bc, zu, rows, cols, ds, cds, crows, x, mods, gn, dsk, gssm, wpool, pscale, wout, casts, tm):
    b, n, d = x.shape
    nchunks = n // CHUNK
    nj = n // tm
    cm_np, inv_np = _pool_constants(n)
    cm = jnp.asarray(cm_np, BF16)
    inv = jnp.asarray(inv_np, F32)
    full = lambda a: pl.BlockSpec(a.shape, lambda i, j: (0,) * a.ndim)
    per_batch = lambda a: pl.BlockSpec((1,) + a.shape[1:], lambda i, j: (i,) + (0,) * (a.ndim - 1))
    cast_specs = []
    for a in casts:
        rps, last = _cast_blocks(a, b * nj)
        cast_specs.append(pl.BlockSpec((rps, a.shape[1]), lambda i, j, last=last: (jnp.minimum(i * nj + j, last), 0)))
    outs = pl.pallas_call(
        _mix_out_kernel,
        grid=(b, n // tm),
        in_specs=[pl.BlockSpec((1, tm, D_SSM), lambda i, j: (i, j, 0)),
                  pl.BlockSpec((1, tm, D_BC), lambda i, j: (i, j, 0)),
                  pl.BlockSpec((1, tm, D_SSM), lambda i, j: (i, j, 0)),
                  pl.BlockSpec((1, n, D_POOL), lambda i, j: (i, 0, 1)),
                  per_batch(rows), per_batch(cols), per_batch(ds), per_batch(cds), per_batch(crows),
                  pl.BlockSpec((1, tm, d), lambda i, j: (i, j, 0)),
                  _mod_spec(d, 2),
                  full(gn), full(dsk), full(gssm), full(wpool), full(pscale), full(wout), full(cm),
                  pl.BlockSpec(inv.shape, lambda i, j: (0, 0, 0), pipeline_mode=pl.Buffered(1))] + cast_specs,
        out_specs=[pl.BlockSpec((1, tm, d), lambda i, j: (i, j, 0))] + cast_specs,
        out_shape=[jax.ShapeDtypeStruct((b, n, d), F32)] + [jax.ShapeDtypeStruct(a.shape, BF16) for a in casts],
        scratch_shapes=[pltpu.VMEM((nchunks, SSD_PAIRS, SSD_STATE, LANES), F32),
                        pltpu.VMEM((nchunks, SSD_PAIRS, SSD_STATE, LANES), F32),
                        pltpu.VMEM((n, D_POOL), BF16),
                        pltpu.VMEM((tm, D_SSM + D_POOL), BF16)],
        compiler_params=pltpu.CompilerParams(dimension_semantics=("arbitrary", "arbitrary"),
                                             vmem_limit_bytes=VMEM_LIMIT),
        name="mix_out",
    )(xs, bc, zu, zu, rows, cols, ds, cds, crows, x, mods, gn, dsk, gssm, wpool, pscale, wout, cm, inv, *casts)
    return outs[0], outs[1:]


FFN_COLS = 2 * LANES
FFN_RING = 2


def _ffn_kernel(x_ref, xp_ref, xn_ref, shift_ref, scale_ref, gate_ref, gpre_ref, gpost_ref,
                wup_ref, cw_ref, cb_ref, wdown_ref, o_ref, act_scr, up_scr, af_scr):
    tm = x_ref.shape[1]
    dff = wdown_ref.shape[0]
    j = pl.program_id(1)
    nj = pl.num_programs(1)

    def hmod(v):
        return _rms(v, gpre_ref[...]) * (1.0 + scale_ref[0]) + shift_ref[0]

    x = x_ref[0]
    hp = jnp.where(j > 0, hmod(xp_ref[0]), 0.0)
    hn = jnp.where(j < nj - 1, hmod(xn_ref[0]), 0.0)
    hext = jnp.concatenate([hp, hmod(x), hn], axis=0).astype(BF16)

    nsl = FFN_COLS // LANES
    for cidx in range(dff // FFN_COLS):
        slot = cidx % FFN_RING
        conv = []
        for half, base in enumerate((cidx * FFN_COLS, dff + cidx * FFN_COLS)):
            up = jnp.dot(hext, wup_ref[:, base:base + FFN_COLS], preferred_element_type=F32)
            for s in range(nsl):
                lanes = slice(base + s * LANES, base + (s + 1) * LANES)
                slab = (2 * slot + half) * nsl + s
                up_scr[slab] = up[:, s * LANES:(s + 1) * LANES]
                conv.append([cb_ref[:, lanes] + sum(
                    cw_ref[k:k + 1, lanes] * up_scr[slab, pl.ds(HALO - 1 + par + k, tm // 2, stride=2), :]
                    for k in range(3)) for par in range(2)])
        for s in range(nsl):
            for par in range(2):
                af_scr[slot * nsl + s, pl.ds(par, tm // 2, stride=2), :] = (
                    _silu(conv[nsl + s][par]) * conv[s][par])
            lo = cidx * FFN_COLS + s * LANES
            act_scr[:, lo:lo + LANES] = af_scr[slot * nsl + s].astype(BF16)

    f = jnp.dot(act_scr[...], wdown_ref[...], preferred_element_type=F32)
    o_ref[0] = x + gate_ref[0] * _rms(f, gpost_ref[...])


def _ffn(x, mods, gpre, gpost, wup, cw, cb, wdown, tm):
    b, n, d = x.shape
    dff = wdown.shape[0]
    return pl.pallas_call(
        _ffn_kernel,
        grid=(b, n // tm),
        in_specs=[pl.BlockSpec((1, tm, d), lambda i, j: (i, j, 0))] + _halo_specs(tm, n, d) + [
            _mod_spec(d, 3), _mod_spec(d, 4), _mod_spec(d, 5),
            pl.BlockSpec((1, d), lambda i, j: (0, 0)),
            pl.BlockSpec((1, d), lambda i, j: (0, 0)),
            pl.BlockSpec(wup.shape, lambda i, j: (0, 0), pipeline_mode=pl.Buffered(1)),
            pl.BlockSpec(cw.shape, lambda i, j: (0, 0)),
            pl.BlockSpec(cb.shape, lambda i, j: (0, 0)),
            pl.BlockSpec(wdown.shape, lambda i, j: (0, 0), pipeline_mode=pl.Buffered(1))],
        out_specs=pl.BlockSpec((1, tm, d), lambda i, j: (i, j, 0)),
        out_shape=jax.ShapeDtypeStruct((b, n, d), F32),
        scratch_shapes=[pltpu.VMEM((tm, dff), BF16),
                        pltpu.VMEM((2 * FFN_RING * FFN_COLS // LANES, tm + 2 * HALO, LANES), F32),
                        pltpu.VMEM((FFN_RING * FFN_COLS // LANES, tm, LANES), F32)],
        compiler_params=pltpu.CompilerParams(dimension_semantics=("arbitrary", "arbitrary"),
                                             vmem_limit_bytes=VMEM_LIMIT),
        name="ffn",
    )(x, x, x, mods, mods, mods, gpre, gpost, wup, cw, cb, wdown)


def kernel(x, c, ctx, c_ctx, pre_norm_mix, post_norm_mix, pre_norm_ffn, post_norm_ffn, w_ada, b_ada,
           w_in, conv_ssd_w, conv_ssd_b, dt_bias, a_log, d_skip, ssm_norm, w_pool, pool_scale, w_out,
           w_up, conv_ffn_w, conv_ffn_b, w_down):
    assert w_ada.shape[0] == 1, "single-layer block"
    b, n, d = x.shape
    n_ctx = ctx.shape[1]
    d_xbc = D_SSM + D_BC
    d_xb = D_SSM + D_BC // 2
    off_dt = D_SSM + d_xbc
    off_pool = off_dt + 2 * SSD_HEADS
    tm_in = 1024

    assert b < ADA_ROWS and w_ada.shape[2] == N_MODS * d
    cc = jnp.concatenate([c, c_ctx[None, :], jnp.zeros((ADA_ROWS - b - 1, d), F32)], axis=0)
    mods = _ada(cc, w_ada[0], b_ada)

    wi = w_in[0]
    w_all = wi.astype(BF16)
    w_pd = jnp.concatenate([wi[:, off_pool:], wi[:, off_dt:off_pool],
                            jnp.zeros((d, LANES - 2 * SSD_HEADS), F32)], axis=1).astype(BF16)
    bias = jnp.broadcast_to(dt_bias[0].reshape(2 * SSD_HEADS, 1), (2 * SSD_HEADS, tm_in))
    alog = jnp.broadcast_to(a_log[0].reshape(2 * SSD_HEADS, 1), (2 * SSD_HEADS, tm_in))

    xs, bc, zu, rows, cols, ds = _mix_in(x, mods, None, pre_norm_mix, w_all, w_pd, conv_ssd_w[0],
                                         conv_ssd_b, bias, alog, tm_in, True, "mix_in")
    crows, cds = _mix_in(ctx, mods, b, pre_norm_mix, w_all, w_pd, conv_ssd_w[0, :, :d_xb],
                         conv_ssd_b[:, :d_xb], bias, alog, tm_in, False, "mix_in_ctx")

    dsk = jnp.repeat(d_skip[0], SSD_HEAD_DIM)[None, :]
    x1, (w_up_b, w_down_b) = _mix_out(xs, bc, zu, rows, cols, ds, cds, crows, x, mods, post_norm_mix, dsk,
                                      ssm_norm, w_pool[0].astype(BF16), pool_scale, w_out[0].astype(BF16),
                                      [w_up[0], w_down[0]], 512)

    return _ffn(x1, mods, pre_norm_ffn, post_norm_ffn, w_up_b, conv_ffn_w[0], conv_ffn_b, w_down_b, 1024)
```

```python
import functools

import numpy as np
import jax
import jax.numpy as jnp
from jax import lax
from jax.experimental import pallas as pl
from jax.experimental.pallas import tpu as pltpu

F32 = jnp.float32
BF16 = jnp.bfloat16

EPS = 1e-6
GRID_W = 64
SSD_HEADS = 8
SSD_HEAD_DIM = 64
SSD_GROUPS = 2
SSD_STATE = 128
SSD_PAIRS = SSD_HEADS // 2
SSD_CONV = 5
D_SSM = SSD_HEADS * SSD_HEAD_DIM
D_BC = 2 * SSD_GROUPS * SSD_STATE
POOL_WINDOWS = (2, 4, 8, 16)
D_POOL = 512
CHUNK = 128
LANES = 128
HALO = 8
N_ROWS = 9
VMEM_LIMIT = 56 * 1024 * 1024


def _silu(v):
    return v / (1.0 + jnp.exp(-v))


def _rms(v, gain):
    ms = jnp.mean(v * v, axis=-1, keepdims=True)
    return v * lax.rsqrt(ms + EPS) * gain


N_MODS = 6
ADA_ROWS = 16


def _mod_spec(d, k, row=None):
    if row is None:
        return pl.BlockSpec((1, 1, d), lambda i, j: (k * ADA_ROWS + i, 0, 0))
    return pl.BlockSpec((1, 1, d), lambda i, j: (k * ADA_ROWS + row, 0, 0))


def _halo_specs(tm, n, d):
    hb, nb = tm // HALO, n // HALO
    return [pl.BlockSpec((1, HALO, d), lambda i, j: (i, jnp.maximum(j * hb - 1, 0), 0)),
            pl.BlockSpec((1, HALO, d), lambda i, j: (i, jnp.minimum((j + 1) * hb, nb - 1), 0))]


def _cast_blocks(a, steps):
    rps = next(r for r in range(16, a.shape[0] + 1, 16) if a.shape[0] % r == 0 and a.shape[0] // r <= steps)
    return rps, a.shape[0] // rps - 1


def _ada_kernel(c_ref, w_ref, b_ref, o_ref):
    s = _silu(c_ref[...]).astype(BF16)
    o_ref[:, 0, :] = jnp.dot(s, w_ref[...].astype(BF16), preferred_element_type=F32) + b_ref[...]


def _ada(cc, w_ada, b_ada):
    rows, d = cc.shape
    n = w_ada.shape[1]
    return pl.pallas_call(
        _ada_kernel,
        grid=(n // d,),
        in_specs=[pl.BlockSpec((rows, d), lambda j: (0, 0)),
                  pl.BlockSpec((d, d), lambda j: (0, j)),
                  pl.BlockSpec((1, d), lambda j: (0, j))],
        out_specs=pl.BlockSpec((rows, 1, d), lambda j: (j, 0, 0)),
        out_shape=jax.ShapeDtypeStruct((n // d * rows, 1, d), F32),
        compiler_params=pltpu.CompilerParams(dimension_semantics=("arbitrary",),
                                             vmem_limit_bytes=VMEM_LIMIT),
        name="ada",
    )(cc, w_ada, b_ada)


def _dt_rows(dtraw_chunks, bias, a):
    h = SSD_HEADS
    raw = jnp.concatenate([blk.T[0:2 * h, :] for blk in dtraw_chunks], axis=1) + bias
    t = raw.shape[1]
    dt = jnp.maximum(raw, 0.0) + jnp.log1p(jnp.exp(-jnp.abs(raw)))
    dta = dt * a
    seg = lax.broadcasted_iota(jnp.int32, dta.shape, 1) & (CHUNK - 1)
    cs, rcs = dta, dta
    k = 1
    while k < CHUNK:
        cs = cs + jnp.where(seg >= k, pltpu.roll(cs, k, 1), 0.0)
        rcs = rcs + jnp.where(seg < CHUNK - k, pltpu.roll(rcs, t - k, 1), 0.0)
        k *= 2
    cd = jnp.exp(cs + rcs - dta)
    wf = jnp.exp(rcs[0:h] - dta[0:h]) * dt[0:h]
    wb = jnp.exp(cs[h:] - dta[h:]) * dt[h:]
    log2e = 1.0 / np.log(2.0)
    pf2, ab2 = cs[0:h] * log2e, rcs[h:] * log2e
    ldt2 = jnp.log(dt) * log2e
    ldg2 = jnp.log(dt[0:h] + dt[h:]) * log2e
    return [pf2, ab2, pf2 - ldt2[0:h], ab2 - ldt2[h:], ldg2, wf, wb, cd[0:h], cd[h:]]


def _pair_rhs(xs, p):
    xp = xs[:, p * LANES:(p + 1) * LANES]
    lane = lax.broadcasted_iota(jnp.int32, xp.shape, 1)
    zero = jnp.zeros_like(xp)
    return jnp.concatenate([jnp.where(lane < SSD_HEAD_DIM, xp, zero),
                            jnp.where(lane >= SSD_HEAD_DIM, xp, zero)], axis=0)


def _pair_select(lane, v0, v1):
    return jnp.where(lane < SSD_HEAD_DIM, v0, v1)


def _local_states(xs, bts, wf, wb):
    out_f, out_b = [], []
    for p in range(SSD_PAIRS):
        bt = bts[p // 2]
        h0, h1 = 2 * p, 2 * p + 1
        lhs = jnp.concatenate(
            [jnp.concatenate([(bt * w[h0:h0 + 1, :]).astype(BF16), (bt * w[h1:h1 + 1, :]).astype(BF16)], axis=1)
             for w in (wf, wb)], axis=0)
        res = jnp.dot(lhs, _pair_rhs(xs, p), preferred_element_type=F32)
        out_f.append(res[0:SSD_STATE])
        out_b.append(res[SSD_STATE:2 * SSD_STATE])
    return out_f, out_b


def _mix_in_kernel(*refs, conv_cols, latent):
    (x_ref, xp_ref, xn_ref, shift_ref, scale_ref, g_ref, w_ref, wpd_ref, cw_ref, cb_ref,
     bias_ref, alog_ref) = refs[:12]
    if latent:
        xs_ref, bc_ref, zu_ref, rows_ref, cols_ref, ds_ref, up_scr, cv_scr = refs[12:]
    else:
        rows_ref, ds_ref, up_scr, cv_scr = refs[12:]
    nseg, seg, d = x_ref.shape
    tm = nseg * seg
    nch = tm // CHUNK
    cps = seg // CHUNK
    j = pl.program_id(1)
    nj = pl.num_programs(1)

    def hmod(v):
        return _rms(v, g_ref[...]) * (1.0 + scale_ref[0]) + shift_ref[0]

    xm = hmod(x_ref[...].reshape(tm, d))
    gap = jnp.zeros((HALO, d), F32)
    if latent:
        hp = jnp.where(j > 0, hmod(xp_ref[0]), 0.0)
        hn = jnp.where(j < nj - 1, hmod(xn_ref[0]), 0.0)
    else:
        hp = hn = gap
    hm = xm.astype(BF16)
    if latent:
        blocks = [(jnp.concatenate([hp, xm, hn], axis=0).astype(BF16), 0)]
        convs = [(HALO, 0, tm)]
    else:
        pieces = [gap]
        for i in range(nseg):
            pieces += [xm[i * seg:(i + 1) * seg], gap]
        ext_rows = up_scr.shape[1]
        if ext_rows > tm + (nseg + 1) * HALO:
            pieces.append(jnp.zeros((ext_rows - tm - (nseg + 1) * HALO, d), F32))
        blocks = [(jnp.concatenate(pieces, axis=0).astype(BF16), 0)]
        convs = [(HALO + i * (seg + HALO), i * seg, seg) for i in range(nseg)]

    def zero_of(v):
        u = pltpu.bitcast(v[0:8, 0:LANES], jnp.uint32)
        u = lax.shift_right_logical(lax.shift_right_logical(u, jnp.uint32(16)), jnp.uint32(16))
        return pltpu.bitcast(u, F32)[0:1, :]

    ties = {}
    if latent:
        pd = jnp.dot(hm, wpd_ref[...], preferred_element_type=F32)
        zu_ref[0, :, D_SSM:D_SSM + D_POOL] = pd[:, 0:D_POOL].astype(BF16)
        dtraw = pd[:, D_POOL:D_POOL + LANES]
        for hz in range(2):
            zc = slice(hz * 2 * LANES, (hz + 1) * 2 * LANES)
            zp = jnp.dot(hm, w_ref[:, zc], preferred_element_type=F32)
            zu_ref[0, :, zc] = zp.astype(BF16)
            ties[conv_cols // LANES - 2 + hz] = zero_of(zp)
    else:
        dtraw = jnp.dot(hm, wpd_ref[:, D_POOL:D_POOL + LANES], preferred_element_type=F32)

    half = SSD_CONV // 2
    cc = 2 * LANES
    for hblk, srow in blocks:
        for cb in range(conv_cols // cc):
            up = jnp.dot(hblk, w_ref[:, D_SSM + cb * cc:D_SSM + (cb + 1) * cc], preferred_element_type=F32)
            for s in range(cc // LANES):
                up_scr[cb * (cc // LANES) + s, srow:srow + hblk.shape[0], :] = up[:, s * LANES:(s + 1) * LANES]
    for slab in range(conv_cols // LANES):
        lanes = slice(slab * LANES, (slab + 1) * LANES)
        for drow, orow, nrows in convs:
            for par in range(2):
                lo = drow - half + par
                acc = cb_ref[:, lanes] + cw_ref[0:1, lanes] * up_scr[slab, pl.ds(lo, nrows // 2, stride=2), :]
                for k in range(1, SSD_CONV):
                    acc = acc + cw_ref[k:k + 1, lanes] * up_scr[slab, pl.ds(lo + k, nrows // 2, stride=2), :]
                if par == 0 and slab in ties:
                    acc = acc + ties[slab]
                cv_scr[slab, pl.ds(orow + par, nrows // 2, stride=2), :] = _silu(acc)
        if latent:
            if slab < D_SSM // LANES:
                xs_ref[0, :, lanes] = cv_scr[slab].astype(BF16)
            else:
                bc_ref[0, :, slab * LANES - D_SSM:(slab + 1) * LANES - D_SSM] = cv_scr[slab].astype(BF16)

    rows = _dt_rows([dtraw[c * CHUNK:(c + 1) * CHUNK] for c in range(nch)], bias_ref[...],
                    -jnp.exp(alog_ref[...]))
    pad = jnp.zeros((LANES - 2 * SSD_HEADS, CHUNK), F32)
    for c in range(nch):
        lanes = slice(c * CHUNK, (c + 1) * CHUNK)
        for q, arr in enumerate(rows):
            rows_ref[c // cps, c % cps, q] = arr[:, lanes]
        if latent:
            cols_ref[0, c] = jnp.concatenate([rows[0][:, lanes], rows[1][:, lanes], pad], axis=0).T

    def state_body(c, carry):
        r0 = pl.multiple_of(c * CHUNK, CHUNK)
        ci, cj = c // cps, c % cps
        xs = jnp.concatenate([cv_scr[s, pl.ds(r0, CHUNK), :] for s in range(D_SSM // LANES)],
                             axis=1).astype(BF16)
        bts = [cv_scr[D_SSM // LANES + g, pl.ds(r0, CHUNK), :].T for g in range(SSD_GROUPS)]
        dsf, dsb = _local_states(xs, bts, rows_ref[ci, cj, 5], rows_ref[ci, cj, 6])
        for p in range(SSD_PAIRS):
            ds_ref[ci, cj, 0, p] = dsf[p].astype(BF16)
            ds_ref[ci, cj, 1, p] = dsb[p].astype(BF16)
        return carry

    lax.fori_loop(0, nch, state_body, 0, unroll=4)


def _mix_in(x, mods, mod_row, gain, w, wpd, conv_w, conv_b, bias, alog, tm, latent, name):
    b, n, d = x.shape
    conv_cols = conv_w.shape[1]
    nch = tm // CHUNK
    nseg = 1 if latent else tm // n
    seg = tm // nseg
    cps = seg // CHUNK
    ext_rows = -(-(tm + (nseg + 1) * HALO) // 16) * 16
    full = lambda a: pl.BlockSpec(a.shape, lambda i, j: (0,) * a.ndim)
    stat = lambda tail, dt: jax.ShapeDtypeStruct((b, n // CHUNK) + tail, dt)
    out_specs, out_shape = [], []
    if latent:
        grid = (b, n // tm)
        x_spec = pl.BlockSpec((1, tm, d), lambda i, j: (i, j, 0))
        out_specs += [pl.BlockSpec((1, tm, D_SSM), lambda i, j: (i, j, 0)),
                      pl.BlockSpec((1, tm, conv_cols - D_SSM), lambda i, j: (i, j, 0)),
                      pl.BlockSpec((1, tm, D_SSM + D_POOL), lambda i, j: (i, j, 0))]
        out_shape += [jax.ShapeDtypeStruct((b, n, D_SSM), BF16),
                      jax.ShapeDtypeStruct((b, n, conv_cols - D_SSM), BF16),
                      jax.ShapeDtypeStruct((b, n, D_SSM + D_POOL), BF16)]
        blk = lambda tail: pl.BlockSpec((1, nch) + tail, lambda i, j: (i, j) + (0,) * len(tail))
    else:
        grid = (b // nseg, 1)
        x_spec = pl.BlockSpec((nseg, n, d), lambda i, j: (i, 0, 0))
        blk = lambda tail: pl.BlockSpec((nseg, cps) + tail, lambda i, j: (i, 0) + (0,) * len(tail))
    out_specs.append(blk((N_ROWS, SSD_HEADS, CHUNK)))
    out_shape.append(stat((N_ROWS, SSD_HEADS, CHUNK), F32))
    if latent:
        out_specs.append(blk((CHUNK, LANES)))
        out_shape.append(stat((CHUNK, LANES), F32))
    out_specs.append(blk((2, SSD_PAIRS, SSD_STATE, LANES)))
    out_shape.append(stat((2, SSD_PAIRS, SSD_STATE, LANES), BF16))
    return pl.pallas_call(
        functools.partial(_mix_in_kernel, conv_cols=conv_cols, latent=latent),
        grid=grid,
        in_specs=[x_spec] + _halo_specs(seg, n, d) + [
            _mod_spec(d, 0, mod_row), _mod_spec(d, 1, mod_row),
            full(gain), full(w), full(wpd), full(conv_w), full(conv_b), full(bias), full(alog)],
        out_specs=out_specs,
        out_shape=out_shape,
        scratch_shapes=[pltpu.VMEM((conv_cols // LANES, ext_rows, LANES), F32),
                        pltpu.VMEM((conv_cols // LANES, tm, LANES), F32)],
        compiler_params=pltpu.CompilerParams(dimension_semantics=("arbitrary", "arbitrary"),
                                             vmem_limit_bytes=VMEM_LIMIT),
        name=name,
    )(x, x, x, mods, mods, gain, w, wpd, conv_w, conv_b, bias, alog)


def _mix_out_kernel(xs_ref, bc_ref, z_ref, u_ref, rows_ref, cols_ref, ds_ref, cds_ref, crows_ref,
                    x_ref, gate_ref, gn_ref, dsk_ref, gssm_ref, wpool_ref, pscale_ref, wout_ref,
                    cm_ref, inv_ref, *rest):
    ncast = (len(rest) - 5) // 2
    o_ref = rest[ncast]
    sf_scr, sb_scr, d_scr, mix_scr = rest[2 * ncast + 1:]
    n = u_ref.shape[1]
    tm = x_ref.shape[1]
    nchunks = n // CHUNK
    nctx = cds_ref.shape[1]
    tch = tm // CHUNK
    j = pl.program_id(1)
    lane_row = lax.broadcasted_iota(jnp.int32, (1, LANES), 1)

    def cd_row(ref, c, q, p):
        cd = ref[0, c, q]
        return _pair_select(lane_row, cd[2 * p:2 * p + 1, :], cd[2 * p + 1:2 * p + 2, :])

    @pl.when(j == 0)
    def _():
        for p in range(SSD_PAIRS):
            sf = cds_ref[0, 0, 0, p].astype(F32)
            for c in range(1, nctx):
                sf = sf * cd_row(crows_ref, c, 7, p) + cds_ref[0, c, 0, p].astype(F32)
            sb = cds_ref[0, nctx - 1, 1, p].astype(F32)
            for c in reversed(range(nctx - 1)):
                sb = sb * cd_row(crows_ref, c, 8, p) + cds_ref[0, c, 1, p].astype(F32)
            sf_scr[0, p] = sf
            sb_scr[nchunks - 1, p] = sb

        def scan_body(i, carry):
            cf = i
            cbk = nchunks - 1 - i
            for p in range(SSD_PAIRS):
                sf_scr[cf + 1, p] = (sf_scr[cf, p] * cd_row(rows_ref, cf, 7, p)
                                     + ds_ref[0, cf, 0, p].astype(F32))
                sb_scr[cbk - 1, p] = (sb_scr[cbk, p] * cd_row(rows_ref, cbk, 8, p)
                                      + ds_ref[0, cbk, 1, p].astype(F32))
            return carry

        lax.fori_loop(0, nchunks - 1, scan_body, 0)

        for gi, w in enumerate(POOL_WINDOWS):
            cols = slice(gi * LANES, (gi + 1) * LANES)
            cmat = cm_ref[gi]
            parts = [jnp.dot(cmat, u_ref[0, i * LANES:(i + 1) * LANES, cols], preferred_element_type=F32)
                     for i in range(n // LANES)]
            cs = jnp.concatenate(parts, axis=0)
            h = jnp.concatenate([jnp.zeros((w // 2 * GRID_W, LANES), F32), cs], axis=0)
            k = 1
            while k < w:
                sh = k * GRID_W
                h = h + jnp.concatenate([h[sh:], jnp.zeros((sh, LANES), F32)], axis=0)
                k *= 2
            m = h[0:n] * inv_ref[gi]
            d_scr[:, cols] = (m - u_ref[0, :, cols].astype(F32)).astype(BF16)

    row_i = lax.broadcasted_iota(jnp.int32, (CHUNK, CHUNK), 0)
    col_i = lax.broadcasted_iota(jnp.int32, (CHUNK, CHUNK), 1)
    lane_full = lax.broadcasted_iota(jnp.int32, (CHUNK, LANES), 1)

    for lc in range(tch):
        c = j * tch + lc
        r0 = lc * CHUNK
        xs = xs_ref[0, pl.ds(r0, CHUNK), :]
        pfp, abp, ldg = rows_ref[0, c, 2], rows_ref[0, c, 3], rows_ref[0, c, 4]
        cols = cols_ref[0, c]

        y_pairs = []
        for g in range(SSD_GROUPS):
            bg = bc_ref[0, pl.ds(r0, CHUNK), g * SSD_STATE:(g + 1) * SSD_STATE]
            cg = bc_ref[0, pl.ds(r0, CHUNK), (SSD_GROUPS + g) * SSD_STATE:(SSD_GROUPS + g + 1) * SSD_STATE]
            gmat = lax.dot_general(cg, bg, (((1,), (1,)), ((), ())), preferred_element_type=F32)
            scat = jnp.concatenate([sf_scr[c, 2 * g].astype(BF16), sf_scr[c, 2 * g + 1].astype(BF16),
                                    sb_scr[c, 2 * g].astype(BF16), sb_scr[c, 2 * g + 1].astype(BF16)], axis=1)
            yoff = jnp.dot(cg, scat, preferred_element_type=F32)
            for q in range(2):
                p = 2 * g + q
                ms, efs, ebs = [], [], []
                for h in (2 * p, 2 * p + 1):
                    pf_c = jnp.broadcast_to(cols[:, h:h + 1], (CHUNK, CHUNK))
                    ab_c = jnp.broadcast_to(cols[:, SSD_HEADS + h:SSD_HEADS + h + 1], (CHUNK, CHUNK))
                    dm = jnp.where(row_i > col_i, pf_c - pfp[h:h + 1, :],
                                   jnp.where(row_i < col_i, ab_c - abp[h:h + 1, :], ldg[h:h + 1, :]))
                    ms.append((gmat * jnp.exp2(dm)).astype(BF16))
                    efs.append(jnp.exp2(pf_c))
                    ebs.append(jnp.exp2(ab_c))
                ydiag = jnp.dot(jnp.concatenate(ms, axis=1), _pair_rhs(xs, p), preferred_element_type=F32)
                y_pairs.append(ydiag
                               + _pair_select(lane_full, efs[0], efs[1]) * yoff[:, q * LANES:(q + 1) * LANES]
                               + _pair_select(lane_full, ebs[0], ebs[1]) * yoff[:, (2 + q) * LANES:(3 + q) * LANES])
        y = jnp.concatenate(y_pairs, axis=1) + dsk_ref[...] * xs.astype(F32)
        yz = y * _silu(z_ref[0, pl.ds(r0, CHUNK), :].astype(F32))
        yn = _rms(yz, gssm_ref[...]).astype(BF16)

        drows = pl.ds(pl.multiple_of(j * tm + r0, CHUNK), CHUNK)
        ps = [jnp.dot(d_scr[drows, gi * LANES:(gi + 1) * LANES], wpool_ref[gi], preferred_element_type=F32)
              for gi in range(len(POOL_WINDOWS))]
        pm = jnp.concatenate(ps, axis=1) * pscale_ref[...]
        yx = (jnp.dot(yn, wout_ref[0:D_SSM, :], preferred_element_type=F32)
              + jnp.dot(pm.astype(BF16), wout_ref[D_SSM:D_SSM + D_POOL, :], preferred_element_type=F32))
        o_ref[0, pl.ds(r0, CHUNK), :] = x_ref[0, pl.ds(r0, CHUNK), :] + gate_ref[0] * _rms(yx, gn_ref[...])

    for src, dst in zip(rest[:ncast], rest[ncast + 1:2 * ncast + 1]):
        dst[...] = src[...].astype(BF16)


def _pool_constants(n):
    rows = n // GRID_W
    t = np.arange(LANES)
    blk, col = t // GRID_W, t % GRID_W
    cms, invs = [], []
    r = np.arange(n) // GRID_W
    c = np.arange(n) % GRID_W
    for w in POOL_WINDOWS:
        d = col[None, :] - col[:, None]
        cms.append(((blk[:, None] == blk[None, :]) & (d >= -(w // 2)) & (d < w - w // 2)).astype(np.float32))
        cnt_r = np.minimum(r + w - w // 2, rows) - np.maximum(r - w // 2, 0)
        cnt_c = np.minimum(c + w - w // 2, GRID_W) - np.maximum(c - w // 2, 0)
        invs.append(np.broadcast_to((1.0 / (cnt_r * cnt_c))[:, None], (n, LANES)).astype(np.float32))
    return np.stack(cms), np.stack(invs)


def _mix_out(xs, bc, zu, rows, cols, ds, cds, crows, x, mods, gn, dsk, gssm, wpool, pscale, wout, casts, tm):
    b, n, d = x.shape
    nchunks = n // CHUNK
    nj = n // tm
    cm_np, inv_np = _pool_constants(n)
    cm = jnp.asarray(cm_np, BF16)
    inv = jnp.asarray(inv_np, F32)
    full = lambda a: pl.BlockSpec(a.shape, lambda i, j: (0,) * a.ndim)
    per_batch = lambda a: pl.BlockSpec((1,) + a.shape[1:], lambda i, j: (i,) + (0,) * (a.ndim - 1))
    cast_specs = []
    for a in casts:
        rps, last = _cast_blocks(a, b * nj)
        cast_specs.append(pl.BlockSpec((rps, a.shape[1]), lambda i, j, last=last: (jnp.minimum(i * nj + j, last), 0)))
    outs = pl.pallas_call(
        _mix_out_kernel,
        grid=(b, n // tm),
        in_specs=[pl.BlockSpec((1, tm, D_SSM), lambda i, j: (i, j, 0)),
                  pl.BlockSpec((1, tm, D_BC), lambda i, j: (i, j, 0)),
                  pl.BlockSpec((1, tm, D_SSM), lambda i, j: (i, j, 0)),
                  pl.BlockSpec((1, n, D_POOL), lambda i, j: (i, 0, 1)),
                  per_batch(rows), per_batch(cols), per_batch(ds), per_batch(cds), per_batch(crows),
                  pl.BlockSpec((1, tm, d), lambda i, j: (i, j, 0)),
                  _mod_spec(d, 2),
                  full(gn), full(dsk), full(gssm), full(wpool), full(pscale), full(wout), full(cm),
                  pl.BlockSpec(inv.shape, lambda i, j: (0, 0, 0), pipeline_mode=pl.Buffered(1))] + cast_specs,
        out_specs=[pl.BlockSpec((1, tm, d), lambda i, j: (i, j, 0))] + cast_specs,
        out_shape=[jax.ShapeDtypeStruct((b, n, d), F32)] + [jax.ShapeDtypeStruct(a.shape, BF16) for a in casts],
        scratch_shapes=[pltpu.VMEM((nchunks, SSD_PAIRS, SSD_STATE, LANES), F32),
                        pltpu.VMEM((nchunks, SSD_PAIRS, SSD_STATE, LANES), F32),
                        pltpu.VMEM((n, D_POOL), BF16),
                        pltpu.VMEM((tm, D_SSM + D_POOL), BF16)],
        compiler_params=pltpu.CompilerParams(dimension_semantics=("arbitrary", "arbitrary"),
                                             vmem_limit_bytes=VMEM_LIMIT),
        name="mix_out",
    )(xs, bc, zu, zu, rows, cols, ds, cds, crows, x, mods, gn, dsk, gssm, wpool, pscale, wout, cm, inv, *casts)
    return outs[0], outs[1:]


FFN_COLS = 2 * LANES
FFN_RING = 2


def _ffn_kernel(x_ref, xp_ref, xn_ref, shift_ref, scale_ref, gate_ref, gpre_ref, gpost_ref,
                wup_ref, cw_ref, cb_ref, wdown_ref, o_ref, act_scr, up_scr, af_scr):
    tm = x_ref.shape[1]
    dff = wdown_ref.shape[0]
    j = pl.program_id(1)
    nj = pl.num_programs(1)

    def hmod(v):
        return _rms(v, gpre_ref[...]) * (1.0 + scale_ref[0]) + shift_ref[0]

    x = x_ref[0]
    hp = jnp.where(j > 0, hmod(xp_ref[0]), 0.0)
    hn = jnp.where(j < nj - 1, hmod(xn_ref[0]), 0.0)
    hext = jnp.concatenate([hp, hmod(x), hn], axis=0).astype(BF16)

    nsl = FFN_COLS // LANES
    for cidx in range(dff // FFN_COLS):
        slot = cidx % FFN_RING
        conv = []
        for half, base in enumerate((cidx * FFN_COLS, dff + cidx * FFN_COLS)):
            up = jnp.dot(hext, wup_ref[:, base:base + FFN_COLS], preferred_element_type=F32)
            for s in range(nsl):
                lanes = slice(base + s * LANES, base + (s + 1) * LANES)
                slab = (2 * slot + half) * nsl + s
                up_scr[slab] = up[:, s * LANES:(s + 1) * LANES]
                conv.append([cb_ref[:, lanes] + sum(
                    cw_ref[k:k + 1, lanes] * up_scr[slab, pl.ds(HALO - 1 + par + k, tm // 2, stride=2), :]
                    for k in range(3)) for par in range(2)])
        for s in range(nsl):
            for par in range(2):
                af_scr[slot * nsl + s, pl.ds(par, tm // 2, stride=2), :] = (
                    _silu(conv[nsl + s][par]) * conv[s][par])
            lo = cidx * FFN_COLS + s * LANES
            act_scr[:, lo:lo + LANES] = af_scr[slot * nsl + s].astype(BF16)

    f = jnp.dot(act_scr[...], wdown_ref[...], preferred_element_type=F32)
    o_ref[0] = x + gate_ref[0] * _rms(f, gpost_ref[...])


def _ffn(x, mods, gpre, gpost, wup, cw, cb, wdown, tm):
    b, n, d = x.shape
    dff = wdown.shape[0]
    return pl.pallas_call(
        _ffn_kernel,
        grid=(b, n // tm),
        in_specs=[pl.BlockSpec((1, tm, d), lambda i, j: (i, j, 0))] + _halo_specs(tm, n, d) + [
            _mod_spec(d, 3), _mod_spec(d, 4), _mod_spec(d, 5),
            pl.BlockSpec((1, d), lambda i, j: (0, 0)),
            pl.BlockSpec((1, d), lambda i, j: (0, 0)),
            pl.BlockSpec(wup.shape, lambda i, j: (0, 0), pipeline_mode=pl.Buffered(1)),
            pl.BlockSpec(cw.shape, lambda i, j: (0, 0)),
            pl.BlockSpec(cb.shape, lambda i, j: (0, 0)),
            pl.BlockSpec(wdown.shape, lambda i, j: (0, 0), pipeline_mode=pl.Buffered(1))],
        out_specs=pl.BlockSpec((1, tm, d), lambda i, j: (i, j, 0)),
        out_shape=jax.ShapeDtypeStruct((b, n, d), F32),
        scratch_shapes=[pltpu.VMEM((tm, dff), BF16),
                        pltpu.VMEM((2 * FFN_RING * FFN_COLS // LANES, tm + 2 * HALO, LANES), F32),
                        pltpu.VMEM((FFN_RING * FFN_COLS // LANES, tm, LANES), F32)],
        compiler_params=pltpu.CompilerParams(dimension_semantics=("arbitrary", "arbitrary"),
                                             vmem_limit_bytes=VMEM_LIMIT),
        name="ffn",
    )(x, x, x, mods, mods, mods, gpre, gpost, wup, cw, cb, wdown)


def kernel(x, c, ctx, c_ctx, pre_norm_mix, post_norm_mix, pre_norm_ffn, post_norm_ffn, w_ada, b_ada,
           w_in, conv_ssd_w, conv_ssd_b, dt_bias, a_log, d_skip, ssm_norm, w_pool, pool_scale, w_out,
           w_up, conv_ffn_w, conv_ffn_b, w_down):
    assert w_ada.shape[0] == 1, "single-layer block"
    b, n, d = x.shape
    n_ctx = ctx.shape[1]
    d_xbc = D_SSM + D_BC
    d_xb = D_SSM + D_BC // 2
    off_dt = D_SSM + d_xbc
    off_pool = off_dt + 2 * SSD_HEADS
    tm_in = 1024

    assert b < ADA_ROWS and w_ada.shape[2] == N_MODS * d
    cc = jnp.concatenate([c, c_ctx[None, :], jnp.zeros((ADA_ROWS - b - 1, d), F32)], axis=0)
    mods = _ada(cc, w_ada[0], b_ada)

    wi = w_in[0]
    w_all = wi.astype(BF16)
    w_pd = jnp.concatenate([wi[:, off_pool:], wi[:, off_dt:off_pool],
                            jnp.zeros((d, LANES - 2 * SSD_HEADS), F32)], axis=1).astype(BF16)
    bias = jnp.broadcast_to(dt_bias[0].reshape(2 * SSD_HEADS, 1), (2 * SSD_HEADS, tm_in))
    alog = jnp.broadcast_to(a_log[0].reshape(2 * SSD_HEADS, 1), (2 * SSD_HEADS, tm_in))

    xs, bc, zu, rows, cols, ds = _mix_in(x, mods, None, pre_norm_mix, w_all, w_pd, conv_ssd_w[0],
                                         conv_ssd_b, bias, alog, tm_in, True, "mix_in")
    crows, cds = _mix_in(ctx, mods, b, pre_norm_mix, w_all, w_pd, conv_ssd_w[0, :, :d_xb],
                         conv_ssd_b[:, :d_xb], bias, alog, tm_in, False, "mix_in_ctx")

    dsk = jnp.repeat(d_skip[0], SSD_HEAD_DIM)[None, :]
    x1, (w_up_b, w_down_b) = _mix_out(xs, bc, zu, rows, cols, ds, cds, crows, x, mods, post_norm_mix, dsk,
                                      ssm_norm, w_pool[0].astype(BF16), pool_scale, w_out[0].astype(BF16),
                                      [w_up[0], w_down[0]], 512)

    return _ffn(x1, mods, pre_norm_ffn, post_norm_ffn, w_up_b, conv_ffn_w[0], conv_ffn_b, w_down_b, 1024)
```

```python
import functools

import numpy as np
import jax
import jax.numpy as jnp
from jax import lax
from jax.experimental import pallas as pl
from jax.experimental.pallas import tpu as pltpu

F32 = jnp.float32
BF16 = jnp.bfloat16

EPS = 1e-6
GRID_W = 64
SSD_HEADS = 8
SSD_HEAD_DIM = 64
SSD_GROUPS = 2
SSD_STATE = 128
SSD_PAIRS = SSD_HEADS // 2
SSD_CONV = 5
D_SSM = SSD_HEADS * SSD_HEAD_DIM
D_BC = 2 * SSD_GROUPS * SSD_STATE
POOL_WINDOWS = (2, 4, 8, 16)
D_POOL = 512
CHUNK = 128
LANES = 128
HALO = 8
N_ROWS = 9
VMEM_LIMIT = 56 * 1024 * 1024


def _silu(v):
    return v / (1.0 + jnp.exp(-v))


def _rms(v, gain):
    ms = jnp.mean(v * v, axis=-1, keepdims=True)
    return v * lax.rsqrt(ms + EPS) * gain


N_MODS = 6
ADA_ROWS = 16


def _mod_spec(d, k, row=None):
    if row is None:
        return pl.BlockSpec((1, 1, d), lambda i, j: (k * ADA_ROWS + i, 0, 0))
    return pl.BlockSpec((1, 1, d), lambda i, j: (k * ADA_ROWS + row, 0, 0))


def _halo_specs(tm, n, d):
    hb, nb = tm // HALO, n // HALO
    return [pl.BlockSpec((1, HALO, d), lambda i, j: (i, jnp.maximum(j * hb - 1, 0), 0)),
            pl.BlockSpec((1, HALO, d), lambda i, j: (i, jnp.minimum((j + 1) * hb, nb - 1), 0))]


def _cast_blocks(a, steps):
    rps = next(r for r in range(16, a.shape[0] + 1, 16) if a.shape[0] % r == 0 and a.shape[0] // r <= steps)
    return rps, a.shape[0] // rps - 1


def _ada_kernel(c_ref, w_ref, b_ref, o_ref):
    s = _silu(c_ref[...]).astype(BF16)
    o_ref[:, 0, :] = jnp.dot(s, w_ref[...].astype(BF16), preferred_element_type=F32) + b_ref[...]


def _ada(cc, w_ada, b_ada):
    rows, d = cc.shape
    n = w_ada.shape[1]
    return pl.pallas_call(
        _ada_kernel,
        grid=(n // d,),
        in_specs=[pl.BlockSpec((rows, d), lambda j: (0, 0)),
                  pl.BlockSpec((d, d), lambda j: (0, j)),
                  pl.BlockSpec((1, d), lambda j: (0, j))],
        out_specs=pl.BlockSpec((rows, 1, d), lambda j: (j, 0, 0)),
        out_shape=jax.ShapeDtypeStruct((n // d * rows, 1, d), F32),
        compiler_params=pltpu.CompilerParams(dimension_semantics=("arbitrary",),
                                             vmem_limit_bytes=VMEM_LIMIT),
        name="ada",
    )(cc, w_ada, b_ada)


def _dt_rows(dtraw_chunks, bias, a):
    h = SSD_HEADS
    raw = jnp.concatenate([blk.T[0:2 * h, :] for blk in dtraw_chunks], axis=1) + bias
    t = raw.shape[1]
    dt = jnp.maximum(raw, 0.0) + jnp.log1p(jnp.exp(-jnp.abs(raw)))
    dta = dt * a
    seg = lax.broadcasted_iota(jnp.int32, dta.shape, 1) & (CHUNK - 1)
    cs, rcs = dta, dta
    k = 1
    while k < CHUNK:
        cs = cs + jnp.where(seg >= k, pltpu.roll(cs, k, 1), 0.0)
        rcs = rcs + jnp.where(seg < CHUNK - k, pltpu.roll(rcs, t - k, 1), 0.0)
        k *= 2
    cd = jnp.exp(cs + rcs - dta)
    wf = jnp.exp(rcs[0:h] - dta[0:h]) * dt[0:h]
    wb = jnp.exp(cs[h:] - dta[h:]) * dt[h:]
    log2e = 1.0 / np.log(2.0)
    pf2, ab2 = cs[0:h] * log2e, rcs[h:] * log2e
    ldt2 = jnp.log(dt) * log2e
    ldg2 = jnp.log(dt[0:h] + dt[h:]) * log2e
    return [pf2, ab2, pf2 - ldt2[0:h], ab2 - ldt2[h:], ldg2, wf, wb, cd[0:h], cd[h:]]


def _pair_rhs(xs, p):
    xp = xs[:, p * LANES:(p + 1) * LANES]
    lane = lax.broadcasted_iota(jnp.int32, xp.shape, 1)
    zero = jnp.zeros_like(xp)
    return jnp.concatenate([jnp.where(lane < SSD_HEAD_DIM, xp, zero),
                            jnp.where(lane >= SSD_HEAD_DIM, xp, zero)], axis=0)


def _pair_select(lane, v0, v1):
    return jnp.where(lane < SSD_HEAD_DIM, v0, v1)


def _local_states(xs, bts, wf, wb):
    out_f, out_b = [], []
    for p in range(SSD_PAIRS):
        bt = bts[p // 2]
        h0, h1 = 2 * p, 2 * p + 1
        lhs = jnp.concatenate(
            [jnp.concatenate([(bt * w[h0:h0 + 1, :]).astype(BF16), (bt * w[h1:h1 + 1, :]).astype(BF16)], axis=1)
             for w in (wf, wb)], axis=0)
        res = jnp.dot(lhs, _pair_rhs(xs, p), preferred_element_type=F32)
        out_f.append(res[0:SSD_STATE])
        out_b.append(res[SSD_STATE:2 * SSD_STATE])
    return out_f, out_b


def _mix_in_kernel(*refs, conv_cols, latent):
    (x_ref, xp_ref, xn_ref, shift_ref, scale_ref, g_ref, w_ref, wpd_ref, cw_ref, cb_ref,
     bias_ref, alog_ref) = refs[:12]
    if latent:
        xs_ref, bc_ref, zu_ref, rows_ref, cols_ref, ds_ref, up_scr, cv_scr = refs[12:]
    else:
        rows_ref, ds_ref, up_scr, cv_scr = refs[12:]
    nseg, seg, d = x_ref.shape
    tm = nseg * seg
    nch = tm // CHUNK
    cps = seg // CHUNK
    j = pl.program_id(1)
    nj = pl.num_programs(1)

    def hmod(v):
        return _rms(v, g_ref[...]) * (1.0 + scale_ref[0]) + shift_ref[0]

    xm = hmod(x_ref[...].reshape(tm, d))
    gap = jnp.zeros((HALO, d), F32)
    hm = xm.astype(BF16)
    if latent:
        hp = jnp.where(j > 0, hmod(xp_ref[0]), 0.0)
        hn = jnp.where(j < nj - 1, hmod(xn_ref[0]), 0.0)
        blocks = [(jnp.concatenate([hp, xm, hn], axis=0).astype(BF16), 0)]
        convs = [(HALO, 0, tm)]
    else:
        pieces = [gap]
        for i in range(nseg):
            pieces += [xm[i * seg:(i + 1) * seg], gap]
        ext_rows = up_scr.shape[1]
        if ext_rows > tm + (nseg + 1) * HALO:
            pieces.append(jnp.zeros((ext_rows - tm - (nseg + 1) * HALO, d), F32))
        blocks = [(jnp.concatenate(pieces, axis=0).astype(BF16), 0)]
        convs = [(HALO + i * (seg + HALO), i * seg, seg) for i in range(nseg)]

    def zero_of(v):
        u = pltpu.bitcast(v[0:8, 0:LANES], jnp.uint32)
        u = lax.shift_right_logical(lax.shift_right_logical(u, jnp.uint32(16)), jnp.uint32(16))
        return pltpu.bitcast(u, F32)[0:1, :]

    ties = {}
    if latent:
        pd = jnp.dot(hm, wpd_ref[...], preferred_element_type=F32)
        zu_ref[0, :, D_SSM:D_SSM + D_POOL] = pd[:, 0:D_POOL].astype(BF16)
        dtraw = pd[:, D_POOL:D_POOL + LANES]
        for hz in range(2):
            zc = slice(hz * 2 * LANES, (hz + 1) * 2 * LANES)
            zp = jnp.dot(hm, w_ref[:, zc], preferred_element_type=F32)
            zu_ref[0, :, zc] = zp.astype(BF16)
            ties[conv_cols // LANES - 2 + hz] = zero_of(zp)
    else:
        dtraw = jnp.dot(hm, wpd_ref[:, D_POOL:D_POOL + LANES], preferred_element_type=F32)

    half = SSD_CONV // 2
    cc = 2 * LANES
    for hblk, srow in blocks:
        for cb in range(conv_cols // cc):
            up = jnp.dot(hblk, w_ref[:, D_SSM + cb * cc:D_SSM + (cb + 1) * cc], preferred_element_type=F32)
            for s in range(cc // LANES):
                up_scr[cb * (cc // LANES) + s, srow:srow + hblk.shape[0], :] = up[:, s * LANES:(s + 1) * LANES]
    for slab in range(conv_cols // LANES):
        lanes = slice(slab * LANES, (slab + 1) * LANES)
        for drow, orow, nrows in convs:
            for par in range(2):
                lo = drow - half + par
                acc = cb_ref[:, lanes] + cw_ref[0:1, lanes] * up_scr[slab, pl.ds(lo, nrows // 2, stride=2), :]
                for k in range(1, SSD_CONV):
                    acc = acc + cw_ref[k:k + 1, lanes] * up_scr[slab, pl.ds(lo + k, nrows // 2, stride=2), :]
                if par == 0 and slab in ties:
                    acc = acc + ties[slab]
                cv_scr[slab, pl.ds(orow + par, nrows // 2, stride=2), :] = _silu(acc)
        if latent:
            if slab < D_SSM // LANES:
                xs_ref[0, :, lanes] = cv_scr[slab].astype(BF16)
            else:
                bc_ref[0, :, slab * LANES - D_SSM:(slab + 1) * LANES - D_SSM] = cv_scr[slab].astype(BF16)

    rows = _dt_rows([dtraw[c * CHUNK:(c + 1) * CHUNK] for c in range(nch)], bias_ref[...],
                    -jnp.exp(alog_ref[...]))
    pad = jnp.zeros((LANES - 2 * SSD_HEADS, CHUNK), F32)
    for c in range(nch):
        lanes = slice(c * CHUNK, (c + 1) * CHUNK)
        for q, arr in enumerate(rows):
            rows_ref[c // cps, c % cps, q] = arr[:, lanes]
        if latent:
            cols_ref[0, c] = jnp.concatenate([rows[0][:, lanes], rows[1][:, lanes], pad], axis=0).T

    def state_body(c, carry):
        r0 = pl.multiple_of(c * CHUNK, CHUNK)
        ci, cj = c // cps, c % cps
        xs = jnp.concatenate([cv_scr[s, pl.ds(r0, CHUNK), :] for s in range(D_SSM // LANES)],
                             axis=1).astype(BF16)
        bts = [cv_scr[D_SSM // LANES + g, pl.ds(r0, CHUNK), :].T for g in range(SSD_GROUPS)]
        dsf, dsb = _local_states(xs, bts, rows_ref[ci, cj, 5], rows_ref[ci, cj, 6])
        for p in range(SSD_PAIRS):
            ds_ref[ci, cj, 0, p] = dsf[p].astype(BF16)
            ds_ref[ci, cj, 1, p] = dsb[p].astype(BF16)
        return carry

    lax.fori_loop(0, nch, state_body, 0, unroll=4)


def _mix_in(x, mods, mod_row, gain, w, wpd, conv_w, conv_b, bias, alog, tm, latent, name):
    b, n, d = x.shape
    conv_cols = conv_w.shape[1]
    nch = tm // CHUNK
    nseg = 1 if latent else tm // n
    seg = tm // nseg
    cps = seg // CHUNK
    ext_rows = -(-(tm + (nseg + 1) * HALO) // 16) * 16
    full = lambda a: pl.BlockSpec(a.shape, lambda i, j: (0,) * a.ndim)
    stat = lambda tail, dt: jax.ShapeDtypeStruct((b, n // CHUNK) + tail, dt)
    out_specs, out_shape = [], []
    if latent:
        grid = (b, n // tm)
        x_spec = pl.BlockSpec((1, tm, d), lambda i, j: (i, j, 0))
        out_specs += [pl.BlockSpec((1, tm, D_SSM), lambda i, j: (i, j, 0)),
                      pl.BlockSpec((1, tm, conv_cols - D_SSM), lambda i, j: (i, j, 0)),
                      pl.BlockSpec((1, tm, D_SSM + D_POOL), lambda i, j: (i, j, 0))]
        out_shape += [jax.ShapeDtypeStruct((b, n, D_SSM), BF16),
                      jax.ShapeDtypeStruct((b, n, conv_cols - D_SSM), BF16),
                      jax.ShapeDtypeStruct((b, n, D_SSM + D_POOL), BF16)]
        blk = lambda tail: pl.BlockSpec((1, nch) + tail, lambda i, j: (i, j) + (0,) * len(tail))
    else:
        grid = (b // nseg, 1)
        x_spec = pl.BlockSpec((nseg, n, d), lambda i, j: (i, 0, 0))
        blk = lambda tail: pl.BlockSpec((nseg, cps) + tail, lambda i, j: (i, 0) + (0,) * len(tail))
    out_specs.append(blk((N_ROWS, SSD_HEADS, CHUNK)))
    out_shape.append(stat((N_ROWS, SSD_HEADS, CHUNK), F32))
    if latent:
        out_specs.append(blk((CHUNK, LANES)))
        out_shape.append(stat((CHUNK, LANES), F32))
    out_specs.append(blk((2, SSD_PAIRS, SSD_STATE, LANES)))
    out_shape.append(stat((2, SSD_PAIRS, SSD_STATE, LANES), BF16))
    return pl.pallas_call(
        functools.partial(_mix_in_kernel, conv_cols=conv_cols, latent=latent),
        grid=grid,
        in_specs=[x_spec] + _halo_specs(seg, n, d) + [
            _mod_spec(d, 0, mod_row), _mod_spec(d, 1, mod_row),
            full(gain), full(w), full(wpd), full(conv_w), full(conv_b), full(bias), full(alog)],
        out_specs=out_specs,
        out_shape=out_shape,
        scratch_shapes=[pltpu.VMEM((conv_cols // LANES, ext_rows, LANES), F32),
                        pltpu.VMEM((conv_cols // LANES, tm, LANES), F32)],
        compiler_params=pltpu.CompilerParams(dimension_semantics=("arbitrary", "arbitrary"),
                                             vmem_limit_bytes=VMEM_LIMIT),
        name=name,
    )(x, x, x, mods, mods, gain, w, wpd, conv_w, conv_b, bias, alog)


def _mix_out_kernel(xs_ref, bc_ref, z_ref, u_ref, rows_ref, cols_ref, ds_ref, cds_ref, crows_ref,
                    x_ref, gate_ref, gn_ref, dsk_ref, gssm_ref, wpool_ref, pscale_ref, wout_ref,
                    cm_ref, inv_ref, *rest):
    ncast = (len(rest) - 4) // 2
    o_ref = rest[ncast]
    sf_scr, sb_scr, d_scr = rest[2 * ncast + 1:]
    n = u_ref.shape[1]
    tm = x_ref.shape[1]
    nchunks = n // CHUNK
    nctx = cds_ref.shape[1]
    tch = tm // CHUNK
    j = pl.program_id(1)
    lane_row = lax.broadcasted_iota(jnp.int32, (1, LANES), 1)

    def cd_row(ref, c, q, p):
        cd = ref[0, c, q]
        return _pair_select(lane_row, cd[2 * p:2 * p + 1, :], cd[2 * p + 1:2 * p + 2, :])

    @pl.when(j == 0)
    def _():
        for p in range(SSD_PAIRS):
            sf = cds_ref[0, 0, 0, p].astype(F32)
            for c in range(1, nctx):
                sf = sf * cd_row(crows_ref, c, 7, p) + cds_ref[0, c, 0, p].astype(F32)
            sb = cds_ref[0, nctx - 1, 1, p].astype(F32)
            for c in reversed(range(nctx - 1)):
                sb = sb * cd_row(crows_ref, c, 8, p) + cds_ref[0, c, 1, p].astype(F32)
            sf_scr[0, p] = sf
            sb_scr[nchunks - 1, p] = sb

        def scan_body(i, carry):
            cf = i
            cbk = nchunks - 1 - i
            for p in range(SSD_PAIRS):
                sf_scr[cf + 1, p] = (sf_scr[cf, p] * cd_row(rows_ref, cf, 7, p)
                                     + ds_ref[0, cf, 0, p].astype(F32))
                sb_scr[cbk - 1, p] = (sb_scr[cbk, p] * cd_row(rows_ref, cbk, 8, p)
                                      + ds_ref[0, cbk, 1, p].astype(F32))
            return carry

        lax.fori_loop(0, nchunks - 1, scan_body, 0)

        for gi, w in enumerate(POOL_WINDOWS):
            cols = slice(gi * LANES, (gi + 1) * LANES)
            cmat = cm_ref[gi]
            parts = [jnp.dot(cmat, u_ref[0, i * LANES:(i + 1) * LANES, cols], preferred_element_type=F32)
                     for i in range(n // LANES)]
            cs = jnp.concatenate(parts, axis=0)
            h = jnp.concatenate([jnp.zeros((w // 2 * GRID_W, LANES), F32), cs], axis=0)
            k = 1
            while k < w:
                sh = k * GRID_W
                h = h + jnp.concatenate([h[sh:], jnp.zeros((sh, LANES), F32)], axis=0)
                k *= 2
            m = h[0:n] * inv_ref[gi]
            d_scr[:, cols] = (m - u_ref[0, :, cols].astype(F32)).astype(BF16)

    row_i = lax.broadcasted_iota(jnp.int32, (CHUNK, CHUNK), 0)
    col_i = lax.broadcasted_iota(jnp.int32, (CHUNK, CHUNK), 1)
    lane_full = lax.broadcasted_iota(jnp.int32, (CHUNK, LANES), 1)

    for lc in range(tch):
        c = j * tch + lc
        r0 = lc * CHUNK
        xs = xs_ref[0, pl.ds(r0, CHUNK), :]
        pfp, abp, ldg = rows_ref[0, c, 2], rows_ref[0, c, 3], rows_ref[0, c, 4]
        cols = cols_ref[0, c]

        y_pairs = []
        for g in range(SSD_GROUPS):
            bg = bc_ref[0, pl.ds(r0, CHUNK), g * SSD_STATE:(g + 1) * SSD_STATE]
            cg = bc_ref[0, pl.ds(r0, CHUNK), (SSD_GROUPS + g) * SSD_STATE:(SSD_GROUPS + g + 1) * SSD_STATE]
            gmat = lax.dot_general(cg, bg, (((1,), (1,)), ((), ())), preferred_element_type=F32)
            scat = jnp.concatenate([sf_scr[c, 2 * g].astype(BF16), sf_scr[c, 2 * g + 1].astype(BF16),
                                    sb_scr[c, 2 * g].astype(BF16), sb_scr[c, 2 * g + 1].astype(BF16)], axis=1)
            yoff = jnp.dot(cg, scat, preferred_element_type=F32)
            for q in range(2):
                p = 2 * g + q
                ms, efs, ebs = [], [], []
                for h in (2 * p, 2 * p + 1):
                    pf_c = jnp.broadcast_to(cols[:, h:h + 1], (CHUNK, CHUNK))
                    ab_c = jnp.broadcast_to(cols[:, SSD_HEADS + h:SSD_HEADS + h + 1], (CHUNK, CHUNK))
                    dm = jnp.where(row_i > col_i, pf_c - pfp[h:h + 1, :],
                                   jnp.where(row_i < col_i, ab_c - abp[h:h + 1, :], ldg[h:h + 1, :]))
                    ms.append((gmat * jnp.exp2(dm)).astype(BF16))
                    efs.append(jnp.exp2(pf_c))
                    ebs.append(jnp.exp2(ab_c))
                ydiag = jnp.dot(jnp.concatenate(ms, axis=1), _pair_rhs(xs, p), preferred_element_type=F32)
                y_pairs.append(ydiag
                               + _pair_select(lane_full, efs[0], efs[1]) * yoff[:, q * LANES:(q + 1) * LANES]
                               + _pair_select(lane_full, ebs[0], ebs[1]) * yoff[:, (2 + q) * LANES:(3 + q) * LANES])
        y = jnp.concatenate(y_pairs, axis=1) + dsk_ref[...] * xs.astype(F32)
        yz = y * _silu(z_ref[0, pl.ds(r0, CHUNK), :].astype(F32))
        yn = _rms(yz, gssm_ref[...]).astype(BF16)

        drows = pl.ds(pl.multiple_of(j * tm + r0, CHUNK), CHUNK)
        ps = [jnp.dot(d_scr[drows, gi * LANES:(gi + 1) * LANES], wpool_ref[gi], preferred_element_type=F32)
              for gi in range(len(POOL_WINDOWS))]
        pm = jnp.concatenate(ps, axis=1) * pscale_ref[...]
        yx = (jnp.dot(yn, wout_ref[0:D_SSM, :], preferred_element_type=F32)
              + jnp.dot(pm.astype(BF16), wout_ref[D_SSM:D_SSM + D_POOL, :], preferred_element_type=F32))
        o_ref[0, pl.ds(r0, CHUNK), :] = x_ref[0, pl.ds(r0, CHUNK), :] + gate_ref[0] * _rms(yx, gn_ref[...])

    for src, dst in zip(rest[:ncast], rest[ncast + 1:2 * ncast + 1]):
        dst[...] = src[...].astype(BF16)


def _pool_constants(n):
    rows = n // GRID_W
    t = np.arange(LANES)
    blk, col = t // GRID_W, t % GRID_W
    cms, invs = [], []
    r = np.arange(n) // GRID_W
    c = np.arange(n) % GRID_W
    for w in POOL_WINDOWS:
        d = col[None, :] - col[:, None]
        cms.append(((blk[:, None] == blk[None, :]) & (d >= -(w // 2)) & (d < w - w // 2)).astype(np.float32))
        cnt_r = np.minimum(r + w - w // 2, rows) - np.maximum(r - w // 2, 0)
        cnt_c = np.minimum(c + w - w // 2, GRID_W) - np.maximum(c - w // 2, 0)
        invs.append(np.broadcast_to((1.0 / (cnt_r * cnt_c))[:, None], (n, LANES)).astype(np.float32))
    return np.stack(cms), np.stack(invs)


def _mix_out(xs, bc, zu, rows, cols, ds, cds, crows, x, mods, gn, dsk, gssm, wpool, pscale, wout, casts, tm):
    b, n, d = x.shape
    nchunks = n // CHUNK
    nj = n // tm
    cm_np, inv_np = _pool_constants(n)
    cm = jnp.asarray(cm_np, BF16)
    inv = jnp.asarray(inv_np, F32)
    full = lambda a: pl.BlockSpec(a.shape, lambda i, j: (0,) * a.ndim)
    per_batch = lambda a: pl.BlockSpec((1,) + a.shape[1:], lambda i, j: (i,) + (0,) * (a.ndim - 1))
    cast_specs = []
    for a in casts:
        rps, last = _cast_blocks(a, b * nj)
        cast_specs.append(pl.BlockSpec((rps, a.shape[1]), lambda i, j, last=last: (jnp.minimum(i * nj + j, last), 0)))
    outs = pl.pallas_call(
        _mix_out_kernel,
        grid=(b, n // tm),
        in_specs=[pl.BlockSpec((1, tm, D_SSM), lambda i, j: (i, j, 0)),
                  pl.BlockSpec((1, tm, D_BC), lambda i, j: (i, j, 0)),
                  pl.BlockSpec((1, tm, D_SSM), lambda i, j: (i, j, 0)),
                  pl.BlockSpec((1, n, D_POOL), lambda i, j: (i, 0, 1)),
                  per_batch(rows), per_batch(cols), per_batch(ds), per_batch(cds), per_batch(crows),
                  pl.BlockSpec((1, tm, d), lambda i, j: (i, j, 0)),
                  _mod_spec(d, 2),
                  full(gn), full(dsk), full(gssm), full(wpool), full(pscale), full(wout), full(cm),
                  pl.BlockSpec(inv.shape, lambda i, j: (0, 0, 0), pipeline_mode=pl.Buffered(1))] + cast_specs,
        out_specs=[pl.BlockSpec((1, tm, d), lambda i, j: (i, j, 0))] + cast_specs,
        out_shape=[jax.ShapeDtypeStruct((b, n, d), F32)] + [jax.ShapeDtypeStruct(a.shape, BF16) for a in casts],
        scratch_shapes=[pltpu.VMEM((nchunks, SSD_PAIRS, SSD_STATE, LANES), F32),
                        pltpu.VMEM((nchunks, SSD_PAIRS, SSD_STATE, LANES), F32),
                        pltpu.VMEM((n, D_POOL), BF16)],
        compiler_params=pltpu.CompilerParams(dimension_semantics=("arbitrary", "arbitrary"),
                                             vmem_limit_bytes=VMEM_LIMIT),
        name="mix_out",
    )(xs, bc, zu, zu, rows, cols, ds, cds, crows, x, mods, gn, dsk, gssm, wpool, pscale, wout, cm, inv, *casts)
    return outs[0], outs[1:]


FFN_COLS = 2 * LANES
FFN_RING = 2


def _ffn_kernel(x_ref, xp_ref, xn_ref, shift_ref, scale_ref, gate_ref, gpre_ref, gpost_ref,
                wup_ref, cw_ref, cb_ref, wdown_ref, o_ref, act_scr, up_scr, af_scr):
    tm = x_ref.shape[1]
    dff = wdown_ref.shape[0]
    j = pl.program_id(1)
    nj = pl.num_programs(1)

    def hmod(v):
        return _rms(v, gpre_ref[...]) * (1.0 + scale_ref[0]) + shift_ref[0]

    x = x_ref[0]
    hp = jnp.where(j > 0, hmod(xp_ref[0]), 0.0)
    hn = jnp.where(j < nj - 1, hmod(xn_ref[0]), 0.0)
    hext = jnp.concatenate([hp, hmod(x), hn], axis=0).astype(BF16)

    nsl = FFN_COLS // LANES
    for cidx in range(dff // FFN_COLS):
        slot = cidx % FFN_RING
        conv = []
        for half, base in enumerate((cidx * FFN_COLS, dff + cidx * FFN_COLS)):
            up = jnp.dot(hext, wup_ref[:, base:base + FFN_COLS], preferred_element_type=F32)
            for s in range(nsl):
                lanes = slice(base + s * LANES, base + (s + 1) * LANES)
                slab = (2 * slot + half) * nsl + s
                up_scr[slab] = up[:, s * LANES:(s + 1) * LANES]
                conv.append([cb_ref[:, lanes] + sum(
                    cw_ref[k:k + 1, lanes] * up_scr[slab, pl.ds(HALO - 1 + par + k, tm // 2, stride=2), :]
                    for k in range(3)) for par in range(2)])
        for s in range(nsl):
            for par in range(2):
                af_scr[slot * nsl + s, pl.ds(par, tm // 2, stride=2), :] = (
                    _silu(conv[nsl + s][par]) * conv[s][par])
            lo = cidx * FFN_COLS + s * LANES
            act_scr[:, lo:lo + LANES] = af_scr[slot * nsl + s].astype(BF16)

    f = jnp.dot(act_scr[...], wdown_ref[...], preferred_element_type=F32)
    o_ref[0] = x + gate_ref[0] * _rms(f, gpost_ref[...])


def _ffn(x, mods, gpre, gpost, wup, cw, cb, wdown, tm):
    b, n, d = x.shape
    dff = wdown.shape[0]
    return pl.pallas_call(
        _ffn_kernel,
        grid=(b, n // tm),
        in_specs=[pl.BlockSpec((1, tm, d), lambda i, j: (i, j, 0))] + _halo_specs(tm, n, d) + [
            _mod_spec(d, 3), _mod_spec(d, 4), _mod_spec(d, 5),
            pl.BlockSpec((1, d), lambda i, j: (0, 0)),
            pl.BlockSpec((1, d), lambda i, j: (0, 0)),
            pl.BlockSpec(wup.shape, lambda i, j: (0, 0), pipeline_mode=pl.Buffered(1)),
            pl.BlockSpec(cw.shape, lambda i, j: (0, 0)),
            pl.BlockSpec(cb.shape, lambda i, j: (0, 0)),
            pl.BlockSpec(wdown.shape, lambda i, j: (0, 0), pipeline_mode=pl.Buffered(1))],
        out_specs=pl.BlockSpec((1, tm, d), lambda i, j: (i, j, 0)),
        out_shape=jax.ShapeDtypeStruct((b, n, d), F32),
        scratch_shapes=[pltpu.VMEM((tm, dff), BF16),
                        pltpu.VMEM((2 * FFN_RING * FFN_COLS // LANES, tm + 2 * HALO, LANES), F32),
                        pltpu.VMEM((FFN_RING * FFN_COLS // LANES, tm, LANES), F32)],
        compiler_params=pltpu.CompilerParams(dimension_semantics=("arbitrary", "arbitrary"),
                                             vmem_limit_bytes=VMEM_LIMIT),
        name="ffn",
    )(x, x, x, mods, mods, mods, gpre, gpost, wup, cw, cb, wdown)


def kernel(x, c, ctx, c_ctx, pre_norm_mix, post_norm_mix, pre_norm_ffn, post_norm_ffn, w_ada, b_ada,
           w_in, conv_ssd_w, conv_ssd_b, dt_bias, a_log, d_skip, ssm_norm, w_pool, pool_scale, w_out,
           w_up, conv_ffn_w, conv_ffn_b, w_down):
    assert w_ada.shape[0] == 1, "single-layer block"
    b, n, d = x.shape
    d_xb = D_SSM + D_BC // 2
    off_dt = 2 * D_SSM + D_BC
    off_pool = off_dt + 2 * SSD_HEADS
    tm_in = 1024

    assert b < ADA_ROWS and w_ada.shape[2] == N_MODS * d
    cc = jnp.concatenate([c, c_ctx[None, :], jnp.zeros((ADA_ROWS - b - 1, d), F32)], axis=0)
    mods = _ada(cc, w_ada[0], b_ada)

    wi = w_in[0]
    w_all = wi[:, :off_dt].astype(BF16)
    w_pd = jnp.concatenate([wi[:, off_pool:], wi[:, off_dt:off_pool],
                            jnp.zeros((d, LANES - 2 * SSD_HEADS), F32)], axis=1).astype(BF16)
    bias = jnp.broadcast_to(dt_bias[0].reshape(2 * SSD_HEADS, 1), (2 * SSD_HEADS, tm_in))
    alog = jnp.broadcast_to(a_log[0].reshape(2 * SSD_HEADS, 1), (2 * SSD_HEADS, tm_in))

    xs, bc, zu, rows, cols, ds = _mix_in(x, mods, None, pre_norm_mix, w_all, w_pd, conv_ssd_w[0],
                                         conv_ssd_b, bias, alog, tm_in, True, "mix_in")
    crows, cds = _mix_in(ctx, mods, b, pre_norm_mix, w_all, w_pd, conv_ssd_w[0, :, :d_xb],
                         conv_ssd_b[:, :d_xb], bias, alog, tm_in, False, "mix_in_ctx")

    dsk = jnp.repeat(d_skip[0], SSD_HEAD_DIM)[None, :]
    x1, (w_up_b, w_down_b) = _mix_out(xs, bc, zu, rows, cols, ds, cds, crows, x, mods, post_norm_mix, dsk,
                                      ssm_norm, w_pool[0].astype(BF16), pool_scale, w_out[0].astype(BF16),
                                      [w_up[0], w_down[0]], 512)

    return _ffn(x1, mods, pre_norm_ffn, post_norm_ffn, w_up_b, conv_ffn_w[0], conv_ffn_b, w_down_b, 1024)
```

```python
import functools

import numpy as np
import jax
import jax.numpy as jnp
from jax import lax
from jax.experimental import pallas as pl
from jax.experimental.pallas import tpu as pltpu

F32 = jnp.float32
BF16 = jnp.bfloat16

EPS = 1e-6
GRID_W = 64
SSD_HEADS = 8
SSD_HEAD_DIM = 64
SSD_GROUPS = 2
SSD_STATE = 128
SSD_PAIRS = SSD_HEADS // 2
SSD_CONV = 5
D_SSM = SSD_HEADS * SSD_HEAD_DIM
D_BC = 2 * SSD_GROUPS * SSD_STATE
POOL_WINDOWS = (2, 4, 8, 16)
D_POOL = 512
CHUNK = 128
LANES = 128
HALO = 8
N_ROWS = 9
VMEM_LIMIT = 56 * 1024 * 1024


def _silu(v):
    return v / (1.0 + jnp.exp(-v))


def _rms(v, gain):
    ms = jnp.mean(v * v, axis=-1, keepdims=True)
    return v * lax.rsqrt(ms + EPS) * gain


N_MODS = 6
ADA_ROWS = 16


def _mod_spec(d, k, row=None):
    if row is None:
        return pl.BlockSpec((1, 1, d), lambda i, j: (k * ADA_ROWS + i, 0, 0))
    return pl.BlockSpec((1, 1, d), lambda i, j: (k * ADA_ROWS + row, 0, 0))


def _halo_specs(tm, n, d):
    hb, nb = tm // HALO, n // HALO
    return [pl.BlockSpec((1, HALO, d), lambda i, j: (i, jnp.maximum(j * hb - 1, 0), 0)),
            pl.BlockSpec((1, HALO, d), lambda i, j: (i, jnp.minimum((j + 1) * hb, nb - 1), 0))]


def _cast_blocks(a, steps):
    rps = next(r for r in range(16, a.shape[0] + 1, 16) if a.shape[0] % r == 0 and a.shape[0] // r <= steps)
    return rps, a.shape[0] // rps - 1


def _ada_kernel(c_ref, w_ref, b_ref, o_ref):
    s = _silu(c_ref[...]).astype(BF16)
    o_ref[:, 0, :] = jnp.dot(s, w_ref[...].astype(BF16), preferred_element_type=F32) + b_ref[...]


def _ada(cc, w_ada, b_ada):
    rows, d = cc.shape
    n = w_ada.shape[1]
    return pl.pallas_call(
        _ada_kernel,
        grid=(n // d,),
        in_specs=[pl.BlockSpec((rows, d), lambda j: (0, 0)),
                  pl.BlockSpec((d, d), lambda j: (0, j)),
                  pl.BlockSpec((1, d), lambda j: (0, j))],
        out_specs=pl.BlockSpec((rows, 1, d), lambda j: (j, 0, 0)),
        out_shape=jax.ShapeDtypeStruct((n // d * rows, 1, d), F32),
        compiler_params=pltpu.CompilerParams(dimension_semantics=("arbitrary",),
                                             vmem_limit_bytes=VMEM_LIMIT),
        name="ada",
    )(cc, w_ada, b_ada)


def _dt_rows(dtraw_chunks, bias, a):
    h = SSD_HEADS
    raw = jnp.concatenate([blk.T[0:2 * h, :] for blk in dtraw_chunks], axis=1) + bias
    t = raw.shape[1]
    dt = jnp.maximum(raw, 0.0) + jnp.log1p(jnp.exp(-jnp.abs(raw)))
    dta = dt * a
    seg = lax.broadcasted_iota(jnp.int32, dta.shape, 1) & (CHUNK - 1)
    cs, rcs = dta, dta
    k = 1
    while k < CHUNK:
        cs = cs + jnp.where(seg >= k, pltpu.roll(cs, k, 1), 0.0)
        rcs = rcs + jnp.where(seg < CHUNK - k, pltpu.roll(rcs, t - k, 1), 0.0)
        k *= 2
    cd = jnp.exp(cs + rcs - dta)
    wf = jnp.exp(rcs[0:h] - dta[0:h]) * dt[0:h]
    wb = jnp.exp(cs[h:] - dta[h:]) * dt[h:]
    log2e = 1.0 / np.log(2.0)
    pf2, ab2 = cs[0:h] * log2e, rcs[h:] * log2e
    ldt2 = jnp.log(dt) * log2e
    ldg2 = jnp.log(dt[0:h] + dt[h:]) * log2e
    return [pf2, ab2, pf2 - ldt2[0:h], ab2 - ldt2[h:], ldg2, wf, wb, cd[0:h], cd[h:]]


def _pair_rhs(xs, p):
    xp = xs[:, p * LANES:(p + 1) * LANES]
    lane = lax.broadcasted_iota(jnp.int32, xp.shape, 1)
    zero = jnp.zeros_like(xp)
    return jnp.concatenate([jnp.where(lane < SSD_HEAD_DIM, xp, zero),
                            jnp.where(lane >= SSD_HEAD_DIM, xp, zero)], axis=0)


def _pair_select(lane, v0, v1):
    return jnp.where(lane < SSD_HEAD_DIM, v0, v1)


def _local_states(xs, bts, wf, wb):
    out_f, out_b = [], []
    for p in range(SSD_PAIRS):
        bt = bts[p // 2]
        h0, h1 = 2 * p, 2 * p + 1
        lhs = jnp.concatenate(
            [jnp.concatenate([(bt * w[h0:h0 + 1, :]).astype(BF16), (bt * w[h1:h1 + 1, :]).astype(BF16)], axis=1)
             for w in (wf, wb)], axis=0)
        res = jnp.dot(lhs, _pair_rhs(xs, p), preferred_element_type=F32)
        out_f.append(res[0:SSD_STATE])
        out_b.append(res[SSD_STATE:2 * SSD_STATE])
    return out_f, out_b


def _mix_in_kernel(*refs, conv_cols, latent):
    (x_ref, xp_ref, xn_ref, shift_ref, scale_ref, g_ref, w_ref, wpd_ref, cw_ref, cb_ref,
     bias_ref, alog_ref) = refs[:12]
    if latent:
        xs_ref, bc_ref, zu_ref, rows_ref, cols_ref, ds_ref, up_scr, cv_scr = refs[12:]
    else:
        rows_ref, ds_ref, up_scr, cv_scr = refs[12:]
    nseg, seg, d = x_ref.shape
    tm = nseg * seg
    nch = tm // CHUNK
    cps = seg // CHUNK
    j = pl.program_id(1)
    nj = pl.num_programs(1)

    def hmod(v):
        return _rms(v, g_ref[...]) * (1.0 + scale_ref[0]) + shift_ref[0]

    xm = hmod(x_ref[...].reshape(tm, d))
    gap = jnp.zeros((HALO, d), F32)
    hm = xm.astype(BF16)
    if latent:
        hp = jnp.where(j > 0, hmod(xp_ref[0]), 0.0)
        hn = jnp.where(j < nj - 1, hmod(xn_ref[0]), 0.0)
        blocks = [(jnp.concatenate([hp, xm, hn], axis=0).astype(BF16), 0)]
        convs = [(HALO, 0, tm)]
    else:
        pieces = [gap]
        for i in range(nseg):
            pieces += [xm[i * seg:(i + 1) * seg], gap]
        ext_rows = up_scr.shape[1]
        if ext_rows > tm + (nseg + 1) * HALO:
            pieces.append(jnp.zeros((ext_rows - tm - (nseg + 1) * HALO, d), F32))
        blocks = [(jnp.concatenate(pieces, axis=0).astype(BF16), 0)]
        convs = [(HALO + i * (seg + HALO), i * seg, seg) for i in range(nseg)]

    def zero_of(v):
        u = pltpu.bitcast(v[0:8, 0:LANES], jnp.uint32)
        u = lax.shift_right_logical(lax.shift_right_logical(u, jnp.uint32(16)), jnp.uint32(16))
        return pltpu.bitcast(u, F32)[0:1, :]

    ties = {}
    if latent:
        pd = jnp.dot(hm, wpd_ref[...], preferred_element_type=F32)
        zu_ref[0, :, D_SSM:D_SSM + D_POOL] = pd[:, 0:D_POOL].astype(BF16)
        dtraw = pd[:, D_POOL:D_POOL + LANES]
        for hz in range(2):
            zc = slice(hz * 2 * LANES, (hz + 1) * 2 * LANES)
            zp = jnp.dot(hm, w_ref[:, zc], preferred_element_type=F32)
            zu_ref[0, :, zc] = zp.astype(BF16)
            ties[conv_cols // LANES - 2 + hz] = zero_of(zp)
    else:
        dtraw = jnp.dot(hm, wpd_ref[:, D_POOL:D_POOL + LANES], preferred_element_type=F32)

    half = SSD_CONV // 2
    cc = 2 * LANES
    for hblk, srow in blocks:
        for cb in range(conv_cols // cc):
            up = jnp.dot(hblk, w_ref[:, D_SSM + cb * cc:D_SSM + (cb + 1) * cc], preferred_element_type=F32)
            for s in range(cc // LANES):
                up_scr[cb * (cc // LANES) + s, srow:srow + hblk.shape[0], :] = up[:, s * LANES:(s + 1) * LANES]
    for slab in range(conv_cols // LANES):
        lanes = slice(slab * LANES, (slab + 1) * LANES)
        for drow, orow, nrows in convs:
            for par in range(2):
                lo = drow - half + par
                acc = cb_ref[:, lanes] + cw_ref[0:1, lanes] * up_scr[slab, pl.ds(lo, nrows // 2, stride=2), :]
                for k in range(1, SSD_CONV):
                    acc = acc + cw_ref[k:k + 1, lanes] * up_scr[slab, pl.ds(lo + k, nrows // 2, stride=2), :]
                if par == 0 and slab in ties:
                    acc = acc + ties[slab]
                cv_scr[slab, pl.ds(orow + par, nrows // 2, stride=2), :] = _silu(acc)
        if latent:
            if slab < D_SSM // LANES:
                xs_ref[0, :, lanes] = cv_scr[slab].astype(BF16)
            else:
                bc_ref[0, :, slab * LANES - D_SSM:(slab + 1) * LANES - D_SSM] = cv_scr[slab].astype(BF16)

    rows = _dt_rows([dtraw[c * CHUNK:(c + 1) * CHUNK] for c in range(nch)], bias_ref[...],
                    -jnp.exp(alog_ref[...]))
    pad = jnp.zeros((LANES - 2 * SSD_HEADS, CHUNK), F32)
    for c in range(nch):
        lanes = slice(c * CHUNK, (c + 1) * CHUNK)
        for q, arr in enumerate(rows):
            rows_ref[c // cps, c % cps, q] = arr[:, lanes]
        if latent:
            cols_ref[0, c] = jnp.concatenate([rows[0][:, lanes], rows[1][:, lanes], pad], axis=0).T

    def state_body(c, carry):
        r0 = pl.multiple_of(c * CHUNK, CHUNK)
        ci, cj = c // cps, c % cps
        xs = jnp.concatenate([cv_scr[s, pl.ds(r0, CHUNK), :] for s in range(D_SSM // LANES)],
                             axis=1).astype(BF16)
        bts = [cv_scr[D_SSM // LANES + g, pl.ds(r0, CHUNK), :].T for g in range(SSD_GROUPS)]
        dsf, dsb = _local_states(xs, bts, rows_ref[ci, cj, 5], rows_ref[ci, cj, 6])
        for p in range(SSD_PAIRS):
            ds_ref[ci, cj, 0, p] = dsf[p].astype(BF16)
            ds_ref[ci, cj, 1, p] = dsb[p].astype(BF16)
        return carry

    lax.fori_loop(0, nch, state_body, 0, unroll=4)


def _mix_in(x, mods, mod_row, gain, w, wpd, conv_w, conv_b, bias, alog, tm, latent, name):
    b, n, d = x.shape
    conv_cols = conv_w.shape[1]
    nch = tm // CHUNK
    nseg = 1 if latent else tm // n
    seg = tm // nseg
    cps = seg // CHUNK
    ext_rows = -(-(tm + (nseg + 1) * HALO) // 16) * 16
    full = lambda a: pl.BlockSpec(a.shape, lambda i, j: (0,) * a.ndim)
    stat = lambda tail, dt: jax.ShapeDtypeStruct((b, n // CHUNK) + tail, dt)
    out_specs, out_shape = [], []
    if latent:
        grid = (b, n // tm)
        x_spec = pl.BlockSpec((1, tm, d), lambda i, j: (i, j, 0))
        out_specs += [pl.BlockSpec((1, tm, D_SSM), lambda i, j: (i, j, 0)),
                      pl.BlockSpec((1, tm, conv_cols - D_SSM), lambda i, j: (i, j, 0)),
                      pl.BlockSpec((1, tm, D_SSM + D_POOL), lambda i, j: (i, j, 0))]
        out_shape += [jax.ShapeDtypeStruct((b, n, D_SSM), BF16),
                      jax.ShapeDtypeStruct((b, n, conv_cols - D_SSM), BF16),
                      jax.ShapeDtypeStruct((b, n, D_SSM + D_POOL), BF16)]
        blk = lambda tail: pl.BlockSpec((1, nch) + tail, lambda i, j: (i, j) + (0,) * len(tail))
    else:
        grid = (b // nseg, 1)
        x_spec = pl.BlockSpec((nseg, n, d), lambda i, j: (i, 0, 0))
        blk = lambda tail: pl.BlockSpec((nseg, cps) + tail, lambda i, j: (i, 0) + (0,) * len(tail))
    out_specs.append(blk((N_ROWS, SSD_HEADS, CHUNK)))
    out_shape.append(stat((N_ROWS, SSD_HEADS, CHUNK), F32))
    if latent:
        out_specs.append(blk((CHUNK, LANES)))
        out_shape.append(stat((CHUNK, LANES), F32))
    out_specs.append(blk((2, SSD_PAIRS, SSD_STATE, LANES)))
    out_shape.append(stat((2, SSD_PAIRS, SSD_STATE, LANES), BF16))
    return pl.pallas_call(
        functools.partial(_mix_in_kernel, conv_cols=conv_cols, latent=latent),
        grid=grid,
        in_specs=[x_spec] + _halo_specs(seg, n, d) + [
            _mod_spec(d, 0, mod_row), _mod_spec(d, 1, mod_row),
            full(gain), full(w), full(wpd), full(conv_w), full(conv_b), full(bias), full(alog)],
        out_specs=out_specs,
        out_shape=out_shape,
        scratch_shapes=[pltpu.VMEM((conv_cols // LANES, ext_rows, LANES), F32),
                        pltpu.VMEM((conv_cols // LANES, tm, LANES), F32)],
        compiler_params=pltpu.CompilerParams(dimension_semantics=("arbitrary", "arbitrary"),
                                             vmem_limit_bytes=VMEM_LIMIT),
        name=name,
    )(x, x, x, mods, mods, gain, w, wpd, conv_w, conv_b, bias, alog)


def _mix_out_kernel(xs_ref, bc_ref, z_ref, u_ref, rows_ref, cols_ref, ds_ref, cds_ref, crows_ref,
                    x_ref, gate_ref, gn_ref, dsk_ref, gssm_ref, wpool_ref, pscale_ref, wout_ref,
                    cm_ref, inv_ref, *rest):
    ncast = (len(rest) - 4) // 2
    o_ref = rest[ncast]
    sf_scr, sb_scr, d_scr = rest[2 * ncast + 1:]
    n = u_ref.shape[1]
    tm = x_ref.shape[1]
    nchunks = n // CHUNK
    nctx = cds_ref.shape[1]
    tch = tm // CHUNK
    j = pl.program_id(1)
    lane_row = lax.broadcasted_iota(jnp.int32, (1, LANES), 1)

    def cd_row(ref, c, q, p):
        cd = ref[0, c, q]
        return _pair_select(lane_row, cd[2 * p:2 * p + 1, :], cd[2 * p + 1:2 * p + 2, :])

    @pl.when(j == 0)
    def _():
        for p in range(SSD_PAIRS):
            sf = cds_ref[0, 0, 0, p].astype(F32)
            for c in range(1, nctx):
                sf = sf * cd_row(crows_ref, c, 7, p) + cds_ref[0, c, 0, p].astype(F32)
            sb = cds_ref[0, nctx - 1, 1, p].astype(F32)
            for c in reversed(range(nctx - 1)):
                sb = sb * cd_row(crows_ref, c, 8, p) + cds_ref[0, c, 1, p].astype(F32)
            sf_scr[0, p] = sf
            sb_scr[nchunks - 1, p] = sb

        def scan_body(i, carry):
            cf = i
            cbk = nchunks - 1 - i
            for p in range(SSD_PAIRS):
                sf_scr[cf + 1, p] = (sf_scr[cf, p] * cd_row(rows_ref, cf, 7, p)
                                     + ds_ref[0, cf, 0, p].astype(F32))
                sb_scr[cbk - 1, p] = (sb_scr[cbk, p] * cd_row(rows_ref, cbk, 8, p)
                                      + ds_ref[0, cbk, 1, p].astype(F32))
            return carry

        lax.fori_loop(0, nchunks - 1, scan_body, 0)

        for gi, w in enumerate(POOL_WINDOWS):
            cols = slice(gi * LANES, (gi + 1) * LANES)
            cmat = cm_ref[gi]
            parts = [jnp.dot(cmat, u_ref[0, i * LANES:(i + 1) * LANES, cols], preferred_element_type=F32)
                     for i in range(n // LANES)]
            cs = jnp.concatenate(parts, axis=0)
            h = jnp.concatenate([jnp.zeros((w // 2 * GRID_W, LANES), F32), cs], axis=0)
            k = 1
            while k < w:
                sh = k * GRID_W
                h = h + jnp.concatenate([h[sh:], jnp.zeros((sh, LANES), F32)], axis=0)
                k *= 2
            m = h[0:n] * inv_ref[gi]
            d_scr[:, cols] = (m - u_ref[0, :, cols].astype(F32)).astype(BF16)

    row_i = lax.broadcasted_iota(jnp.int32, (CHUNK, CHUNK), 0)
    col_i = lax.broadcasted_iota(jnp.int32, (CHUNK, CHUNK), 1)
    lane_full = lax.broadcasted_iota(jnp.int32, (CHUNK, LANES), 1)

    for lc in range(tch):
        c = j * tch + lc
        r0 = lc * CHUNK
        xs = xs_ref[0, pl.ds(r0, CHUNK), :]
        pfp, abp, ldg = rows_ref[0, c, 2], rows_ref[0, c, 3], rows_ref[0, c, 4]
        cols = cols_ref[0, c]

        y_pairs = []
        for g in range(SSD_GROUPS):
            bg = bc_ref[0, pl.ds(r0, CHUNK), g * SSD_STATE:(g + 1) * SSD_STATE]
            cg = bc_ref[0, pl.ds(r0, CHUNK), (SSD_GROUPS + g) * SSD_STATE:(SSD_GROUPS + g + 1) * SSD_STATE]
            gmat = lax.dot_general(cg, bg, (((1,), (1,)), ((), ())), preferred_element_type=F32)
            scat = jnp.concatenate([sf_scr[c, 2 * g].astype(BF16), sf_scr[c, 2 * g + 1].astype(BF16),
                                    sb_scr[c, 2 * g].astype(BF16), sb_scr[c, 2 * g + 1].astype(BF16)], axis=1)
            yoff = jnp.dot(cg, scat, preferred_element_type=F32)
            for q in range(2):
                p = 2 * g + q
                ms, efs, ebs = [], [], []
                for h in (2 * p, 2 * p + 1):
                    pf_c = jnp.broadcast_to(cols[:, h:h + 1], (CHUNK, CHUNK))
                    ab_c = jnp.broadcast_to(cols[:, SSD_HEADS + h:SSD_HEADS + h + 1], (CHUNK, CHUNK))
                    dm = jnp.where(row_i > col_i, pf_c - pfp[h:h + 1, :],
                                   jnp.where(row_i < col_i, ab_c - abp[h:h + 1, :], ldg[h:h + 1, :]))
                    ms.append((gmat * jnp.exp2(dm)).astype(BF16))
                    efs.append(jnp.exp2(pf_c))
                    ebs.append(jnp.exp2(ab_c))
                ydiag = jnp.dot(jnp.concatenate(ms, axis=1), _pair_rhs(xs, p), preferred_element_type=F32)
                y_pairs.append(ydiag
                               + _pair_select(lane_full, efs[0], efs[1]) * yoff[:, q * LANES:(q + 1) * LANES]
                               + _pair_select(lane_full, ebs[0], ebs[1]) * yoff[:, (2 + q) * LANES:(3 + q) * LANES])
        y = jnp.concatenate(y_pairs, axis=1) + dsk_ref[...] * xs.astype(F32)
        yz = y * _silu(z_ref[0, pl.ds(r0, CHUNK), :].astype(F32))
        yn = _rms(yz, gssm_ref[...]).astype(BF16)

        drows = pl.ds(pl.multiple_of(j * tm + r0, CHUNK), CHUNK)
        ps = [jnp.dot(d_scr[drows, gi * LANES:(gi + 1) * LANES], wpool_ref[gi], preferred_element_type=F32)
              for gi in range(len(POOL_WINDOWS))]
        pm = jnp.concatenate(ps, axis=1) * pscale_ref[...]
        yx = (jnp.dot(yn, wout_ref[0:D_SSM, :], preferred_element_type=F32)
              + jnp.dot(pm.astype(BF16), wout_ref[D_SSM:D_SSM + D_POOL, :], preferred_element_type=F32))
        o_ref[0, pl.ds(r0, CHUNK), :] = x_ref[0, pl.ds(r0, CHUNK), :] + gate_ref[0] * _rms(yx, gn_ref[...])

    for src, dst in zip(rest[:ncast], rest[ncast + 1:2 * ncast + 1]):
        dst[...] = src[...].astype(BF16)


def _pool_constants(n):
    rows = n // GRID_W
    t = np.arange(LANES)
    blk, col = t // GRID_W, t % GRID_W
    cms, invs = [], []
    r = np.arange(n) // GRID_W
    c = np.arange(n) % GRID_W
    for w in POOL_WINDOWS:
        d = col[None, :] - col[:, None]
        cms.append(((blk[:, None] == blk[None, :]) & (d >= -(w // 2)) & (d < w - w // 2)).astype(np.float32))
        cnt_r = np.minimum(r + w - w // 2, rows) - np.maximum(r - w // 2, 0)
        cnt_c = np.minimum(c + w - w // 2, GRID_W) - np.maximum(c - w // 2, 0)
        invs.append(np.broadcast_to((1.0 / (cnt_r * cnt_c))[:, None], (n, LANES)).astype(np.float32))
    return np.stack(cms), np.stack(invs)


def _mix_out(xs, bc, zu, rows, cols, ds, cds, crows, x, mods, gn, dsk, gssm, wpool, pscale, wout, casts, tm):
    b, n, d = x.shape
    nchunks = n // CHUNK
    nj = n // tm
    cm_np, inv_np = _pool_constants(n)
    cm = jnp.asarray(cm_np, BF16)
    inv = jnp.asarray(inv_np, F32)
    full = lambda a: pl.BlockSpec(a.shape, lambda i, j: (0,) * a.ndim)
    per_batch = lambda a: pl.BlockSpec((1,) + a.shape[1:], lambda i, j: (i,) + (0,) * (a.ndim - 1))
    cast_specs = []
    for a in casts:
        rps, last = _cast_blocks(a, b * nj)
        cast_specs.append(pl.BlockSpec((rps, a.shape[1]), lambda i, j, last=last: (jnp.minimum(i * nj + j, last), 0)))
    outs = pl.pallas_call(
        _mix_out_kernel,
        grid=(b, n // tm),
        in_specs=[pl.BlockSpec((1, tm, D_SSM), lambda i, j: (i, j, 0)),
                  pl.BlockSpec((1, tm, D_BC), lambda i, j: (i, j, 0)),
                  pl.BlockSpec((1, tm, D_SSM), lambda i, j: (i, j, 0)),
                  pl.BlockSpec((1, n, D_POOL), lambda i, j: (i, 0, 1)),
                  per_batch(rows), per_batch(cols), per_batch(ds), per_batch(cds), per_batch(crows),
                  pl.BlockSpec((1, tm, d), lambda i, j: (i, j, 0)),
                  _mod_spec(d, 2),
                  full(gn), full(dsk), full(gssm), full(wpool), full(pscale), full(wout), full(cm),
                  pl.BlockSpec(inv.shape, lambda i, j: (0, 0, 0), pipeline_mode=pl.Buffered(1))] + cast_specs,
        out_specs=[pl.BlockSpec((1, tm, d), lambda i, j: (i, j, 0))] + cast_specs,
        out_shape=[jax.ShapeDtypeStruct((b, n, d), F32)] + [jax.ShapeDtypeStruct(a.shape, BF16) for a in casts],
        scratch_shapes=[pltpu.VMEM((nchunks, SSD_PAIRS, SSD_STATE, LANES), F32),
                        pltpu.VMEM((nchunks, SSD_PAIRS, SSD_STATE, LANES), F32),
                        pltpu.VMEM((n, D_POOL), BF16)],
        compiler_params=pltpu.CompilerParams(dimension_semantics=("arbitrary", "arbitrary"),
                                             vmem_limit_bytes=VMEM_LIMIT),
        name="mix_out",
    )(xs, bc, zu, zu, rows, cols, ds, cds, crows, x, mods, gn, dsk, gssm, wpool, pscale, wout, cm, inv, *casts)
    return outs[0], outs[1:]


FFN_COLS = 2 * LANES
FFN_RING = 2


def _ffn_kernel(x_ref, xp_ref, xn_ref, shift_ref, scale_ref, gate_ref, gpre_ref, gpost_ref,
                wup_ref, cw_ref, cb_ref, wdown_ref, o_ref, act_scr, up_scr, af_scr):
    tm = x_ref.shape[1]
    dff = wdown_ref.shape[0]
    j = pl.program_id(1)
    nj = pl.num_programs(1)

    def hmod(v):
        return _rms(v, gpre_ref[...]) * (1.0 + scale_ref[0]) + shift_ref[0]

    x = x_ref[0]
    hp = jnp.where(j > 0, hmod(xp_ref[0]), 0.0)
    hn = jnp.where(j < nj - 1, hmod(xn_ref[0]), 0.0)
    hext = jnp.concatenate([hp, hmod(x), hn], axis=0).astype(BF16)

    nsl = FFN_COLS // LANES
    for cidx in range(dff // FFN_COLS):
        slot = cidx % FFN_RING
        conv = []
        for half, base in enumerate((cidx * FFN_COLS, dff + cidx * FFN_COLS)):
            up = jnp.dot(hext, wup_ref[:, base:base + FFN_COLS], preferred_element_type=F32)
            for s in range(nsl):
                lanes = slice(base + s * LANES, base + (s + 1) * LANES)
                slab = (2 * slot + half) * nsl + s
                up_scr[slab] = up[:, s * LANES:(s + 1) * LANES]
                conv.append([cb_ref[:, lanes] + sum(
                    cw_ref[k:k + 1, lanes] * up_scr[slab, pl.ds(HALO - 1 + par + k, tm // 2, stride=2), :]
                    for k in range(3)) for par in range(2)])
        for s in range(nsl):
            for par in range(2):
                af_scr[slot * nsl + s, pl.ds(par, tm // 2, stride=2), :] = (
                    _silu(conv[nsl + s][par]) * conv[s][par])
            lo = cidx * FFN_COLS + s * LANES
            act_scr[:, lo:lo + LANES] = af_scr[slot * nsl + s].astype(BF16)

    f = jnp.dot(act_scr[...], wdown_ref[...], preferred_element_type=F32)
    o_ref[0] = x + gate_ref[0] * _rms(f, gpost_ref[...])


def _ffn(x, mods, gpre, gpost, wup, cw, cb, wdown, tm):
    b, n, d = x.shape
    dff = wdown.shape[0]
    return pl.pallas_call(
        _ffn_kernel,
        grid=(b, n // tm),
        in_specs=[pl.BlockSpec((1, tm, d), lambda i, j: (i, j, 0))] + _halo_specs(tm, n, d) + [
            _mod_spec(d, 3), _mod_spec(d, 4), _mod_spec(d, 5),
            pl.BlockSpec((1, d), lambda i, j: (0, 0)),
            pl.BlockSpec((1, d), lambda i, j: (0, 0)),
            pl.BlockSpec(wup.shape, lambda i, j: (0, 0), pipeline_mode=pl.Buffered(1)),
            pl.BlockSpec(cw.shape, lambda i, j: (0, 0)),
            pl.BlockSpec(cb.shape, lambda i, j: (0, 0)),
            pl.BlockSpec(wdown.shape, lambda i, j: (0, 0), pipeline_mode=pl.Buffered(1))],
        out_specs=pl.BlockSpec((1, tm, d), lambda i, j: (i, j, 0)),
        out_shape=jax.ShapeDtypeStruct((b, n, d), F32),
        scratch_shapes=[pltpu.VMEM((tm, dff), BF16),
                        pltpu.VMEM((2 * FFN_RING * FFN_COLS // LANES, tm + 2 * HALO, LANES), F32),
                        pltpu.VMEM((FFN_RING * FFN_COLS // LANES, tm, LANES), F32)],
        compiler_params=pltpu.CompilerParams(dimension_semantics=("arbitrary", "arbitrary"),
                                             vmem_limit_bytes=VMEM_LIMIT),
        name="ffn",
    )(x, x, x, mods, mods, mods, gpre, gpost, wup, cw, cb, wdown)


def kernel(x, c, ctx, c_ctx, pre_norm_mix, post_norm_mix, pre_norm_ffn, post_norm_ffn, w_ada, b_ada,
           w_in, conv_ssd_w, conv_ssd_b, dt_bias, a_log, d_skip, ssm_norm, w_pool, pool_scale, w_out,
           w_up, conv_ffn_w, conv_ffn_b, w_down):
    assert w_ada.shape[0] == 1, "single-layer block"
    b, n, d = x.shape
    d_xb = D_SSM + D_BC // 2
    off_dt = 2 * D_SSM + D_BC
    off_pool = off_dt + 2 * SSD_HEADS
    tm_in = 1024

    assert b < ADA_ROWS and w_ada.shape[2] == N_MODS * d
    cc = jnp.concatenate([c, c_ctx[None, :], jnp.zeros((ADA_ROWS - b - 1, d), F32)], axis=0)
    mods = _ada(cc, w_ada[0], b_ada)

    wi = w_in[0]
    w_all = wi.astype(BF16)
    w_pd = jnp.concatenate([wi[:, off_pool:], wi[:, off_dt:off_pool],
                            jnp.zeros((d, LANES - 2 * SSD_HEADS), F32)], axis=1).astype(BF16)
    bias = jnp.broadcast_to(dt_bias[0].reshape(2 * SSD_HEADS, 1), (2 * SSD_HEADS, tm_in))
    alog = jnp.broadcast_to(a_log[0].reshape(2 * SSD_HEADS, 1), (2 * SSD_HEADS, tm_in))

    xs, bc, zu, rows, cols, ds = _mix_in(x, mods, None, pre_norm_mix, w_all, w_pd, conv_ssd_w[0],
                                         conv_ssd_b, bias, alog, tm_in, True, "mix_in")
    crows, cds = _mix_in(ctx, mods, b, pre_norm_mix, w_all, w_pd, conv_ssd_w[0, :, :d_xb],
                         conv_ssd_b[:, :d_xb], bias, alog, tm_in, False, "mix_in_ctx")

    dsk = jnp.repeat(d_skip[0], SSD_HEAD_DIM)[None, :]
    x1, (w_up_b, w_down_b) = _mix_out(xs, bc, zu, rows, cols, ds, cds, crows, x, mods, post_norm_mix, dsk,
                                      ssm_norm, w_pool[0].astype(BF16), pool_scale, w_out[0].astype(BF16),
                                      [w_up[0], w_down[0]], 512)

    return _ffn(x1, mods, pre_norm_ffn, post_norm_ffn, w_up_b, conv_ffn_w[0], conv_ffn_b, w_down_b, 1024)
```

```python
import functools

import numpy as np
import jax
import jax.numpy as jnp
from jax import lax
from jax.experimental import pallas as pl
from jax.experimental.pallas import tpu as pltpu

F32 = jnp.float32
BF16 = jnp.bfloat16

EPS = 1e-6
GRID_W = 64
SSD_HEADS = 8
SSD_HEAD_DIM = 64
SSD_GROUPS = 2
SSD_STATE = 128
SSD_PAIRS = SSD_HEADS // 2
SSD_CONV = 5
D_SSM = SSD_HEADS * SSD_HEAD_DIM
D_BC = 2 * SSD_GROUPS * SSD_STATE
POOL_WINDOWS = (2, 4, 8, 16)
D_POOL = 512
CHUNK = 128
LANES = 128
HALO = 8
N_ROWS = 9
VMEM_LIMIT = 56 * 1024 * 1024


def _silu(v):
    return v / (1.0 + jnp.exp(-v))


def _rms(v, gain):
    ms = jnp.mean(v * v, axis=-1, keepdims=True)
    return v * lax.rsqrt(ms + EPS) * gain


N_MODS = 6
ADA_ROWS = 16


def _mod_spec(d, k, row=None):
    if row is None:
        return pl.BlockSpec((1, 1, d), lambda i, j: (k * ADA_ROWS + i, 0, 0))
    return pl.BlockSpec((1, 1, d), lambda i, j: (k * ADA_ROWS + row, 0, 0))


def _halo_specs(tm, n, d):
    hb, nb = tm // HALO, n // HALO
    return [pl.BlockSpec((1, HALO, d), lambda i, j: (i, jnp.maximum(j * hb - 1, 0), 0)),
            pl.BlockSpec((1, HALO, d), lambda i, j: (i, jnp.minimum((j + 1) * hb, nb - 1), 0))]


def _cast_blocks(a, steps):
    rps = next(r for r in range(16, a.shape[0] + 1, 16) if a.shape[0] % r == 0 and a.shape[0] // r <= steps)
    return rps, a.shape[0] // rps - 1


def _ada_kernel(c_ref, w_ref, b_ref, o_ref):
    s = _silu(c_ref[...]).astype(BF16)
    o_ref[:, 0, :] = jnp.dot(s, w_ref[...].astype(BF16), preferred_element_type=F32) + b_ref[...]


def _ada(cc, w_ada, b_ada):
    rows, d = cc.shape
    n = w_ada.shape[1]
    return pl.pallas_call(
        _ada_kernel,
        grid=(n // d,),
        in_specs=[pl.BlockSpec((rows, d), lambda j: (0, 0)),
                  pl.BlockSpec((d, d), lambda j: (0, j)),
                  pl.BlockSpec((1, d), lambda j: (0, j))],
        out_specs=pl.BlockSpec((rows, 1, d), lambda j: (j, 0, 0)),
        out_shape=jax.ShapeDtypeStruct((n // d * rows, 1, d), F32),
        compiler_params=pltpu.CompilerParams(dimension_semantics=("arbitrary",),
                                             vmem_limit_bytes=VMEM_LIMIT),
        name="ada",
    )(cc, w_ada, b_ada)


def _dt_rows(dtraw_chunks, bias, a):
    h = SSD_HEADS
    raw = jnp.concatenate([blk.T[0:2 * h, :] for blk in dtraw_chunks], axis=1) + bias
    t = raw.shape[1]
    dt = jnp.maximum(raw, 0.0) + jnp.log1p(jnp.exp(-jnp.abs(raw)))
    dta = dt * a
    seg = lax.broadcasted_iota(jnp.int32, dta.shape, 1) & (CHUNK - 1)
    cs, rcs = dta, dta
    k = 1
    while k < CHUNK:
        cs = cs + jnp.where(seg >= k, pltpu.roll(cs, k, 1), 0.0)
        rcs = rcs + jnp.where(seg < CHUNK - k, pltpu.roll(rcs, t - k, 1), 0.0)
        k *= 2
    cd = jnp.exp(cs + rcs - dta)
    wf = jnp.exp(rcs[0:h] - dta[0:h]) * dt[0:h]
    wb = jnp.exp(cs[h:] - dta[h:]) * dt[h:]
    log2e = 1.0 / np.log(2.0)
    pf2, ab2 = cs[0:h] * log2e, rcs[h:] * log2e
    ldt2 = jnp.log(dt) * log2e
    ldg2 = jnp.log(dt[0:h] + dt[h:]) * log2e
    return [pf2, ab2, pf2 - ldt2[0:h], ab2 - ldt2[h:], ldg2, wf, wb, cd[0:h], cd[h:]]


def _pair_rhs(xs, p):
    xp = xs[:, p * LANES:(p + 1) * LANES]
    lane = lax.broadcasted_iota(jnp.int32, xp.shape, 1)
    zero = jnp.zeros_like(xp)
    return jnp.concatenate([jnp.where(lane < SSD_HEAD_DIM, xp, zero),
                            jnp.where(lane >= SSD_HEAD_DIM, xp, zero)], axis=0)


def _pair_select(lane, v0, v1):
    return jnp.where(lane < SSD_HEAD_DIM, v0, v1)


def _local_states(xs, bts, wf, wb):
    out_f, out_b = [], []
    for p in range(SSD_PAIRS):
        bt = bts[p // 2]
        h0, h1 = 2 * p, 2 * p + 1
        lhs = jnp.concatenate(
            [jnp.concatenate([(bt * w[h0:h0 + 1, :]).astype(BF16), (bt * w[h1:h1 + 1, :]).astype(BF16)], axis=1)
             for w in (wf, wb)], axis=0)
        res = jnp.dot(lhs, _pair_rhs(xs, p), preferred_element_type=F32)
        out_f.append(res[0:SSD_STATE])
        out_b.append(res[SSD_STATE:2 * SSD_STATE])
    return out_f, out_b


def _mix_in_kernel(*refs, conv_cols, latent):
    (x_ref, xp_ref, xn_ref, shift_ref, scale_ref, g_ref, w_ref, wpd_ref, cw_ref, cb_ref,
     bias_ref, alog_ref) = refs[:12]
    if latent:
        xs_ref, bc_ref, zu_ref, rows_ref, cols_ref, ds_ref, up_scr, cv_scr = refs[12:]
    else:
        rows_ref, ds_ref, up_scr, cv_scr = refs[12:]
    nseg, seg, d = x_ref.shape
    tm = nseg * seg
    nch = tm // CHUNK
    cps = seg // CHUNK
    j = pl.program_id(1)
    nj = pl.num_programs(1)

    def hmod(v):
        return _rms(v, g_ref[...]) * (1.0 + scale_ref[0]) + shift_ref[0]

    xm = hmod(x_ref[...].reshape(tm, d))
    gap = jnp.zeros((HALO, d), F32)
    hm = xm.astype(BF16)
    if latent:
        hp = jnp.where(j > 0, hmod(xp_ref[0]), 0.0)
        hn = jnp.where(j < nj - 1, hmod(xn_ref[0]), 0.0)
        blocks = [(jnp.concatenate([hp, xm, hn], axis=0).astype(BF16), 0)]
        convs = [(HALO, 0, tm)]
    else:
        pieces = [gap]
        for i in range(nseg):
            pieces += [xm[i * seg:(i + 1) * seg], gap]
        ext_rows = up_scr.shape[1]
        if ext_rows > tm + (nseg + 1) * HALO:
            pieces.append(jnp.zeros((ext_rows - tm - (nseg + 1) * HALO, d), F32))
        blocks = [(jnp.concatenate(pieces, axis=0).astype(BF16), 0)]
        convs = [(HALO + i * (seg + HALO), i * seg, seg) for i in range(nseg)]

    def zero_of(v):
        u = pltpu.bitcast(v[0:8, 0:LANES], jnp.uint32)
        u = lax.shift_right_logical(lax.shift_right_logical(u, jnp.uint32(16)), jnp.uint32(16))
        return pltpu.bitcast(u, F32)[0:1, :]

    ties = {}
    if latent:
        pd = jnp.dot(hm, wpd_ref[...], preferred_element_type=F32)
        zu_ref[0, :, D_SSM:D_SSM + D_POOL] = pd[:, 0:D_POOL].astype(BF16)
        dtraw = pd[:, D_POOL:D_POOL + LANES]
        for hz in range(2):
            zc = slice(hz * 2 * LANES, (hz + 1) * 2 * LANES)
            zp = jnp.dot(hm, w_ref[:, zc], preferred_element_type=F32)
            zu_ref[0, :, zc] = zp.astype(BF16)
            ties[conv_cols // LANES - 2 + hz] = zero_of(zp)
    else:
        dtraw = jnp.dot(hm, wpd_ref[:, D_POOL:D_POOL + LANES], preferred_element_type=F32)

    half = SSD_CONV // 2
    cc = 2 * LANES
    for hblk, srow in blocks:
        for cb in range(conv_cols // cc):
            up = jnp.dot(hblk, w_ref[:, D_SSM + cb * cc:D_SSM + (cb + 1) * cc], preferred_element_type=F32)
            for s in range(cc // LANES):
                up_scr[cb * (cc // LANES) + s, srow:srow + hblk.shape[0], :] = up[:, s * LANES:(s + 1) * LANES]
    for slab in range(conv_cols // LANES):
        lanes = slice(slab * LANES, (slab + 1) * LANES)
        for drow, orow, nrows in convs:
            for par in range(2):
                lo = drow - half + par
                acc = cb_ref[:, lanes] + cw_ref[0:1, lanes] * up_scr[slab, pl.ds(lo, nrows // 2, stride=2), :]
                for k in range(1, SSD_CONV):
                    acc = acc + cw_ref[k:k + 1, lanes] * up_scr[slab, pl.ds(lo + k, nrows // 2, stride=2), :]
                if par == 0 and slab in ties:
                    acc = acc + ties[slab]
                cv_scr[slab, pl.ds(orow + par, nrows // 2, stride=2), :] = _silu(acc)
        if latent:
            if slab < D_SSM // LANES:
                xs_ref[0, :, lanes] = cv_scr[slab].astype(BF16)
            else:
                bc_ref[0, :, slab * LANES - D_SSM:(slab + 1) * LANES - D_SSM] = cv_scr[slab].astype(BF16)

    rows = _dt_rows([dtraw[c * CHUNK:(c + 1) * CHUNK] for c in range(nch)], bias_ref[...],
                    -jnp.exp(alog_ref[...]))
    pad = jnp.zeros((LANES - 2 * SSD_HEADS, CHUNK), F32)
    for c in range(nch):
        lanes = slice(c * CHUNK, (c + 1) * CHUNK)
        for q, arr in enumerate(rows):
            rows_ref[c // cps, c % cps, q] = arr[:, lanes]
        if latent:
            cols_ref[0, c] = jnp.concatenate([rows[0][:, lanes], rows[1][:, lanes], pad], axis=0).T

    def state_body(c, carry):
        r0 = pl.multiple_of(c * CHUNK, CHUNK)
        ci, cj = c // cps, c % cps
        xs = jnp.concatenate([cv_scr[s, pl.ds(r0, CHUNK), :] for s in range(D_SSM // LANES)],
                             axis=1).astype(BF16)
        bts = [cv_scr[D_SSM // LANES + g, pl.ds(r0, CHUNK), :].T for g in range(SSD_GROUPS)]
        dsf, dsb = _local_states(xs, bts, rows_ref[ci, cj, 5], rows_ref[ci, cj, 6])
        for p in range(SSD_PAIRS):
            ds_ref[ci, cj, 0, p] = dsf[p].astype(BF16)
            ds_ref[ci, cj, 1, p] = dsb[p].astype(BF16)
        return carry

    lax.fori_loop(0, nch, state_body, 0, unroll=4)


def _mix_in(x, mods, mod_row, gain, w, wpd, conv_w, conv_b, bias, alog, tm, latent, name):
    b, n, d = x.shape
    conv_cols = conv_w.shape[1]
    nch = tm // CHUNK
    nseg = 1 if latent else tm // n
    seg = tm // nseg
    cps = seg // CHUNK
    ext_rows = -(-(tm + (nseg + 1) * HALO) // 16) * 16
    full = lambda a: pl.BlockSpec(a.shape, lambda i, j: (0,) * a.ndim)
    stat = lambda tail, dt: jax.ShapeDtypeStruct((b, n // CHUNK) + tail, dt)
    out_specs, out_shape = [], []
    if latent:
        grid = (b, n // tm)
        x_spec = pl.BlockSpec((1, tm, d), lambda i, j: (i, j, 0))
        out_specs += [pl.BlockSpec((1, tm, D_SSM), lambda i, j: (i, j, 0)),
                      pl.BlockSpec((1, tm, conv_cols - D_SSM), lambda i, j: (i, j, 0)),
                      pl.BlockSpec((1, tm, D_SSM + D_POOL), lambda i, j: (i, j, 0))]
        out_shape += [jax.ShapeDtypeStruct((b, n, D_SSM), BF16),
                      jax.ShapeDtypeStruct((b, n, conv_cols - D_SSM), BF16),
                      jax.ShapeDtypeStruct((b, n, D_SSM + D_POOL), BF16)]
        blk = lambda tail: pl.BlockSpec((1, nch) + tail, lambda i, j: (i, j) + (0,) * len(tail))
    else:
        grid = (b // nseg, 1)
        x_spec = pl.BlockSpec((nseg, n, d), lambda i, j: (i, 0, 0))
        blk = lambda tail: pl.BlockSpec((nseg, cps) + tail, lambda i, j: (i, 0) + (0,) * len(tail))
    out_specs.append(blk((N_ROWS, SSD_HEADS, CHUNK)))
    out_shape.append(stat((N_ROWS, SSD_HEADS, CHUNK), F32))
    if latent:
        out_specs.append(blk((CHUNK, LANES)))
        out_shape.append(stat((CHUNK, LANES), F32))
    out_specs.append(blk((2, SSD_PAIRS, SSD_STATE, LANES)))
    out_shape.append(stat((2, SSD_PAIRS, SSD_STATE, LANES), BF16))
    return pl.pallas_call(
        functools.partial(_mix_in_kernel, conv_cols=conv_cols, latent=latent),
        grid=grid,
        in_specs=[x_spec] + _halo_specs(seg, n, d) + [
            _mod_spec(d, 0, mod_row), _mod_spec(d, 1, mod_row),
            full(gain), full(w), full(wpd), full(conv_w), full(conv_b), full(bias), full(alog)],
        out_specs=out_specs,
        out_shape=out_shape,
        scratch_shapes=[pltpu.VMEM((conv_cols // LANES, ext_rows, LANES), F32),
                        pltpu.VMEM((conv_cols // LANES, tm, LANES), F32)],
        compiler_params=pltpu.CompilerParams(dimension_semantics=("arbitrary", "arbitrary"),
                                             vmem_limit_bytes=VMEM_LIMIT),
        name=name,
    )(x, x, x, mods, mods, gain, w, wpd, conv_w, conv_b, bias, alog)


def _mix_out_kernel(xs_ref, bc_ref, z_ref, u_ref, rows_ref, cols_ref, ds_ref, cds_ref, crows_ref,
                    x_ref, gate_ref, gn_ref, dsk_ref, gssm_ref, wpool_ref, pscale_ref, wout_ref,
                    cm_ref, inv_ref, *rest):
    ncast = (len(rest) - 4) // 2
    o_ref = rest[ncast]
    sf_scr, sb_scr, d_scr = rest[2 * ncast + 1:]
    n = u_ref.shape[1]
    tm = x_ref.shape[1]
    nchunks = n // CHUNK
    nctx = cds_ref.shape[1]
    tch = tm // CHUNK
    j = pl.program_id(1)
    lane_row = lax.broadcasted_iota(jnp.int32, (1, LANES), 1)

    def cd_row(ref, c, q, p):
        cd = ref[0, c, q]
        return _pair_select(lane_row, cd[2 * p:2 * p + 1, :], cd[2 * p + 1:2 * p + 2, :])

    @pl.when(j == 0)
    def _():
        for p in range(SSD_PAIRS):
            sf = cds_ref[0, 0, 0, p].astype(F32)
            for c in range(1, nctx):
                sf = sf * cd_row(crows_ref, c, 7, p) + cds_ref[0, c, 0, p].astype(F32)
            sb = cds_ref[0, nctx - 1, 1, p].astype(F32)
            for c in reversed(range(nctx - 1)):
                sb = sb * cd_row(crows_ref, c, 8, p) + cds_ref[0, c, 1, p].astype(F32)
            sf_scr[0, p] = sf
            sb_scr[nchunks - 1, p] = sb

        def scan_body(i, carry):
            cf = i
            cbk = nchunks - 1 - i
            for p in range(SSD_PAIRS):
                sf_scr[cf + 1, p] = (sf_scr[cf, p] * cd_row(rows_ref, cf, 7, p)
                                     + ds_ref[0, cf, 0, p].astype(F32))
                sb_scr[cbk - 1, p] = (sb_scr[cbk, p] * cd_row(rows_ref, cbk, 8, p)
                                      + ds_ref[0, cbk, 1, p].astype(F32))
            return carry

        lax.fori_loop(0, nchunks - 1, scan_body, 0)

        for gi, w in enumerate(POOL_WINDOWS):
            cols = slice(gi * LANES, (gi + 1) * LANES)
            cmat = cm_ref[gi]
            parts = [jnp.dot(cmat, u_ref[0, i * LANES:(i + 1) * LANES, cols], preferred_element_type=F32)
                     for i in range(n // LANES)]
            cs = jnp.concatenate(parts, axis=0)
            h = jnp.concatenate([jnp.zeros((w // 2 * GRID_W, LANES), F32), cs], axis=0)
            k = 1
            while k < w:
                sh = k * GRID_W
                h = h + jnp.concatenate([h[sh:], jnp.zeros((sh, LANES), F32)], axis=0)
                k *= 2
            m = h[0:n] * inv_ref[gi]
            d_scr[:, cols] = (m - u_ref[0, :, cols].astype(F32)).astype(BF16)

    row_i = lax.broadcasted_iota(jnp.int32, (CHUNK, CHUNK), 0)
    col_i = lax.broadcasted_iota(jnp.int32, (CHUNK, CHUNK), 1)
    lane_full = lax.broadcasted_iota(jnp.int32, (CHUNK, LANES), 1)

    yns = []
    for lc in range(tch):
        c = j * tch + lc
        r0 = lc * CHUNK
        xs = xs_ref[0, pl.ds(r0, CHUNK), :]
        pfp, abp, ldg = rows_ref[0, c, 2], rows_ref[0, c, 3], rows_ref[0, c, 4]
        cols = cols_ref[0, c]

        y_pairs = []
        for g in range(SSD_GROUPS):
            bg = bc_ref[0, pl.ds(r0, CHUNK), g * SSD_STATE:(g + 1) * SSD_STATE]
            cg = bc_ref[0, pl.ds(r0, CHUNK), (SSD_GROUPS + g) * SSD_STATE:(SSD_GROUPS + g + 1) * SSD_STATE]
            gmat = lax.dot_general(cg, bg, (((1,), (1,)), ((), ())), preferred_element_type=F32)
            scat = jnp.concatenate([sf_scr[c, 2 * g].astype(BF16), sf_scr[c, 2 * g + 1].astype(BF16),
                                    sb_scr[c, 2 * g].astype(BF16), sb_scr[c, 2 * g + 1].astype(BF16)], axis=1)
            yoff = jnp.dot(cg, scat, preferred_element_type=F32)
            for q in range(2):
                p = 2 * g + q
                ms, efs, ebs = [], [], []
                for h in (2 * p, 2 * p + 1):
                    pf_c = jnp.broadcast_to(cols[:, h:h + 1], (CHUNK, CHUNK))
                    ab_c = jnp.broadcast_to(cols[:, SSD_HEADS + h:SSD_HEADS + h + 1], (CHUNK, CHUNK))
                    dm = jnp.where(row_i > col_i, pf_c - pfp[h:h + 1, :],
                                   jnp.where(row_i < col_i, ab_c - abp[h:h + 1, :], ldg[h:h + 1, :]))
                    ms.append((gmat * jnp.exp2(dm)).astype(BF16))
                    efs.append(jnp.exp2(pf_c))
                    ebs.append(jnp.exp2(ab_c))
                ydiag = jnp.dot(jnp.concatenate(ms, axis=1), _pair_rhs(xs, p), preferred_element_type=F32)
                y_pairs.append(ydiag
                               + _pair_select(lane_full, efs[0], efs[1]) * yoff[:, q * LANES:(q + 1) * LANES]
                               + _pair_select(lane_full, ebs[0], ebs[1]) * yoff[:, (2 + q) * LANES:(3 + q) * LANES])
        y = jnp.concatenate(y_pairs, axis=1) + dsk_ref[...] * xs.astype(F32)
        yz = y * _silu(z_ref[0, pl.ds(r0, CHUNK), :].astype(F32))
        yns.append(_rms(yz, gssm_ref[...]).astype(BF16))

    yn = jnp.concatenate(yns, axis=0)
    trows = pl.ds(pl.multiple_of(j * tm, tm), tm)
    ps = [jnp.dot(d_scr[trows, gi * LANES:(gi + 1) * LANES], wpool_ref[gi], preferred_element_type=F32)
          for gi in range(len(POOL_WINDOWS))]
    pm = (jnp.concatenate(ps, axis=1) * pscale_ref[...]).astype(BF16)
    oc = 2 * LANES
    yx = jnp.concatenate(
        [jnp.dot(yn, wout_ref[0:D_SSM, nt * oc:(nt + 1) * oc], preferred_element_type=F32)
         + jnp.dot(pm, wout_ref[D_SSM:D_SSM + D_POOL, nt * oc:(nt + 1) * oc], preferred_element_type=F32)
         for nt in range(wout_ref.shape[1] // oc)], axis=1)
    o_ref[0] = x_ref[0] + gate_ref[0] * _rms(yx, gn_ref[...])

    for src, dst in zip(rest[:ncast], rest[ncast + 1:2 * ncast + 1]):
        dst[...] = src[...].astype(BF16)


def _pool_constants(n):
    rows = n // GRID_W
    t = np.arange(LANES)
    blk, col = t // GRID_W, t % GRID_W
    cms, invs = [], []
    r = np.arange(n) // GRID_W
    c = np.arange(n) % GRID_W
    for w in POOL_WINDOWS:
        d = col[None, :] - col[:, None]
        cms.append(((blk[:, None] == blk[None, :]) & (d >= -(w // 2)) & (d < w - w // 2)).astype(np.float32))
        cnt_r = np.minimum(r + w - w // 2, rows) - np.maximum(r - w // 2, 0)
        cnt_c = np.minimum(c + w - w // 2, GRID_W) - np.maximum(c - w // 2, 0)
        invs.append(np.broadcast_to((1.0 / (cnt_r * cnt_c))[:, None], (n, LANES)).astype(np.float32))
    return np.stack(cms), np.stack(invs)


def _mix_out(xs, bc, zu, rows, cols, ds, cds, crows, x, mods, gn, dsk, gssm, wpool, pscale, wout, casts, tm):
    b, n, d = x.shape
    nchunks = n // CHUNK
    nj = n // tm
    cm_np, inv_np = _pool_constants(n)
    cm = jnp.asarray(cm_np, BF16)
    inv = jnp.asarray(inv_np, F32)
    full = lambda a: pl.BlockSpec(a.shape, lambda i, j: (0,) * a.ndim)
    per_batch = lambda a: pl.BlockSpec((1,) + a.shape[1:], lambda i, j: (i,) + (0,) * (a.ndim - 1))
    cast_specs = []
    for a in casts:
        rps, last = _cast_blocks(a, b * nj)
        cast_specs.append(pl.BlockSpec((rps, a.shape[1]), lambda i, j, last=last: (jnp.minimum(i * nj + j, last), 0)))
    outs = pl.pallas_call(
        _mix_out_kernel,
        grid=(b, n // tm),
        in_specs=[pl.BlockSpec((1, tm, D_SSM), lambda i, j: (i, j, 0)),
                  pl.BlockSpec((1, tm, D_BC), lambda i, j: (i, j, 0)),
                  pl.BlockSpec((1, tm, D_SSM), lambda i, j: (i, j, 0)),
                  pl.BlockSpec((1, n, D_POOL), lambda i, j: (i, 0, 1)),
                  per_batch(rows), per_batch(cols), per_batch(ds), per_batch(cds), per_batch(crows),
                  pl.BlockSpec((1, tm, d), lambda i, j: (i, j, 0)),
                  _mod_spec(d, 2),
                  full(gn), full(dsk), full(gssm), full(wpool), full(pscale), full(wout), full(cm),
                  pl.BlockSpec(inv.shape, lambda i, j: (0, 0, 0), pipeline_mode=pl.Buffered(1))] + cast_specs,
        out_specs=[pl.BlockSpec((1, tm, d), lambda i, j: (i, j, 0))] + cast_specs,
        out_shape=[jax.ShapeDtypeStruct((b, n, d), F32)] + [jax.ShapeDtypeStruct(a.shape, BF16) for a in casts],
        scratch_shapes=[pltpu.VMEM((nchunks, SSD_PAIRS, SSD_STATE, LANES), F32),
                        pltpu.VMEM((nchunks, SSD_PAIRS, SSD_STATE, LANES), F32),
                        pltpu.VMEM((n, D_POOL), BF16)],
        compiler_params=pltpu.CompilerParams(dimension_semantics=("arbitrary", "arbitrary"),
                                             vmem_limit_bytes=VMEM_LIMIT),
        name="mix_out",
    )(xs, bc, zu, zu, rows, cols, ds, cds, crows, x, mods, gn, dsk, gssm, wpool, pscale, wout, cm, inv, *casts)
    return outs[0], outs[1:]


FFN_COLS = 2 * LANES
FFN_RING = 2


def _ffn_kernel(x_ref, xp_ref, xn_ref, shift_ref, scale_ref, gate_ref, gpre_ref, gpost_ref,
                wup_ref, cw_ref, cb_ref, wdown_ref, o_ref, act_scr, up_scr, af_scr):
    tm = x_ref.shape[1]
    dff = wdown_ref.shape[0]
    j = pl.program_id(1)
    nj = pl.num_programs(1)

    def hmod(v):
        return _rms(v, gpre_ref[...]) * (1.0 + scale_ref[0]) + shift_ref[0]

    x = x_ref[0]
    hp = jnp.where(j > 0, hmod(xp_ref[0]), 0.0)
    hn = jnp.where(j < nj - 1, hmod(xn_ref[0]), 0.0)
    hext = jnp.concatenate([hp, hmod(x), hn], axis=0).astype(BF16)

    nsl = FFN_COLS // LANES
    for cidx in range(dff // FFN_COLS):
        slot = cidx % FFN_RING
        conv = []
        for half, base in enumerate((cidx * FFN_COLS, dff + cidx * FFN_COLS)):
            up = jnp.dot(hext, wup_ref[:, base:base + FFN_COLS], preferred_element_type=F32)
            for s in range(nsl):
                lanes = slice(base + s * LANES, base + (s + 1) * LANES)
                slab = (2 * slot + half) * nsl + s
                up_scr[slab] = up[:, s * LANES:(s + 1) * LANES]
                conv.append([cb_ref[:, lanes] + sum(
                    cw_ref[k:k + 1, lanes] * up_scr[slab, pl.ds(HALO - 1 + par + k, tm // 2, stride=2), :]
                    for k in range(3)) for par in range(2)])
        for s in range(nsl):
            for par in range(2):
                af_scr[slot * nsl + s, pl.ds(par, tm // 2, stride=2), :] = (
                    _silu(conv[nsl + s][par]) * conv[s][par])
            lo = cidx * FFN_COLS + s * LANES
            act_scr[:, lo:lo + LANES] = af_scr[slot * nsl + s].astype(BF16)

    f = jnp.dot(act_scr[...], wdown_ref[...], preferred_element_type=F32)
    o_ref[0] = x + gate_ref[0] * _rms(f, gpost_ref[...])


def _ffn(x, mods, gpre, gpost, wup, cw, cb, wdown, tm):
    b, n, d = x.shape
    dff = wdown.shape[0]
    return pl.pallas_call(
        _ffn_kernel,
        grid=(b, n // tm),
        in_specs=[pl.BlockSpec((1, tm, d), lambda i, j: (i, j, 0))] + _halo_specs(tm, n, d) + [
            _mod_spec(d, 3), _mod_spec(d, 4), _mod_spec(d, 5),
            pl.BlockSpec((1, d), lambda i, j: (0, 0)),
            pl.BlockSpec((1, d), lambda i, j: (0, 0)),
            pl.BlockSpec(wup.shape, lambda i, j: (0, 0), pipeline_mode=pl.Buffered(1)),
            pl.BlockSpec(cw.shape, lambda i, j: (0, 0)),
            pl.BlockSpec(cb.shape, lambda i, j: (0, 0)),
            pl.BlockSpec(wdown.shape, lambda i, j: (0, 0), pipeline_mode=pl.Buffered(1))],
        out_specs=pl.BlockSpec((1, tm, d), lambda i, j: (i, j, 0)),
        out_shape=jax.ShapeDtypeStruct((b, n, d), F32),
        scratch_shapes=[pltpu.VMEM((tm, dff), BF16),
                        pltpu.VMEM((2 * FFN_RING * FFN_COLS // LANES, tm + 2 * HALO, LANES), F32),
                        pltpu.VMEM((FFN_RING * FFN_COLS // LANES, tm, LANES), F32)],
        compiler_params=pltpu.CompilerParams(dimension_semantics=("arbitrary", "arbitrary"),
                                             vmem_limit_bytes=VMEM_LIMIT),
        name="ffn",
    )(x, x, x, mods, mods, mods, gpre, gpost, wup, cw, cb, wdown)


def kernel(x, c, ctx, c_ctx, pre_norm_mix, post_norm_mix, pre_norm_ffn, post_norm_ffn, w_ada, b_ada,
           w_in, conv_ssd_w, conv_ssd_b, dt_bias, a_log, d_skip, ssm_norm, w_pool, pool_scale, w_out,
           w_up, conv_ffn_w, conv_ffn_b, w_down):
    assert w_ada.shape[0] == 1, "single-layer block"
    b, n, d = x.shape
    d_xb = D_SSM + D_BC // 2
    off_dt = 2 * D_SSM + D_BC
    off_pool = off_dt + 2 * SSD_HEADS
    tm_in = 1024

    assert b < ADA_ROWS and w_ada.shape[2] == N_MODS * d
    cc = jnp.concatenate([c, c_ctx[None, :], jnp.zeros((ADA_ROWS - b - 1, d), F32)], axis=0)
    mods = _ada(cc, w_ada[0], b_ada)

    wi = w_in[0]
    w_all = wi.astype(BF16)
    w_pd = jnp.concatenate([wi[:, off_pool:], wi[:, off_dt:off_pool],
                            jnp.zeros((d, LANES - 2 * SSD_HEADS), F32)], axis=1).astype(BF16)
    bias = jnp.broadcast_to(dt_bias[0].reshape(2 * SSD_HEADS, 1), (2 * SSD_HEADS, tm_in))
    alog = jnp.broadcast_to(a_log[0].reshape(2 * SSD_HEADS, 1), (2 * SSD_HEADS, tm_in))

    xs, bc, zu, rows, cols, ds = _mix_in(x, mods, None, pre_norm_mix, w_all, w_pd, conv_ssd_w[0],
                                         conv_ssd_b, bias, alog, tm_in, True, "mix_in")
    crows, cds = _mix_in(ctx, mods, b, pre_norm_mix, w_all, w_pd, conv_ssd_w[0, :, :d_xb],
                         conv_ssd_b[:, :d_xb], bias, alog, tm_in, False, "mix_in_ctx")

    dsk = jnp.repeat(d_skip[0], SSD_HEAD_DIM)[None, :]
    x1, (w_up_b, w_down_b) = _mix_out(xs, bc, zu, rows, cols, ds, cds, crows, x, mods, post_norm_mix, dsk,
                                      ssm_norm, w_pool[0].astype(BF16), pool_scale, w_out[0].astype(BF16),
                                      [w_up[0], w_down[0]], 512)

    return _ffn(x1, mods, pre_norm_ffn, post_norm_ffn, w_up_b, conv_ffn_w[0], conv_ffn_b, w_down_b, 1024)
```

```python
import functools

import numpy as np
import jax
import jax.numpy as jnp
from jax import lax
from jax.experimental import pallas as pl
from jax.experimental.pallas import tpu as pltpu

F32 = jnp.float32
BF16 = jnp.bfloat16

EPS = 1e-6
GRID_W = 64
SSD_HEADS = 8
SSD_HEAD_DIM = 64
SSD_GROUPS = 2
SSD_STATE = 128
SSD_PAIRS = SSD_HEADS // 2
SSD_CONV = 5
D_SSM = SSD_HEADS * SSD_HEAD_DIM
D_BC = 2 * SSD_GROUPS * SSD_STATE
POOL_WINDOWS = (2, 4, 8, 16)
D_POOL = 512
CHUNK = 128
LANES = 128
HALO = 8
N_ROWS = 9
VMEM_LIMIT = 56 * 1024 * 1024


def _silu(v):
    return v / (1.0 + jnp.exp(-v))


def _rms(v, gain):
    ms = jnp.mean(v * v, axis=-1, keepdims=True)
    return v * lax.rsqrt(ms + EPS) * gain


N_MODS = 6
ADA_ROWS = 16


def _mod_spec(d, k, row=None):
    if row is None:
        return pl.BlockSpec((1, 1, d), lambda i, j: (k * ADA_ROWS + i, 0, 0))
    return pl.BlockSpec((1, 1, d), lambda i, j: (k * ADA_ROWS + row, 0, 0))


def _halo_specs(tm, n, d):
    hb, nb = tm // HALO, n // HALO
    return [pl.BlockSpec((1, HALO, d), lambda i, j: (i, jnp.maximum(j * hb - 1, 0), 0)),
            pl.BlockSpec((1, HALO, d), lambda i, j: (i, jnp.minimum((j + 1) * hb, nb - 1), 0))]


def _cast_blocks(a, steps):
    rps = next(r for r in range(16, a.shape[0] + 1, 16) if a.shape[0] % r == 0 and a.shape[0] // r <= steps)
    return rps, a.shape[0] // rps - 1


def _ada_kernel(c_ref, w_ref, b_ref, o_ref):
    s = _silu(c_ref[...]).astype(BF16)
    o_ref[:, 0, :] = jnp.dot(s, w_ref[...].astype(BF16), preferred_element_type=F32) + b_ref[...]


def _ada(cc, w_ada, b_ada):
    rows, d = cc.shape
    n = w_ada.shape[1]
    return pl.pallas_call(
        _ada_kernel,
        grid=(n // d,),
        in_specs=[pl.BlockSpec((rows, d), lambda j: (0, 0)),
                  pl.BlockSpec((d, d), lambda j: (0, j)),
                  pl.BlockSpec((1, d), lambda j: (0, j))],
        out_specs=pl.BlockSpec((rows, 1, d), lambda j: (j, 0, 0)),
        out_shape=jax.ShapeDtypeStruct((n // d * rows, 1, d), F32),
        compiler_params=pltpu.CompilerParams(dimension_semantics=("arbitrary",),
                                             vmem_limit_bytes=VMEM_LIMIT),
        name="ada",
    )(cc, w_ada, b_ada)


def _dt_rows(dtraw_chunks, bias, a):
    h = SSD_HEADS
    raw = jnp.concatenate([blk.T[0:2 * h, :] for blk in dtraw_chunks], axis=1) + bias
    t = raw.shape[1]
    dt = jnp.maximum(raw, 0.0) + jnp.log1p(jnp.exp(-jnp.abs(raw)))
    dta = dt * a
    seg = lax.broadcasted_iota(jnp.int32, dta.shape, 1) & (CHUNK - 1)
    cs, rcs = dta, dta
    k = 1
    while k < CHUNK:
        cs = cs + jnp.where(seg >= k, pltpu.roll(cs, k, 1), 0.0)
        rcs = rcs + jnp.where(seg < CHUNK - k, pltpu.roll(rcs, t - k, 1), 0.0)
        k *= 2
    cd = jnp.exp(cs + rcs - dta)
    wf = jnp.exp(rcs[0:h] - dta[0:h]) * dt[0:h]
    wb = jnp.exp(cs[h:] - dta[h:]) * dt[h:]
    log2e = 1.0 / np.log(2.0)
    pf2, ab2 = cs[0:h] * log2e, rcs[h:] * log2e
    ldt2 = jnp.log(dt) * log2e
    ldg2 = jnp.log(dt[0:h] + dt[h:]) * log2e
    return [pf2, ab2, pf2 - ldt2[0:h], ab2 - ldt2[h:], ldg2, wf, wb, cd[0:h], cd[h:]]


def _pair_rhs(xs, p):
    xp = xs[:, p * LANES:(p + 1) * LANES]
    lane = lax.broadcasted_iota(jnp.int32, xp.shape, 1)
    zero = jnp.zeros_like(xp)
    return jnp.concatenate([jnp.where(lane < SSD_HEAD_DIM, xp, zero),
                            jnp.where(lane >= SSD_HEAD_DIM, xp, zero)], axis=0)


def _pair_select(lane, v0, v1):
    return jnp.where(lane < SSD_HEAD_DIM, v0, v1)


def _local_states(xs, bts, wf, wb):
    out_f, out_b = [], []
    for p in range(SSD_PAIRS):
        bt = bts[p // 2]
        h0, h1 = 2 * p, 2 * p + 1
        lhs = jnp.concatenate(
            [jnp.concatenate([(bt * w[h0:h0 + 1, :]).astype(BF16), (bt * w[h1:h1 + 1, :]).astype(BF16)], axis=1)
             for w in (wf, wb)], axis=0)
        res = jnp.dot(lhs, _pair_rhs(xs, p), preferred_element_type=F32)
        out_f.append(res[0:SSD_STATE])
        out_b.append(res[SSD_STATE:2 * SSD_STATE])
    return out_f, out_b


def _mix_in_kernel(*refs, conv_cols, latent):
    (x_ref, xp_ref, xn_ref, shift_ref, scale_ref, g_ref, w_ref, wpd_ref, cw_ref, cb_ref,
     bias_ref, alog_ref) = refs[:12]
    if latent:
        xs_ref, bc_ref, zu_ref, rows_ref, cols_ref, ds_ref, up_scr, cv_scr = refs[12:]
    else:
        rows_ref, ds_ref, up_scr, cv_scr = refs[12:]
    nseg, seg, d = x_ref.shape
    tm = nseg * seg
    nch = tm // CHUNK
    cps = seg // CHUNK
    j = pl.program_id(1)
    nj = pl.num_programs(1)

    def hmod(v):
        return _rms(v, g_ref[...]) * (1.0 + scale_ref[0]) + shift_ref[0]

    xm = hmod(x_ref[...].reshape(tm, d))
    gap = jnp.zeros((HALO, d), F32)
    hm = xm.astype(BF16)
    if latent:
        hp = jnp.where(j > 0, hmod(xp_ref[0]), 0.0)
        hn = jnp.where(j < nj - 1, hmod(xn_ref[0]), 0.0)
        blocks = [(jnp.concatenate([hp, xm, hn], axis=0).astype(BF16), 0)]
        convs = [(HALO, 0, tm)]
    else:
        pieces = [gap]
        for i in range(nseg):
            pieces += [xm[i * seg:(i + 1) * seg], gap]
        ext_rows = up_scr.shape[1]
        if ext_rows > tm + (nseg + 1) * HALO:
            pieces.append(jnp.zeros((ext_rows - tm - (nseg + 1) * HALO, d), F32))
        blocks = [(jnp.concatenate(pieces, axis=0).astype(BF16), 0)]
        convs = [(HALO + i * (seg + HALO), i * seg, seg) for i in range(nseg)]

    def zero_of(v):
        u = pltpu.bitcast(v[0:8, 0:LANES], jnp.uint32)
        u = lax.shift_right_logical(lax.shift_right_logical(u, jnp.uint32(16)), jnp.uint32(16))
        return pltpu.bitcast(u, F32)[0:1, :]

    ties = {}
    if latent:
        pd = jnp.dot(hm, wpd_ref[...], preferred_element_type=F32)
        zu_ref[0, :, D_SSM:D_SSM + D_POOL] = pd[:, 0:D_POOL].astype(BF16)
        dtraw = pd[:, D_POOL:D_POOL + LANES]
        for hz in range(2):
            zc = slice(hz * 2 * LANES, (hz + 1) * 2 * LANES)
            zp = jnp.dot(hm, w_ref[:, zc], preferred_element_type=F32)
            zu_ref[0, :, zc] = zp.astype(BF16)
            ties[conv_cols // LANES - 2 + hz] = zero_of(zp)
    else:
        dtraw = jnp.dot(hm, wpd_ref[:, D_POOL:D_POOL + LANES], preferred_element_type=F32)

    half = SSD_CONV // 2
    cc = 2 * LANES
    for hblk, srow in blocks:
        for cb in range(conv_cols // cc):
            up = jnp.dot(hblk, w_ref[:, D_SSM + cb * cc:D_SSM + (cb + 1) * cc], preferred_element_type=F32)
            for s in range(cc // LANES):
                up_scr[cb * (cc // LANES) + s, srow:srow + hblk.shape[0], :] = up[:, s * LANES:(s + 1) * LANES]
    for slab in range(conv_cols // LANES):
        lanes = slice(slab * LANES, (slab + 1) * LANES)
        for drow, orow, nrows in convs:
            for par in range(2):
                lo = drow - half + par
                acc = cb_ref[:, lanes] + cw_ref[0:1, lanes] * up_scr[slab, pl.ds(lo, nrows // 2, stride=2), :]
                for k in range(1, SSD_CONV):
                    acc = acc + cw_ref[k:k + 1, lanes] * up_scr[slab, pl.ds(lo + k, nrows // 2, stride=2), :]
                if par == 0 and slab in ties:
                    acc = acc + ties[slab]
                cv_scr[slab, pl.ds(orow + par, nrows // 2, stride=2), :] = _silu(acc)
        if latent:
            if slab < D_SSM // LANES:
                xs_ref[0, :, lanes] = cv_scr[slab].astype(BF16)
            else:
                bc_ref[0, :, slab * LANES - D_SSM:(slab + 1) * LANES - D_SSM] = cv_scr[slab].astype(BF16)

    rows = _dt_rows([dtraw[c * CHUNK:(c + 1) * CHUNK] for c in range(nch)], bias_ref[...],
                    -jnp.exp(alog_ref[...]))
    pad = jnp.zeros((LANES - 2 * SSD_HEADS, CHUNK), F32)
    for c in range(nch):
        lanes = slice(c * CHUNK, (c + 1) * CHUNK)
        for q, arr in enumerate(rows):
            rows_ref[c // cps, c % cps, q] = arr[:, lanes]
        if latent:
            cols_ref[0, c] = jnp.concatenate([rows[0][:, lanes], rows[1][:, lanes], pad], axis=0).T

    def state_body(c, carry):
        r0 = pl.multiple_of(c * CHUNK, CHUNK)
        ci, cj = c // cps, c % cps
        xs = jnp.concatenate([cv_scr[s, pl.ds(r0, CHUNK), :] for s in range(D_SSM // LANES)],
                             axis=1).astype(BF16)
        bts = [cv_scr[D_SSM // LANES + g, pl.ds(r0, CHUNK), :].T for g in range(SSD_GROUPS)]
        dsf, dsb = _local_states(xs, bts, rows_ref[ci, cj, 5], rows_ref[ci, cj, 6])
        for p in range(SSD_PAIRS):
            ds_ref[ci, cj, 0, p] = dsf[p].astype(BF16)
            ds_ref[ci, cj, 1, p] = dsb[p].astype(BF16)
        return carry

    if latent:
        for c in range(nch):
            state_body(c, 0)
    else:
        lax.fori_loop(0, nch, state_body, 0, unroll=4)


def _mix_in(x, mods, mod_row, gain, w, wpd, conv_w, conv_b, bias, alog, tm, latent, name):
    b, n, d = x.shape
    conv_cols = conv_w.shape[1]
    nch = tm // CHUNK
    nseg = 1 if latent else tm // n
    seg = tm // nseg
    cps = seg // CHUNK
    ext_rows = -(-(tm + (nseg + 1) * HALO) // 16) * 16
    full = lambda a: pl.BlockSpec(a.shape, lambda i, j: (0,) * a.ndim)
    stat = lambda tail, dt: jax.ShapeDtypeStruct((b, n // CHUNK) + tail, dt)
    out_specs, out_shape = [], []
    if latent:
        grid = (b, n // tm)
        x_spec = pl.BlockSpec((1, tm, d), lambda i, j: (i, j, 0))
        out_specs += [pl.BlockSpec((1, tm, D_SSM), lambda i, j: (i, j, 0)),
                      pl.BlockSpec((1, tm, conv_cols - D_SSM), lambda i, j: (i, j, 0)),
                      pl.BlockSpec((1, tm, D_SSM + D_POOL), lambda i, j: (i, j, 0))]
        out_shape += [jax.ShapeDtypeStruct((b, n, D_SSM), BF16),
                      jax.ShapeDtypeStruct((b, n, conv_cols - D_SSM), BF16),
                      jax.ShapeDtypeStruct((b, n, D_SSM + D_POOL), BF16)]
        blk = lambda tail: pl.BlockSpec((1, nch) + tail, lambda i, j: (i, j) + (0,) * len(tail))
    else:
        grid = (b // nseg, 1)
        x_spec = pl.BlockSpec((nseg, n, d), lambda i, j: (i, 0, 0))
        blk = lambda tail: pl.BlockSpec((nseg, cps) + tail, lambda i, j: (i, 0) + (0,) * len(tail))
    out_specs.append(blk((N_ROWS, SSD_HEADS, CHUNK)))
    out_shape.append(stat((N_ROWS, SSD_HEADS, CHUNK), F32))
    if latent:
        out_specs.append(blk((CHUNK, LANES)))
        out_shape.append(stat((CHUNK, LANES), F32))
    out_specs.append(blk((2, SSD_PAIRS, SSD_STATE, LANES)))
    out_shape.append(stat((2, SSD_PAIRS, SSD_STATE, LANES), BF16))
    return pl.pallas_call(
        functools.partial(_mix_in_kernel, conv_cols=conv_cols, latent=latent),
        grid=grid,
        in_specs=[x_spec] + _halo_specs(seg, n, d) + [
            _mod_spec(d, 0, mod_row), _mod_spec(d, 1, mod_row),
            full(gain), full(w), full(wpd), full(conv_w), full(conv_b), full(bias), full(alog)],
        out_specs=out_specs,
        out_shape=out_shape,
        scratch_shapes=[pltpu.VMEM((conv_cols // LANES, ext_rows, LANES), F32),
                        pltpu.VMEM((conv_cols // LANES, tm, LANES), F32)],
        compiler_params=pltpu.CompilerParams(dimension_semantics=("arbitrary", "arbitrary"),
                                             vmem_limit_bytes=VMEM_LIMIT),
        name=name,
    )(x, x, x, mods, mods, gain, w, wpd, conv_w, conv_b, bias, alog)


def _mix_out_kernel(xs_ref, bc_ref, z_ref, u_ref, rows_ref, cols_ref, ds_ref, cds_ref, crows_ref,
                    x_ref, gate_ref, gn_ref, dsk_ref, gssm_ref, wpool_ref, pscale_ref, wout_ref,
                    cm_ref, inv_ref, *rest):
    ncast = (len(rest) - 4) // 2
    o_ref = rest[ncast]
    sf_scr, sb_scr, d_scr = rest[2 * ncast + 1:]
    n = u_ref.shape[1]
    tm = x_ref.shape[1]
    nchunks = n // CHUNK
    nctx = cds_ref.shape[1]
    tch = tm // CHUNK
    j = pl.program_id(1)
    lane_row = lax.broadcasted_iota(jnp.int32, (1, LANES), 1)

    def cd_row(ref, c, q, p):
        cd = ref[0, c, q]
        return _pair_select(lane_row, cd[2 * p:2 * p + 1, :], cd[2 * p + 1:2 * p + 2, :])

    @pl.when(j == 0)
    def _():
        for p in range(SSD_PAIRS):
            sf = cds_ref[0, 0, 0, p].astype(F32)
            for c in range(1, nctx):
                sf = sf * cd_row(crows_ref, c, 7, p) + cds_ref[0, c, 0, p].astype(F32)
            sb = cds_ref[0, nctx - 1, 1, p].astype(F32)
            for c in reversed(range(nctx - 1)):
                sb = sb * cd_row(crows_ref, c, 8, p) + cds_ref[0, c, 1, p].astype(F32)
            sf_scr[0, p] = sf
            sb_scr[nchunks - 1, p] = sb

        def scan_body(i, carry):
            cf = i
            cbk = nchunks - 1 - i
            for p in range(SSD_PAIRS):
                sf_scr[cf + 1, p] = (sf_scr[cf, p] * cd_row(rows_ref, cf, 7, p)
                                     + ds_ref[0, cf, 0, p].astype(F32))
                sb_scr[cbk - 1, p] = (sb_scr[cbk, p] * cd_row(rows_ref, cbk, 8, p)
                                      + ds_ref[0, cbk, 1, p].astype(F32))
            return carry

        lax.fori_loop(0, nchunks - 1, scan_body, 0)

        for gi, w in enumerate(POOL_WINDOWS):
            cols = slice(gi * LANES, (gi + 1) * LANES)
            cmat = cm_ref[gi]
            parts = [jnp.dot(cmat, u_ref[0, i * LANES:(i + 1) * LANES, cols], preferred_element_type=F32)
                     for i in range(n // LANES)]
            cs = jnp.concatenate(parts, axis=0)
            h = jnp.concatenate([jnp.zeros((w // 2 * GRID_W, LANES), F32), cs], axis=0)
            k = 1
            while k < w:
                sh = k * GRID_W
                h = h + jnp.concatenate([h[sh:], jnp.zeros((sh, LANES), F32)], axis=0)
                k *= 2
            m = h[0:n] * inv_ref[gi]
            d_scr[:, cols] = (m - u_ref[0, :, cols].astype(F32)).astype(BF16)

    row_i = lax.broadcasted_iota(jnp.int32, (CHUNK, CHUNK), 0)
    col_i = lax.broadcasted_iota(jnp.int32, (CHUNK, CHUNK), 1)
    lane_full = lax.broadcasted_iota(jnp.int32, (CHUNK, LANES), 1)

    yns = []
    for lc in range(tch):
        c = j * tch + lc
        r0 = lc * CHUNK
        xs = xs_ref[0, pl.ds(r0, CHUNK), :]
        pfp, abp, ldg = rows_ref[0, c, 2], rows_ref[0, c, 3], rows_ref[0, c, 4]
        cols = cols_ref[0, c]

        y_pairs = []
        for g in range(SSD_GROUPS):
            bg = bc_ref[0, pl.ds(r0, CHUNK), g * SSD_STATE:(g + 1) * SSD_STATE]
            cg = bc_ref[0, pl.ds(r0, CHUNK), (SSD_GROUPS + g) * SSD_STATE:(SSD_GROUPS + g + 1) * SSD_STATE]
            gmat = lax.dot_general(cg, bg, (((1,), (1,)), ((), ())), preferred_element_type=F32)
            scat = jnp.concatenate([sf_scr[c, 2 * g].astype(BF16), sf_scr[c, 2 * g + 1].astype(BF16),
                                    sb_scr[c, 2 * g].astype(BF16), sb_scr[c, 2 * g + 1].astype(BF16)], axis=1)
            yoff = jnp.dot(cg, scat, preferred_element_type=F32)
            for q in range(2):
                p = 2 * g + q
                ms, efs, ebs = [], [], []
                for h in (2 * p, 2 * p + 1):
                    pf_c = jnp.broadcast_to(cols[:, h:h + 1], (CHUNK, CHUNK))
                    ab_c = jnp.broadcast_to(cols[:, SSD_HEADS + h:SSD_HEADS + h + 1], (CHUNK, CHUNK))
                    dm = jnp.where(row_i > col_i, pf_c - pfp[h:h + 1, :],
                                   jnp.where(row_i < col_i, ab_c - abp[h:h + 1, :], ldg[h:h + 1, :]))
                    ms.append((gmat * jnp.exp2(dm)).astype(BF16))
                    efs.append(jnp.exp2(pf_c))
                    ebs.append(jnp.exp2(ab_c))
                ydiag = jnp.dot(jnp.concatenate(ms, axis=1), _pair_rhs(xs, p), preferred_element_type=F32)
                y_pairs.append(ydiag
                               + _pair_select(lane_full, efs[0], efs[1]) * yoff[:, q * LANES:(q + 1) * LANES]
                               + _pair_select(lane_full, ebs[0], ebs[1]) * yoff[:, (2 + q) * LANES:(3 + q) * LANES])
        y = jnp.concatenate(y_pairs, axis=1) + dsk_ref[...] * xs.astype(F32)
        yz = y * _silu(z_ref[0, pl.ds(r0, CHUNK), :].astype(F32))
        yns.append(_rms(yz, gssm_ref[...]).astype(BF16))

    yn = jnp.concatenate(yns, axis=0)
    trows = pl.ds(pl.multiple_of(j * tm, tm), tm)
    ps = [jnp.dot(d_scr[trows, gi * LANES:(gi + 1) * LANES], wpool_ref[gi], preferred_element_type=F32)
          for gi in range(len(POOL_WINDOWS))]
    pm = (jnp.concatenate(ps, axis=1) * pscale_ref[...]).astype(BF16)
    oc = 2 * LANES
    yx = jnp.concatenate(
        [jnp.dot(yn, wout_ref[0:D_SSM, nt * oc:(nt + 1) * oc], preferred_element_type=F32)
         + jnp.dot(pm, wout_ref[D_SSM:D_SSM + D_POOL, nt * oc:(nt + 1) * oc], preferred_element_type=F32)
         for nt in range(wout_ref.shape[1] // oc)], axis=1)
    o_ref[0] = x_ref[0] + gate_ref[0] * _rms(yx, gn_ref[...])

    for src, dst in zip(rest[:ncast], rest[ncast + 1:2 * ncast + 1]):
        dst[...] = src[...].astype(BF16)


def _pool_constants(n):
    rows = n // GRID_W
    t = np.arange(LANES)
    blk, col = t // GRID_W, t % GRID_W
    cms, invs = [], []
    r = np.arange(n) // GRID_W
    c = np.arange(n) % GRID_W
    for w in POOL_WINDOWS:
        d = col[None, :] - col[:, None]
        cms.append(((blk[:, None] == blk[None, :]) & (d >= -(w // 2)) & (d < w - w // 2)).astype(np.float32))
        cnt_r = np.minimum(r + w - w // 2, rows) - np.maximum(r - w // 2, 0)
        cnt_c = np.minimum(c + w - w // 2, GRID_W) - np.maximum(c - w // 2, 0)
        invs.append(np.broadcast_to((1.0 / (cnt_r * cnt_c))[:, None], (n, LANES)).astype(np.float32))
    return np.stack(cms), np.stack(invs)


def _mix_out(xs, bc, zu, rows, cols, ds, cds, crows, x, mods, gn, dsk, gssm, wpool, pscale, wout, casts, tm):
    b, n, d = x.shape
    nchunks = n // CHUNK
    nj = n // tm
    cm_np, inv_np = _pool_constants(n)
    cm = jnp.asarray(cm_np, BF16)
    inv = jnp.asarray(inv_np, F32)
    full = lambda a: pl.BlockSpec(a.shape, lambda i, j: (0,) * a.ndim)
    per_batch = lambda a: pl.BlockSpec((1,) + a.shape[1:], lambda i, j: (i,) + (0,) * (a.ndim - 1))
    cast_specs = []
    for a in casts:
        rps, last = _cast_blocks(a, b * nj)
        cast_specs.append(pl.BlockSpec((rps, a.shape[1]), lambda i, j, last=last: (jnp.minimum(i * nj + j, last), 0)))
    outs = pl.pallas_call(
        _mix_out_kernel,
        grid=(b, n // tm),
        in_specs=[pl.BlockSpec((1, tm, D_SSM), lambda i, j: (i, j, 0)),
                  pl.BlockSpec((1, tm, D_BC), lambda i, j: (i, j, 0)),
                  pl.BlockSpec((1, tm, D_SSM), lambda i, j: (i, j, 0)),
                  pl.BlockSpec((1, n, D_POOL), lambda i, j: (i, 0, 1)),
                  per_batch(rows), per_batch(cols), per_batch(ds), per_batch(cds), per_batch(crows),
                  pl.BlockSpec((1, tm, d), lambda i, j: (i, j, 0)),
                  _mod_spec(d, 2),
                  full(gn), full(dsk), full(gssm), full(wpool), full(pscale), full(wout), full(cm),
                  pl.BlockSpec(inv.shape, lambda i, j: (0, 0, 0), pipeline_mode=pl.Buffered(1))] + cast_specs,
        out_specs=[pl.BlockSpec((1, tm, d), lambda i, j: (i, j, 0))] + cast_specs,
        out_shape=[jax.ShapeDtypeStruct((b, n, d), F32)] + [jax.ShapeDtypeStruct(a.shape, BF16) for a in casts],
        scratch_shapes=[pltpu.VMEM((nchunks, SSD_PAIRS, SSD_STATE, LANES), F32),
                        pltpu.VMEM((nchunks, SSD_PAIRS, SSD_STATE, LANES), F32),
                        pltpu.VMEM((n, D_POOL), BF16)],
        compiler_params=pltpu.CompilerParams(dimension_semantics=("arbitrary", "arbitrary"),
                                             vmem_limit_bytes=VMEM_LIMIT),
        name="mix_out",
    )(xs, bc, zu, zu, rows, cols, ds, cds, crows, x, mods, gn, dsk, gssm, wpool, pscale, wout, cm, inv, *casts)
    return outs[0], outs[1:]


FFN_COLS = 2 * LANES
FFN_RING = 2


def _ffn_kernel(x_ref, xp_ref, xn_ref, shift_ref, scale_ref, gate_ref, gpre_ref, gpost_ref,
                wup_ref, cw_ref, cb_ref, wdown_ref, o_ref, act_scr, up_scr, af_scr):
    tm = x_ref.shape[1]
    dff = wdown_ref.shape[0]
    j = pl.program_id(1)
    nj = pl.num_programs(1)

    def hmod(v):
        return _rms(v, gpre_ref[...]) * (1.0 + scale_ref[0]) + shift_ref[0]

    x = x_ref[0]
    hp = jnp.where(j > 0, hmod(xp_ref[0]), 0.0)
    hn = jnp.where(j < nj - 1, hmod(xn_ref[0]), 0.0)
    hext = jnp.concatenate([hp, hmod(x), hn], axis=0).astype(BF16)

    nsl = FFN_COLS // LANES
    for cidx in range(dff // FFN_COLS):
        slot = cidx % FFN_RING
        conv = []
        for half, base in enumerate((cidx * FFN_COLS, dff + cidx * FFN_COLS)):
            up = jnp.dot(hext, wup_ref[:, base:base + FFN_COLS], preferred_element_type=F32)
            for s in range(nsl):
                lanes = slice(base + s * LANES, base + (s + 1) * LANES)
                slab = (2 * slot + half) * nsl + s
                up_scr[slab] = up[:, s * LANES:(s + 1) * LANES]
                conv.append([cb_ref[:, lanes] + sum(
                    cw_ref[k:k + 1, lanes] * up_scr[slab, pl.ds(HALO - 1 + par + k, tm // 2, stride=2), :]
                    for k in range(3)) for par in range(2)])
        for s in range(nsl):
            for par in range(2):
                af_scr[slot * nsl + s, pl.ds(par, tm // 2, stride=2), :] = (
                    _silu(conv[nsl + s][par]) * conv[s][par])
            lo = cidx * FFN_COLS + s * LANES
            act_scr[:, lo:lo + LANES] = af_scr[slot * nsl + s].astype(BF16)

    f = jnp.dot(act_scr[...], wdown_ref[...], preferred_element_type=F32)
    o_ref[0] = x + gate_ref[0] * _rms(f, gpost_ref[...])


def _ffn(x, mods, gpre, gpost, wup, cw, cb, wdown, tm):
    b, n, d = x.shape
    dff = wdown.shape[0]
    return pl.pallas_call(
        _ffn_kernel,
        grid=(b, n // tm),
        in_specs=[pl.BlockSpec((1, tm, d), lambda i, j: (i, j, 0))] + _halo_specs(tm, n, d) + [
            _mod_spec(d, 3), _mod_spec(d, 4), _mod_spec(d, 5),
            pl.BlockSpec((1, d), lambda i, j: (0, 0)),
            pl.BlockSpec((1, d), lambda i, j: (0, 0)),
            pl.BlockSpec(wup.shape, lambda i, j: (0, 0), pipeline_mode=pl.Buffered(1)),
            pl.BlockSpec(cw.shape, lambda i, j: (0, 0)),
            pl.BlockSpec(cb.shape, lambda i, j: (0, 0)),
            pl.BlockSpec(wdown.shape, lambda i, j: (0, 0), pipeline_mode=pl.Buffered(1))],
        out_specs=pl.BlockSpec((1, tm, d), lambda i, j: (i, j, 0)),
        out_shape=jax.ShapeDtypeStruct((b, n, d), F32),
        scratch_shapes=[pltpu.VMEM((tm, dff), BF16),
                        pltpu.VMEM((2 * FFN_RING * FFN_COLS // LANES, tm + 2 * HALO, LANES), F32),
                        pltpu.VMEM((FFN_RING * FFN_COLS // LANES, tm, LANES), F32)],
        compiler_params=pltpu.CompilerParams(dimension_semantics=("arbitrary", "arbitrary"),
                                             vmem_limit_bytes=VMEM_LIMIT),
        name="ffn",
    )(x, x, x, mods, mods, mods, gpre, gpost, wup, cw, cb, wdown)


def kernel(x, c, ctx, c_ctx, pre_norm_mix, post_norm_mix, pre_norm_ffn, post_norm_ffn, w_ada, b_ada,
           w_in, conv_ssd_w, conv_ssd_b, dt_bias, a_log, d_skip, ssm_norm, w_pool, pool_scale, w_out,
           w_up, conv_ffn_w, conv_ffn_b, w_down):
    assert w_ada.shape[0] == 1, "single-layer block"
    b, n, d = x.shape
    d_xb = D_SSM + D_BC // 2
    off_dt = 2 * D_SSM + D_BC
    off_pool = off_dt + 2 * SSD_HEADS
    tm_in = 1024

    assert b < ADA_ROWS and w_ada.shape[2] == N_MODS * d
    cc = jnp.concatenate([c, c_ctx[None, :], jnp.zeros((ADA_ROWS - b - 1, d), F32)], axis=0)
    mods = _ada(cc, w_ada[0], b_ada)

    wi = w_in[0]
    w_all = wi.astype(BF16)
    w_pd = jnp.concatenate([wi[:, off_pool:], wi[:, off_dt:off_pool],
                            jnp.zeros((d, LANES - 2 * SSD_HEADS), F32)], axis=1).astype(BF16)
    bias = jnp.broadcast_to(dt_bias[0].reshape(2 * SSD_HEADS, 1), (2 * SSD_HEADS, tm_in))
    alog = jnp.broadcast_to(a_log[0].reshape(2 * SSD_HEADS, 1), (2 * SSD_HEADS, tm_in))

    xs, bc, zu, rows, cols, ds = _mix_in(x, mods, None, pre_norm_mix, w_all, w_pd, conv_ssd_w[0],
                                         conv_ssd_b, bias, alog, tm_in, True, "mix_in")
    crows, cds = _mix_in(ctx, mods, b, pre_norm_mix, w_all, w_pd, conv_ssd_w[0, :, :d_xb],
                         conv_ssd_b[:, :d_xb], bias, alog, tm_in, False, "mix_in_ctx")

    dsk = jnp.repeat(d_skip[0], SSD_HEAD_DIM)[None, :]
    x1, (w_up_b, w_down_b) = _mix_out(xs, bc, zu, rows, cols, ds, cds, crows, x, mods, post_norm_mix, dsk,
                                      ssm_norm, w_pool[0].astype(BF16), pool_scale, w_out[0].astype(BF16),
                                      [w_up[0], w_down[0]], 512)

    return _ffn(x1, mods, pre_norm_ffn, post_norm_ffn, w_up_b, conv_ffn_w[0], conv_ffn_b, w_down_b, 1024)
```
